```python
import math
import jax, jax.numpy as jnp
from jax import lax
import numpy as np

D_MODEL = 2048
BATCH = 1
SEQ = 8192
DEPTH = 2

N_A = max(1, DEPTH // 2)
N_B = DEPTH - N_A
N_META = 16
CONV_WIDTH = 31
HEAD_DIM = 64
N_HEADS = D_MODEL // HEAD_DIM
N_KV_HEADS = max(1, N_HEADS // 8)
GQA_GROUP = N_HEADS // N_KV_HEADS
WINDOW = 128
BLOCK = 128
ROT_DIM = HEAD_DIM // 4
ROPE_THETA = 500000.0
N_GROUPS = 4
EXPERTS_PER_GROUP = 8
N_EXPERTS = N_GROUPS * EXPERTS_PER_GROUP
TOP_K = 2
EXPERT_FF = D_MODEL // 8
ALPHA = (2.0 * DEPTH) ** 0.25
BETA = (8.0 * DEPTH) ** -0.25
LN_EPS = 1e-5

kernel_name = "yoco_conformer_swa_sinks_hier_moe"


def layer_norm(x, g, b):
    xf = x.astype(jnp.float32)
    mu = jnp.mean(xf, axis=-1, keepdims=True)
    var = jnp.mean(jnp.square(xf - mu), axis=-1, keepdims=True)
    y = (xf - mu) * lax.rsqrt(var + LN_EPS) * g.astype(jnp.float32) + b.astype(jnp.float32)
    return y.astype(x.dtype)


def rope_partial(x, pos):
    half = ROT_DIM // 2
    inv_freq = ROPE_THETA ** (-jnp.arange(0, ROT_DIM, 2, dtype=jnp.float32) / ROT_DIM)
    ang = pos.astype(jnp.float32)[:, None] * inv_freq[None, :]
    cos = jnp.cos(ang)[None, :, None, :].astype(x.dtype)
    sin = jnp.sin(ang)[None, :, None, :].astype(x.dtype)
    x1 = x[..., :half]
    x2 = x[..., half:ROT_DIM]
    return jnp.concatenate([x1 * cos - x2 * sin, x2 * cos + x1 * sin, x[..., ROT_DIM:]], axis=-1)


def conformer_conv(h, w_in, b_in, w_dw, b_dw, ln_g, ln_b, w_out, b_out):
    d = h.shape[-1]
    u = h @ w_in + b_in
    y = u[..., :d] * jax.nn.sigmoid(u[..., d:])
    y = lax.conv_general_dilated(
        y, w_dw[:, None, :], window_strides=(1,), padding=[(CONV_WIDTH - 1, 0)],
        dimension_numbers=("NWC", "WIO", "NWC"), feature_group_count=d) + b_dw
    y = layer_norm(y, ln_g, ln_b)
    y = jax.nn.silu(y)
    return y @ w_out + b_out


def swa_with_sinks(q, k, v, sinks):
    B, L = q.shape[0], q.shape[1]
    lead = (-N_META) % BLOCK
    Lp = L + lead
    nb = Lp // BLOCK
    pad = ((0, 0), (lead, 0), (0, 0), (0, 0))
    qb = jnp.pad(q, pad).reshape(B, nb, BLOCK, N_KV_HEADS, GQA_GROUP, HEAD_DIM)
    kc = jnp.pad(k, pad).reshape(B, nb, BLOCK, N_KV_HEADS, HEAD_DIM)
    vc = jnp.pad(v, pad).reshape(B, nb, BLOCK, N_KV_HEADS, HEAD_DIM)
    zero_blk = jnp.zeros_like(kc[:, :1])
    kb = jnp.concatenate([jnp.concatenate([zero_blk, kc[:, :-1]], axis=1), kc], axis=2)
    vb = jnp.concatenate([jnp.concatenate([zero_blk, vc[:, :-1]], axis=1), vc], axis=2)
    k_meta = k[:, :N_META]
    v_meta = v[:, :N_META]

    base = jnp.arange(nb)[:, None] * BLOCK - lead
    qpos = base + jnp.arange(BLOCK)[None, :]
    kpos = base + jnp.arange(-BLOCK, BLOCK)[None, :]
    band_mask = ((kpos[:, None, :] <= qpos[:, :, None])
                 & (qpos[:, :, None] - kpos[:, None, :] < WINDOW)
                 & (kpos[:, None, :] >= N_META))
    meta_mask = jnp.arange(N_META)[None, None, :] <= qpos[:, :, None]

    scale = 1.0 / math.sqrt(HEAD_DIM)
    s_band = jnp.einsum("bnqhgd,bnkhd->bnhgqk", qb, kb).astype(jnp.float32) * scale
    s_meta = jnp.einsum("bnqhgd,bmhd->bnhgqm", qb, k_meta).astype(jnp.float32) * scale
    s_band = jnp.where(band_mask[None, :, None, None], s_band, -jnp.inf)
    s_meta = jnp.where(meta_mask[None, :, None, None], s_meta, -jnp.inf)
    sink = jnp.broadcast_to(sinks.astype(jnp.float32).reshape(1, 1, N_KV_HEADS, GQA_GROUP, 1, 1),
                            s_band.shape[:-1] + (1,))
    p = jax.nn.softmax(jnp.concatenate([s_band, s_meta, sink], axis=-1), axis=-1)
    p_band = p[..., :2 * BLOCK].astype(v.dtype)
    p_meta = p[..., 2 * BLOCK:2 * BLOCK + N_META].astype(v.dtype)
    o = (jnp.einsum("bnhgqk,bnkhd->bnqhgd", p_band, vb)
         + jnp.einsum("bnhgqm,bmhd->bnqhgd", p_meta, v_meta))
    return o.reshape(B, Lp, N_HEADS, HEAD_DIM)[:, lead:]


def hier_moe(h, wg, bg, we, be, w1, w3, w2):
    B, L, D = h.shape
    xt = h.reshape(B * L, D)
    g_prob = jax.nn.softmax((xt @ wg + bg).astype(jnp.float32), axis=-1)
    g_w, g_idx = lax.top_k(g_prob, 1)
    e_logits = (xt @ we + be).astype(jnp.float32).reshape(-1, N_GROUPS, EXPERTS_PER_GROUP)
    e_sel = jnp.take_along_axis(e_logits, g_idx[:, :, None], axis=1)[:, 0]
    e_w, e_idx = lax.top_k(jax.nn.softmax(e_sel, axis=-1), TOP_K)
    e_w = e_w / jnp.sum(e_w, axis=-1, keepdims=True)
    flat = g_idx * EXPERTS_PER_GROUP + e_idx
    gate = jnp.sum(jax.nn.one_hot(flat, N_EXPERTS, dtype=jnp.float32) * (g_w * e_w)[..., None], axis=1)
    hid = (jax.nn.silu(jnp.einsum("td,edf->tef", xt, w1))
           * jnp.einsum("td,edf->tef", xt, w3) * gate[..., None].astype(xt.dtype))
    return jnp.einsum("tef,efd->td", hid, w2).reshape(B, L, D)


def setup_inputs(seed: int = 0) -> dict:
    key = jax.random.key(seed)
    ks = iter(jax.random.split(key, 40))
    D = D_MODEL
    kv_w = N_KV_HEADS * HEAD_DIM
    qw = N_HEADS * HEAD_DIM

    def nrm(shape, scale):
        return jax.random.normal(next(ks), shape, jnp.float32) * scale

    def gain(shape):
        return 1.0 + nrm(shape, 0.02)

    return {
        "x": nrm((BATCH, SEQ, D), 1.0),
        "meta_tokens": nrm((N_META, D), 1.0),
        "conv_w_in": nrm((N_A, D, 2 * D), D ** -0.5),
        "conv_b_in": nrm((N_A, 2 * D), 0.02),
        "conv_w_dw": nrm((N_A, CONV_WIDTH, D), CONV_WIDTH ** -0.5),
        "conv_b_dw": nrm((N_A, D), 0.02),
        "conv_ln_g": gain((N_A, D)),
        "conv_ln_b": nrm((N_A, D), 0.02),
        "conv_w_out": nrm((N_A, D, D), BETA * D ** -0.5),
        "conv_b_out": nrm((N_A, D), 0.02),
        "w_k": nrm((D, kv_w), D ** -0.5),
        "b_k": nrm((kv_w,), 0.02),
        "w_v": nrm((D, kv_w), BETA * D ** -0.5),
        "b_v": nrm((kv_w,), 0.02),
        "w_q": nrm((N_B, D, qw), D ** -0.5),
        "b_q": nrm((N_B, qw), 0.02),
        "w_o": nrm((N_B, qw, D), BETA * qw ** -0.5),
        "b_o": nrm((N_B, D), 0.02),
        "sinks": nrm((N_B, N_HEADS), 1.0),
        "ln_mix_g": gain((DEPTH, D)),
        "ln_mix_b": nrm((DEPTH, D), 0.02),
        "ln_ffn_g": gain((DEPTH, D)),
        "ln_ffn_b": nrm((DEPTH, D), 0.02),
        "router_group_w": nrm((DEPTH, D, N_GROUPS), D ** -0.5),
        "router_group_b": nrm((DEPTH, N_GROUPS), 0.01),
        "router_expert_w": nrm((DEPTH, D, N_EXPERTS), D ** -0.5),
        "router_expert_b": nrm((DEPTH, N_EXPERTS), 0.01),
        "expert_w1": nrm((DEPTH, N_EXPERTS, D, EXPERT_FF), D ** -0.5),
        "expert_w3": nrm((DEPTH, N_EXPERTS, D, EXPERT_FF), D ** -0.5),
        "expert_w2": nrm((DEPTH, N_EXPERTS, EXPERT_FF, D), BETA * EXPERT_FF ** -0.5),
    }


def reference(x, meta_tokens, conv_w_in, conv_b_in, conv_w_dw, conv_b_dw, conv_ln_g, conv_ln_b,
              conv_w_out, conv_b_out, w_k, b_k, w_v, b_v, w_q, b_q, w_o, b_o, sinks,
              ln_mix_g, ln_mix_b, ln_ffn_g, ln_ffn_b, router_group_w, router_group_b,
              router_expert_w, router_expert_b, expert_w1, expert_w3, expert_w2):
    B = x.shape[0]
    meta = jnp.broadcast_to(meta_tokens[None].astype(x.dtype), (B, N_META, D_MODEL))
    h = jnp.concatenate([meta, x], axis=1)
    L = h.shape[1]
    pos = jnp.arange(L)
    k_sh = None
    v_sh = None
    for layer in range(DEPTH):
        if layer < N_A:
            a = layer
            mix = conformer_conv(h, conv_w_in[a], conv_b_in[a], conv_w_dw[a], conv_b_dw[a],
                                 conv_ln_g[a], conv_ln_b[a], conv_w_out[a], conv_b_out[a])
        else:
            bi = layer - N_A
            q = rope_partial((h @ w_q[bi] + b_q[bi]).reshape(B, L, N_HEADS, HEAD_DIM), pos)
            att = swa_with_sinks(q, k_sh, v_sh, sinks[bi])
            mix = att.reshape(B, L, N_HEADS * HEAD_DIM) @ w_o[bi] + b_o[bi]
        h = layer_norm(ALPHA * h + mix, ln_mix_g[layer], ln_mix_b[layer])
        ffn = hier_moe(h, router_group_w[layer], router_group_b[layer], router_expert_w[layer],
                       router_expert_b[layer], expert_w1[layer], expert_w3[layer], expert_w2[layer])
        h = layer_norm(ALPHA * h + ffn, ln_ffn_g[layer], ln_ffn_b[layer])
        if layer == N_A - 1:
            k_sh = rope_partial((h @ w_k + b_k).reshape(B, L, N_KV_HEADS, HEAD_DIM), pos)
            v_sh = (h @ w_v + b_v).reshape(B, L, N_KV_HEADS, HEAD_DIM)
    return h[:, N_META:]
```

```python
import functools
import math

import jax
import jax.numpy as jnp
from jax import lax
from jax.experimental import pallas as pl
from jax.experimental.pallas import tpu as pltpu

f32 = jnp.float32
bf16 = jnp.bfloat16
i32 = jnp.int32

D = 2048
SEQ = 8192
DEPTH = 2
N_META = 16
CONV_W = 31
HEAD_DIM = 64
N_HEADS = 32
N_KV = 4
GQA = 8
KVW = N_KV * HEAD_DIM
WINDOW = 128
ROT = 16
ROPE_THETA = 500000.0
N_GROUPS = 4
EPG = 8
N_EXP = 32
FF = 256
ALPHA = (2.0 * DEPTH) ** 0.25
LN_EPS = 1e-5

LANES = 128
TM = 256
NXT = SEQ // TM
NT = NXT + 1
TP = NT * TM
META_ROW = SEQ
CHUNKS = D // LANES
HALO = 32
TME = 256
NTE = (2 * TP) // TME + N_EXP
NS = NTE * TME
QB = 128
VMEM_LIMIT = 52 * 1024 * 1024


def _cparams():
    return pltpu.CompilerParams(dimension_semantics=("arbitrary",), vmem_limit_bytes=VMEM_LIMIT)


def _resident(shape):
    nd = len(shape)
    return pl.BlockSpec(shape, lambda *a: (0,) * nd, pipeline_mode=pl.Buffered(1))


def _layer_norm(x, g, b):
    mu = jnp.mean(x, axis=-1, keepdims=True)
    xc = x - mu
    var = jnp.mean(xc * xc, axis=-1, keepdims=True)
    return xc * lax.rsqrt(var + LN_EPS) * g + b


def _x_or_meta(i, x_ref, meta_ref):
    return jnp.where(i == NXT, meta_ref[...], x_ref[...])


def _glu_kernel(x_ref, meta_ref, w_ref, b_ref, y_ref):
    i = pl.program_id(0)
    xb = _x_or_meta(i, x_ref, meta_ref).astype(bf16)
    cw = 512
    for c in range(D // cw):
        lo, hi = c * cw, (c + 1) * cw
        a = jnp.dot(xb, w_ref[:, lo:hi], preferred_element_type=f32) + b_ref[:, lo:hi]
        g = jnp.dot(xb, w_ref[:, D + lo:D + hi], preferred_element_type=f32) + b_ref[:, D + lo:D + hi]
        y_ref[:, lo:hi] = a * jax.nn.sigmoid(g)


def _glu(x2d, meta_pad, w_in, b_in):
    return pl.pallas_call(
        _glu_kernel,
        grid=(NT,),
        in_specs=[
            pl.BlockSpec((TM, D), lambda i: (jnp.minimum(i, NXT - 1), 0)),
            _resident((TM, D)),
            _resident((D, 2 * D)),
            _resident((1, 2 * D)),
        ],
        out_specs=pl.BlockSpec((TM, D), lambda i: (i, 0)),
        out_shape=jax.ShapeDtypeStruct((TP, D), f32),
        compiler_params=_cparams(),
        name="glu",
    )(x2d, meta_pad, w_in, b_in)


def _conv_kernel(y_ref, ymeta_ref, w_ref, bdw_ref, g_ref, b_ref, z_ref, scr, accs, u):
    i = pl.program_id(0)
    hrows = HALO * CHUNKS
    trows = TM * CHUNKS

    @pl.when(i == 0)
    def _():
        scr[0:(HALO - N_META) * CHUNKS, :] = jnp.zeros(((HALO - N_META) * CHUNKS, LANES), f32)
        for c in range(CHUNKS):
            scr[pl.ds((HALO - N_META) * CHUNKS + c, N_META, stride=CHUNKS), :] = \
                ymeta_ref[:, c * LANES:(c + 1) * LANES]

    @pl.when(i == NT - 1)
    def _():
        scr[0:hrows, :] = jnp.zeros((hrows, LANES), f32)

    @pl.when(jnp.logical_and(i > 0, i < NT - 1))
    def _():
        scr[0:hrows, :] = scr[trows:trows + hrows, :]

    for c in range(CHUNKS):
        scr[pl.ds(hrows + c, TM, stride=CHUNKS), :] = y_ref[:, c * LANES:(c + 1) * LANES]

    tb = 8
    brows = tb * CHUNKS
    first = (HALO - (CONV_W - 1)) * CHUNKS

    def block(t, carry):
        base = pl.multiple_of(t * brows, brows)
        acc = jnp.zeros((tb, CHUNKS, LANES), f32)
        for j in range(CONV_W):
            sl = scr[pl.ds(base + first + j * CHUNKS, brows), :].reshape(tb, CHUNKS, LANES)
            acc = acc + sl * w_ref[j][None]
        accs[pl.ds(base, brows), :] = acc.reshape(brows, LANES)
        return carry

    lax.fori_loop(0, TM // tb, block, 0)

    for c in range(CHUNKS):
        u[:, c * LANES:(c + 1) * LANES] = accs[pl.ds(c, TM, stride=CHUNKS), :]
    v = _layer_norm(u[...] + bdw_ref[...], g_ref[...], b_ref[...])
    z_ref[...] = (v * jax.nn.sigmoid(v)).astype(bf16)


def _conv(y, w_dw3, b_dw, ln_g, ln_b):
    return pl.pallas_call(
        _conv_kernel,
        grid=(NT,),
        in_specs=[
            pl.BlockSpec((TM, D), lambda i: (i, 0)),
            pl.BlockSpec((N_META, D), lambda i: (META_ROW // N_META, 0)),
            _resident((CONV_W, CHUNKS, LANES)),
            _resident((1, D)),
            _resident((1, D)),
            _resident((1, D)),
        ],
        out_specs=pl.BlockSpec((TM, D), lambda i: (i, 0)),
        out_shape=jax.ShapeDtypeStruct((TP, D), bf16),
        scratch_shapes=[
            pltpu.VMEM(((TM + HALO) * CHUNKS, LANES), f32),
            pltpu.VMEM((TM * CHUNKS, LANES), f32),
            pltpu.VMEM((TM, D), f32),
        ],
        compiler_params=_cparams(),
        name="conv_ln_swish",
    )(y, y, w_dw3, b_dw, ln_g, ln_b)


R_EXP0 = 8


def _route(h, wr_ref, br_ref, running):
    logits = jnp.dot(h.astype(bf16), wr_ref[...], preferred_element_type=f32) + br_ref[...]
    lt = logits.T
    gl = [lt[k:k + 1, :] for k in range(N_GROUPS)]
    gm = functools.reduce(jnp.maximum, gl)
    gex = [jnp.exp(v - gm) for v in gl]
    gden = functools.reduce(lambda a, b: a + b, gex)
    gp = [v / gden for v in gex]
    best = gp[0]
    gi = jnp.zeros((1, TM), i32)
    for k in range(1, N_GROUPS):
        better = gp[k] > best
        gi = jnp.where(better, k, gi)
        best = jnp.where(better, gp[k], best)
    esel = lt[R_EXP0:R_EXP0 + EPG, :]
    for k in range(1, N_GROUPS):
        esel = jnp.where(gi == k, lt[R_EXP0 + EPG * k:R_EXP0 + EPG * (k + 1), :], esel)
    em = jnp.max(esel, axis=0, keepdims=True)
    eex = jnp.exp(esel - em)
    ep = eex / jnp.sum(eex, axis=0, keepdims=True)
    io8 = lax.broadcasted_iota(i32, (EPG, TM), 0)
    v1 = jnp.max(ep, axis=0, keepdims=True)
    i1 = jnp.min(jnp.where(ep == v1, io8, EPG), axis=0, keepdims=True)
    ep2 = jnp.where(io8 == i1, -1.0, ep)
    v2 = jnp.max(ep2, axis=0, keepdims=True)
    i2 = jnp.min(jnp.where(ep2 == v2, io8, EPG), axis=0, keepdims=True)
    s = v1 + v2
    gate0 = best * (v1 / s)
    gate1 = best * (v2 / s)
    f0 = gi * EPG + i1
    f1 = gi * EPG + i2

    io32 = lax.broadcasted_iota(i32, (N_EXP, TM), 0)
    oh0 = (io32 == f0).astype(f32)
    oh1 = (io32 == f1).astype(f32)
    cnt = oh0 + oh1
    upper = (lax.broadcasted_iota(i32, (TM, TM), 0) < lax.broadcasted_iota(i32, (TM, TM), 1))
    before = jnp.dot(cnt.astype(bf16), upper.astype(f32).astype(bf16), preferred_element_type=f32)
    base = running[...] + before
    r0 = jnp.sum(oh0 * base, axis=0, keepdims=True).astype(i32)
    r1 = jnp.sum(oh1 * base, axis=0, keepdims=True).astype(i32)
    running[...] = running[...] + jnp.sum(cnt, axis=1, keepdims=True)

    io128 = lax.broadcasted_iota(i32, (LANES, TM), 0)
    gcol = jnp.where(io128 == 0, gate0, jnp.where(io128 == 1, gate1, 0.0)).T
    return f0, f1, r0, r1, gcol


def _proj_ln_route_body(i, a_ref, res, w_ref, bias_ref, g_ref, b_ref, wr_ref, br_ref,
                        h_ref, eidx_ref, rank_ref, gcol_ref, cnt_ref, running):
    @pl.when(i == 0)
    def _():
        running[...] = jnp.zeros_like(running)

    mix = jnp.dot(a_ref[...], w_ref[...], preferred_element_type=f32) + bias_ref[...]
    h = _layer_norm(ALPHA * res + mix, g_ref[...], b_ref[...])
    h_ref[...] = h
    f0, f1, r0, r1, gcol = _route(h, wr_ref, br_ref, running)
    eidx_ref[0, 0:1, :] = f0
    eidx_ref[0, 1:2, :] = f1
    rank_ref[0, 0:1, :] = r0
    rank_ref[0, 1:2, :] = r1
    gcol_ref[...] = gcol
    cnt_ref[...] = running[...]


def _proj_ln_route_kernel_split(a_ref, x_ref, meta_ref, *rest):
    i = pl.program_id(0)
    _proj_ln_route_body(i, a_ref, _x_or_meta(i, x_ref, meta_ref), *rest)


def _proj_ln_route_kernel(a_ref, res_ref, *rest):
    _proj_ln_route_body(pl.program_id(0), a_ref, res_ref[...], *rest)


def _proj_ln_route(a, res, w, bias, ln_g, ln_b, wr, br):
    split = isinstance(res, tuple)
    kdim = a.shape[1]
    if split:
        res_specs = [pl.BlockSpec((TM, D), lambda i: (jnp.minimum(i, NXT - 1), 0)), _resident((TM, D))]
        res_args = list(res)
        body = _proj_ln_route_kernel_split
    else:
        res_specs = [pl.BlockSpec((TM, D), lambda i: (i, 0))]
        res_args = [res]
        body = _proj_ln_route_kernel
    tile3 = pl.BlockSpec((1, 2, TM), lambda i: (i, 0, 0))
    return pl.pallas_call(
        body,
        grid=(NT,),
        in_specs=[pl.BlockSpec((TM, kdim), lambda i: (i, 0))] + res_specs + [
            _resident((kdim, D)), _resident((1, D)), _resident((1, D)), _resident((1, D)),
            _resident((D, LANES)), _resident((1, LANES)),
        ],
        out_specs=[
            pl.BlockSpec((TM, D), lambda i: (i, 0)),
            tile3, tile3,
            pl.BlockSpec((TM, LANES), lambda i: (i, 0)),
            pl.BlockSpec((N_EXP, TM), lambda i: (0, 0)),
        ],
        out_shape=[
            jax.ShapeDtypeStruct((TP, D), f32),
            jax.ShapeDtypeStruct((NT, 2, TM), i32),
            jax.ShapeDtypeStruct((NT, 2, TM), i32),
            jax.ShapeDtypeStruct((TP, LANES), f32),
            jax.ShapeDtypeStruct((N_EXP, TM), f32),
        ],
        scratch_shapes=[pltpu.VMEM((N_EXP, TM), f32)],
        compiler_params=_cparams(),
        name="proj_ln_route",
    )(a, *res_args, w, bias, ln_g, ln_b, wr, br)


def _plan(eidx, rank, cnt):
    counts = cnt[:, 0].astype(i32)
    padded = ((counts + TME - 1) // TME) * TME
    ends = jnp.cumsum(padded)
    offs = ends - padded
    ntiles = ends[-1] // TME
    dest = (offs[eidx] + rank).reshape(-1)
    tile_start = jnp.minimum(jnp.arange(NTE, dtype=i32), ntiles - 1) * TME
    tile_expert = jnp.minimum(jnp.searchsorted(ends, tile_start, side="right"), N_EXP - 1).astype(i32)
    zstart = jnp.where(padded > 0, ends - TME, 0).astype(i32)
    zflag = (padded > 0).astype(i32)
    return dest.astype(i32), tile_expert, ntiles.reshape(1).astype(i32), zstart, zflag


def _scatter_kernel(dest_ref, zstart_ref, zflag_ref, h_ref, xs_hbm, zeros, sem, zsem):
    i = pl.program_id(0)

    @pl.when(i == 0)
    def _():
        zeros[...] = jnp.zeros_like(zeros)
        for e in range(N_EXP):
            @pl.when(zflag_ref[e] > 0)
            def _():
                start = pl.multiple_of(zstart_ref[e], TME)
                cp = pltpu.make_async_copy(zeros, xs_hbm.at[pl.ds(start, TME)], zsem)
                cp.start()
                cp.wait()

    base = i * (2 * TM)

    def row(r, carry):
        for k in range(2):
            d = dest_ref[base + k * TM + r]
            pltpu.make_async_copy(h_ref.at[pl.ds(r, 1)], xs_hbm.at[pl.ds(d, 1)], sem).start()
        return carry

    lax.fori_loop(0, TM, row, 0)

    def drain(r, carry):
        pltpu.make_async_copy(h_ref.at[pl.ds(0, 1)], xs_hbm.at[pl.ds(0, 1)], sem).wait()
        return carry

    lax.fori_loop(0, 2 * TM, drain, 0)


def _scatter(dest, zstart, zflag, h):
    return pl.pallas_call(
        _scatter_kernel,
        grid_spec=pltpu.PrefetchScalarGridSpec(
            num_scalar_prefetch=3,
            grid=(NT,),
            in_specs=[pl.BlockSpec((TM, D), lambda i, *_: (i, 0))],
            out_specs=pl.BlockSpec(memory_space=pl.ANY),
            scratch_shapes=[pltpu.VMEM((TME, D), f32), pltpu.SemaphoreType.DMA(()), pltpu.SemaphoreType.DMA(())],
        ),
        out_shape=jax.ShapeDtypeStruct((NS, D), f32),
        compiler_params=pltpu.CompilerParams(dimension_semantics=("arbitrary",), vmem_limit_bytes=VMEM_LIMIT,
                                             has_side_effects=True),
        name="moe_scatter",
    )(dest, zstart, zflag, h)


def _expert_kernel(te_ref, nt_ref, xs_ref, w1_ref, w3_ref, w2_ref, ys_ref):
    i = pl.program_id(0)

    @pl.when(i < nt_ref[0])
    def _():
        xb = xs_ref[...].astype(bf16)
        a = jnp.dot(xb, w1_ref[0].astype(bf16), preferred_element_type=f32)
        b = jnp.dot(xb, w3_ref[0].astype(bf16), preferred_element_type=f32)
        hid = (a * jax.nn.sigmoid(a) * b).astype(bf16)
        ys_ref[...] = jnp.dot(hid, w2_ref[0].astype(bf16), preferred_element_type=f32)


def _experts(tile_expert, ntiles, xs, w1, w3, w2):
    def row_map(i, te, nt):
        return (jnp.minimum(i, nt[0] - 1), 0)

    def w_map(i, te, nt):
        return (te[i], 0, 0)

    return pl.pallas_call(
        _expert_kernel,
        grid_spec=pltpu.PrefetchScalarGridSpec(
            num_scalar_prefetch=2,
            grid=(NTE,),
            in_specs=[
                pl.BlockSpec((TME, D), row_map),
                pl.BlockSpec((1, D, FF), w_map),
                pl.BlockSpec((1, D, FF), w_map),
                pl.BlockSpec((1, FF, D), w_map),
            ],
            out_specs=pl.BlockSpec((TME, D), row_map),
        ),
        out_shape=jax.ShapeDtypeStruct((NS, D), f32),
        compiler_params=_cparams(),
        name="moe_experts",
    )(tile_expert, ntiles, xs, w1, w3, w2)


def _combine_kernel(n, dest_ref, ys_hbm, h_ref, gcol_ref, g_ref, b_ref, o_ref, buf, sem):
    i = pl.program_id(0)
    slot = lax.rem(i, 2)

    def issue(tile, s):
        base = tile * (2 * TM)

        def row(r, carry):
            for k in range(2):
                d = dest_ref[base + k * TM + r]
                pltpu.make_async_copy(ys_hbm.at[pl.ds(d, 1)], buf.at[s, k, pl.ds(r, 1)], sem.at[s]).start()
            return carry

        lax.fori_loop(0, TM, row, 0)

    @pl.when(i == 0)
    def _():
        issue(0, 0)

    @pl.when(i + 1 < n)
    def _():
        issue(i + 1, 1 - slot)

    def drain(r, carry):
        pltpu.make_async_copy(ys_hbm.at[pl.ds(0, 1)], buf.at[slot, 0, pl.ds(0, 1)], sem.at[slot]).wait()
        return carry

    lax.fori_loop(0, 2 * TM, drain, 0)

    ffn = buf[slot, 0] * gcol_ref[:, 0:1] + buf[slot, 1] * gcol_ref[:, 1:2]
    o_ref[...] = _layer_norm(ALPHA * h_ref[...] + ffn, g_ref[...], b_ref[...])


def _combine(dest, ys, h, gcol, ln_g, ln_b, ntiles_out):
    return pl.pallas_call(
        functools.partial(_combine_kernel, ntiles_out),
        grid_spec=pltpu.PrefetchScalarGridSpec(
            num_scalar_prefetch=1,
            grid=(ntiles_out,),
            in_specs=[
                pl.BlockSpec(memory_space=pl.ANY),
                pl.BlockSpec((TM, D), lambda i, *_: (i, 0)),
                pl.BlockSpec((TM, LANES), lambda i, *_: (i, 0)),
                pl.BlockSpec((1, D), lambda i, *_: (0, 0)),
                pl.BlockSpec((1, D), lambda i, *_: (0, 0)),
            ],
            out_specs=pl.BlockSpec((TM, D), lambda i, *_: (i, 0)),
            scratch_shapes=[pltpu.VMEM((2, 2, TM, D), f32), pltpu.SemaphoreType.DMA((2,))],
        ),
        out_shape=jax.ShapeDtypeStruct((ntiles_out * TM, D), f32),
        compiler_params=_cparams(),
        name="moe_combine_ln",
    )(dest, ys, h, gcol, ln_g, ln_b)


def _moe(h, eidx, rank, gcol, cnt, w1, w3, w2, ln_g, ln_b, ntiles_out):
    dest, tile_expert, ntiles, zstart, zflag = _plan(eidx, rank, cnt)
    xs = _scatter(dest, zstart, zflag, h)
    ys = _experts(tile_expert, ntiles, xs, w1, w3, w2)
    return _combine(dest, ys, h, gcol, ln_g, ln_b, ntiles_out)


def _rope(t, cos, sa, sb):
    w = t.shape[1]
    reps = w // LANES
    c = jnp.tile(cos, (1, reps))
    a = jnp.tile(sa, (1, reps))
    b = jnp.tile(sb, (1, reps))
    return t * c + pltpu.roll(t, w - ROT // 2, 1) * a + pltpu.roll(t, ROT // 2, 1) * b


def _qkv_kernel(h_ref, w_ref, b_ref, cos_ref, sa_ref, sb_ref, q_ref, k_ref, v_ref):
    hb = h_ref[...].astype(bf16)
    cos, sa, sb = cos_ref[...], sa_ref[...], sb_ref[...]
    cw = 512
    scale = 1.0 / math.sqrt(HEAD_DIM)
    for c in range(D // cw):
        lo, hi = c * cw, (c + 1) * cw
        t = jnp.dot(hb, w_ref[:, lo:hi], preferred_element_type=f32) + b_ref[:, lo:hi]
        q_ref[:, lo:hi] = (_rope(t, cos, sa, sb) * scale).astype(bf16)
    t = jnp.dot(hb, w_ref[:, D:D + KVW], preferred_element_type=f32) + b_ref[:, D:D + KVW]
    k_ref[...] = _rope(t, cos, sa, sb).astype(bf16)
    t = jnp.dot(hb, w_ref[:, D + KVW:D + 2 * KVW], preferred_element_type=f32) + b_ref[:, D + KVW:D + 2 * KVW]
    v_ref[...] = t.astype(bf16)


def _qkv(h, wqkv, bqkv, cos_t, sa_t, sb_t):
    tab = pl.BlockSpec((TM, LANES), lambda i: (i, 0))
    return pl.pallas_call(
        _qkv_kernel,
        grid=(NT,),
        in_specs=[
            pl.BlockSpec((TM, D), lambda i: (i, 0)),
            _resident((D, D + 2 * KVW)),
            _resident((1, D + 2 * KVW)),
            tab, tab, tab,
        ],
        out_specs=[
            pl.BlockSpec((TM, D), lambda i: (i, 0)),
            pl.BlockSpec((TM, KVW), lambda i: (i, 0)),
            pl.BlockSpec((TM, KVW), lambda i: (i, 0)),
        ],
        out_shape=[
            jax.ShapeDtypeStruct((TP, D), bf16),
            jax.ShapeDtypeStruct((TP, KVW), bf16),
            jax.ShapeDtypeStruct((TP, KVW), bf16),
        ],
        compiler_params=_cparams(),
        name="qkv_rope",
    )(h, wqkv, bqkv, cos_t, sa_t, sb_t)


NKEY = 2 * QB + N_META


def _attn_kernel(q_ref, kc_ref, vc_ref, kp_ref, vp_ref, km_ref, vm_ref, sink_ref, o_ref):
    i = pl.program_id(0)
    is_meta = i == NT - 1
    rq = lax.broadcasted_iota(i32, (QB, NKEY), 0)
    ck = lax.broadcasted_iota(i32, (QB, NKEY), 1)
    rq = jnp.concatenate([rq] * GQA, axis=0)
    ck = jnp.concatenate([ck] * GQA, axis=0)
    in_band = jnp.logical_and(ck > rq, ck <= rq + QB)
    meta_col = ck >= 2 * QB
    meta_ok = jnp.logical_or(jnp.logical_not(is_meta), ck - 2 * QB <= rq)

    for blk in range(TM // QB):
        lo = jnp.where(is_meta, 2 * QB, jnp.where(jnp.logical_and(i == 0, blk == 0), QB, 0))
        mask = jnp.logical_or(jnp.logical_and(meta_col, meta_ok), jnp.logical_and(in_band, ck >= lo))
        r0 = blk * QB
        for g in range(N_KV):
            cs = slice(g * HEAD_DIM, (g + 1) * HEAD_DIM)
            if blk == 0:
                kprev, vprev = kp_ref[:, cs], vp_ref[:, cs]
            else:
                kprev, vprev = kc_ref[r0 - QB:r0, cs], vc_ref[r0 - QB:r0, cs]
            kcat = jnp.concatenate([kprev, kc_ref[r0:r0 + QB, cs], km_ref[:, cs]], axis=0)
            vcat = jnp.concatenate([vprev, vc_ref[r0:r0 + QB, cs], vm_ref[:, cs]], axis=0)
            qg = jnp.concatenate(
                [q_ref[r0:r0 + QB, (g * GQA + j) * HEAD_DIM:(g * GQA + j + 1) * HEAD_DIM] for j in range(GQA)],
                axis=0)
            s = lax.dot_general(qg, kcat, (((1,), (1,)), ((), ())), preferred_element_type=f32)
            s = jnp.where(mask, s, -jnp.inf)
            sink = sink_ref[g]
            m = jnp.maximum(jnp.max(s, axis=-1, keepdims=True), sink)
            p = jnp.exp(s - m)
            den = jnp.sum(p, axis=-1, keepdims=True) + jnp.exp(sink - m)
            pn = (p / den).astype(bf16)
            o = jnp.dot(pn, vcat, preferred_element_type=f32)
            for j in range(0, GQA, 2):
                pair = jnp.concatenate([o[j * QB:(j + 1) * QB], o[(j + 1) * QB:(j + 2) * QB]], axis=1)
                c0 = (g * GQA + j) * HEAD_DIM
                o_ref[r0:r0 + QB, c0:c0 + 2 * HEAD_DIM] = pair.astype(bf16)


def _attention(q, k, v, sink_rows):
    return pl.pallas_call(
        _attn_kernel,
        grid=(NT,),
        in_specs=[
            pl.BlockSpec((TM, D), lambda i: (i, 0)),
            pl.BlockSpec((TM, KVW), lambda i: (i, 0)),
            pl.BlockSpec((TM, KVW), lambda i: (i, 0)),
            pl.BlockSpec((QB, KVW), lambda i: (jnp.maximum(i * (TM // QB) - 1, 0), 0)),
            pl.BlockSpec((QB, KVW), lambda i: (jnp.maximum(i * (TM // QB) - 1, 0), 0)),
            pl.BlockSpec((N_META, KVW), lambda i: (META_ROW // N_META, 0)),
            pl.BlockSpec((N_META, KVW), lambda i: (META_ROW // N_META, 0)),
            _resident((N_KV, GQA * QB, 1)),
        ],
        out_specs=pl.BlockSpec((TM, D), lambda i: (i, 0)),
        out_shape=jax.ShapeDtypeStruct((TP, D), bf16),
        compiler_params=_cparams(),
        name="swa_attention",
    )(q, k, v, k, v, k, v, sink_rows)


def _router_weights(wg, bg, we, be):
    wr = jnp.zeros((D, LANES), f32).at[:, 0:N_GROUPS].set(wg).at[:, R_EXP0:R_EXP0 + N_EXP].set(we)
    br = jnp.zeros((1, LANES), f32).at[0, 0:N_GROUPS].set(bg).at[0, R_EXP0:R_EXP0 + N_EXP].set(be)
    return wr.astype(bf16), br


def _rope_tables():
    pos = jnp.concatenate([jnp.arange(SEQ) + N_META, jnp.arange(TM)]).astype(f32)
    half = ROT // 2
    inv_freq = ROPE_THETA ** (-jnp.arange(0, ROT, 2, dtype=f32) / ROT)
    ang = pos[:, None] * inv_freq[None, :]
    cos, sin = jnp.cos(ang), jnp.sin(ang)
    ones = jnp.ones((TP, HEAD_DIM - ROT), f32)
    zeros = jnp.zeros((TP, HEAD_DIM - ROT), f32)
    z8 = jnp.zeros((TP, half), f32)
    cos_h = jnp.concatenate([cos, cos, ones], axis=1)
    sa_h = jnp.concatenate([-sin, z8, zeros], axis=1)
    sb_h = jnp.concatenate([z8, sin, zeros], axis=1)
    rep = LANES // HEAD_DIM
    return jnp.tile(cos_h, (1, rep)), jnp.tile(sa_h, (1, rep)), jnp.tile(sb_h, (1, rep))


def kernel(x, meta_tokens, conv_w_in, conv_b_in, conv_w_dw, conv_b_dw, conv_ln_g, conv_ln_b, conv_w_out,
           conv_b_out, w_k, b_k, w_v, b_v, w_q, b_q, w_o, b_o, sinks, ln_mix_g, ln_mix_b, ln_ffn_g, ln_ffn_b,
           router_group_w, router_group_b, router_expert_w, router_expert_b, expert_w1, expert_w3, expert_w2):
    assert x.shape == (1, SEQ, D)
    row = lambda v: v.reshape(1, -1)
    x2d = x.reshape(SEQ, D)
    meta_pad = jnp.pad(meta_tokens.astype(f32), ((0, TM - N_META), (0, 0)))

    y = _glu(x2d, meta_pad, conv_w_in[0].astype(bf16), row(conv_b_in[0]))
    z = _conv(y, conv_w_dw[0].reshape(CONV_W, CHUNKS, LANES), row(conv_b_dw[0]), row(conv_ln_g[0]),
              row(conv_ln_b[0]))
    wr, br = _router_weights(router_group_w[0], router_group_b[0], router_expert_w[0], router_expert_b[0])
    h, eidx, rank, gcol, cnt = _proj_ln_route(z, (x2d, meta_pad), conv_w_out[0].astype(bf16), row(conv_b_out[0]),
                                              row(ln_mix_g[0]), row(ln_mix_b[0]), wr, br)
    h = _moe(h, eidx, rank, gcol, cnt, expert_w1[0], expert_w3[0], expert_w2[0],
             row(ln_ffn_g[0]), row(ln_ffn_b[0]), NT)

    wqkv = jnp.concatenate([w_q[0], w_k, w_v], axis=1).astype(bf16)
    bqkv = row(jnp.concatenate([b_q[0], b_k, b_v]))
    q, k, v = _qkv(h, wqkv, bqkv, *_rope_tables())
    sink_rows = jnp.repeat(sinks[0].astype(f32).reshape(N_KV, GQA), QB, axis=1).reshape(N_KV, GQA * QB, 1)
    att = _attention(q, k, v, sink_rows)
    wr, br = _router_weights(router_group_w[1], router_group_b[1], router_expert_w[1], router_expert_b[1])
    h, eidx, rank, gcol, cnt = _proj_ln_route(att, h, w_o[0].astype(bf16), row(b_o[0]),
                                              row(ln_mix_g[1]), row(ln_mix_b[1]), wr, br)
    out = _moe(h, eidx, rank, gcol, cnt, expert_w1[1], expert_w3[1], expert_w2[1],
               row(ln_ffn_g[1]), row(ln_ffn_b[1]), NXT)
    return out.reshape(1, SEQ, D)
```

```python
import functools
import math

import jax
import jax.numpy as jnp
from jax import lax
from jax.experimental import pallas as pl
from jax.experimental.pallas import tpu as pltpu

f32 = jnp.float32
bf16 = jnp.bfloat16
i32 = jnp.int32
u32 = jnp.uint32

D = 2048
SEQ = 8192
DEPTH = 2
N_META = 16
CONV_W = 31
HEAD_DIM = 64
N_HEADS = 32
N_KV = 4
GQA = 8
KVW = N_KV * HEAD_DIM
WINDOW = 128
ROT = 16
ROPE_THETA = 500000.0
N_GROUPS = 4
EPG = 8
N_EXP = 32
FF = 256
ALPHA = (2.0 * DEPTH) ** 0.25
LN_EPS = 1e-5

LANES = 128
TM = 256
NXT = SEQ // TM
NT = NXT + 1
TP = NT * TM
META_ROW = SEQ
CHUNKS = D // LANES
HALO = 32
TME = 256
NTE = (2 * TP) // TME + N_EXP
NS = NTE * TME
QB = 128
VMEM_LIMIT = 52 * 1024 * 1024


def _cparams():
    return pltpu.CompilerParams(dimension_semantics=("arbitrary",), vmem_limit_bytes=VMEM_LIMIT)


def _resident(shape):
    nd = len(shape)
    return pl.BlockSpec(shape, lambda *a: (0,) * nd, pipeline_mode=pl.Buffered(1))


def _layer_norm(x, g, b):
    mu = jnp.mean(x, axis=-1, keepdims=True)
    xc = x - mu
    var = jnp.mean(xc * xc, axis=-1, keepdims=True)
    return xc * lax.rsqrt(var + LN_EPS) * g + b


def _x_or_meta(i, x_ref, meta_ref):
    return jnp.where(i == NXT, meta_ref[...], x_ref[...])


def _glu_kernel(x_ref, meta_ref, w_ref, b_ref, y_ref):
    i = pl.program_id(0)
    xb = _x_or_meta(i, x_ref, meta_ref).astype(bf16)
    cw = 512
    for c in range(D // cw):
        lo, hi = c * cw, (c + 1) * cw
        a = jnp.dot(xb, w_ref[:, lo:hi], preferred_element_type=f32) + b_ref[:, lo:hi]
        g = jnp.dot(xb, w_ref[:, D + lo:D + hi], preferred_element_type=f32) + b_ref[:, D + lo:D + hi]
        y_ref[:, lo:hi] = a * jax.nn.sigmoid(g)


def _glu(x2d, meta_pad, w_in, b_in):
    return pl.pallas_call(
        _glu_kernel,
        grid=(NT,),
        in_specs=[
            pl.BlockSpec((TM, D), lambda i: (jnp.minimum(i, NXT - 1), 0)),
            _resident((TM, D)),
            _resident((D, 2 * D)),
            _resident((1, 2 * D)),
        ],
        out_specs=pl.BlockSpec((TM, D), lambda i: (i, 0)),
        out_shape=jax.ShapeDtypeStruct((TP, D), f32),
        compiler_params=_cparams(),
        name="glu",
    )(x2d, meta_pad, w_in, b_in)


def _conv_kernel(y_ref, ymeta_ref, w_ref, bdw_ref, g_ref, b_ref, z_ref, scr, accs, u):
    i = pl.program_id(0)
    hrows = HALO * CHUNKS
    trows = TM * CHUNKS

    @pl.when(i == 0)
    def _():
        scr[0:(HALO - N_META) * CHUNKS, :] = jnp.zeros(((HALO - N_META) * CHUNKS, LANES), f32)
        for c in range(CHUNKS):
            scr[pl.ds((HALO - N_META) * CHUNKS + c, N_META, stride=CHUNKS), :] = \
                ymeta_ref[:, c * LANES:(c + 1) * LANES]

    @pl.when(i == NT - 1)
    def _():
        scr[0:hrows, :] = jnp.zeros((hrows, LANES), f32)

    @pl.when(jnp.logical_and(i > 0, i < NT - 1))
    def _():
        scr[0:hrows, :] = scr[trows:trows + hrows, :]

    for c in range(CHUNKS):
        scr[pl.ds(hrows + c, TM, stride=CHUNKS), :] = y_ref[:, c * LANES:(c + 1) * LANES]

    tb = 8
    brows = tb * CHUNKS
    first = (HALO - (CONV_W - 1)) * CHUNKS

    def block(t, carry):
        base = pl.multiple_of(t * brows, brows)
        acc = jnp.zeros((tb, CHUNKS, LANES), f32)
        for j in range(CONV_W):
            sl = scr[pl.ds(base + first + j * CHUNKS, brows), :].reshape(tb, CHUNKS, LANES)
            acc = acc + sl * w_ref[j][None]
        accs[pl.ds(base, brows), :] = acc.reshape(brows, LANES)
        return carry

    lax.fori_loop(0, TM // tb, block, 0)

    for c in range(CHUNKS):
        u[:, c * LANES:(c + 1) * LANES] = accs[pl.ds(c, TM, stride=CHUNKS), :]
    v = _layer_norm(u[...] + bdw_ref[...], g_ref[...], b_ref[...])
    z_ref[...] = (v * jax.nn.sigmoid(v)).astype(bf16)


def _conv(y, w_dw3, b_dw, ln_g, ln_b):
    return pl.pallas_call(
        _conv_kernel,
        grid=(NT,),
        in_specs=[
            pl.BlockSpec((TM, D), lambda i: (i, 0)),
            pl.BlockSpec((N_META, D), lambda i: (META_ROW // N_META, 0)),
            _resident((CONV_W, CHUNKS, LANES)),
            _resident((1, D)),
            _resident((1, D)),
            _resident((1, D)),
        ],
        out_specs=pl.BlockSpec((TM, D), lambda i: (i, 0)),
        out_shape=jax.ShapeDtypeStruct((TP, D), bf16),
        scratch_shapes=[
            pltpu.VMEM(((TM + HALO) * CHUNKS, LANES), f32),
            pltpu.VMEM((TM * CHUNKS, LANES), f32),
            pltpu.VMEM((TM, D), f32),
        ],
        compiler_params=_cparams(),
        name="conv_ln_swish",
    )(y, y, w_dw3, b_dw, ln_g, ln_b)


R_EXP0 = 8


def _route(h, wr_ref, br_ref, running):
    logits = jnp.dot(h.astype(bf16), wr_ref[...], preferred_element_type=f32) + br_ref[...]
    lt = logits.T
    gl = [lt[k:k + 1, :] for k in range(N_GROUPS)]
    gm = functools.reduce(jnp.maximum, gl)
    gex = [jnp.exp(v - gm) for v in gl]
    gden = functools.reduce(lambda a, b: a + b, gex)
    gp = [v / gden for v in gex]
    best = gp[0]
    gi = jnp.zeros((1, TM), i32)
    for k in range(1, N_GROUPS):
        better = gp[k] > best
        gi = jnp.where(better, k, gi)
        best = jnp.where(better, gp[k], best)
    esel = lt[R_EXP0:R_EXP0 + EPG, :]
    for k in range(1, N_GROUPS):
        esel = jnp.where(gi == k, lt[R_EXP0 + EPG * k:R_EXP0 + EPG * (k + 1), :], esel)
    em = jnp.max(esel, axis=0, keepdims=True)
    eex = jnp.exp(esel - em)
    ep = eex / jnp.sum(eex, axis=0, keepdims=True)
    io8 = lax.broadcasted_iota(i32, (EPG, TM), 0)
    v1 = jnp.max(ep, axis=0, keepdims=True)
    i1 = jnp.min(jnp.where(ep == v1, io8, EPG), axis=0, keepdims=True)
    ep2 = jnp.where(io8 == i1, -1.0, ep)
    v2 = jnp.max(ep2, axis=0, keepdims=True)
    i2 = jnp.min(jnp.where(ep2 == v2, io8, EPG), axis=0, keepdims=True)
    s = v1 + v2
    gate0 = best * (v1 / s)
    gate1 = best * (v2 / s)
    f0 = gi * EPG + i1
    f1 = gi * EPG + i2

    io32 = lax.broadcasted_iota(i32, (N_EXP, TM), 0)
    oh0 = (io32 == f0).astype(f32)
    oh1 = (io32 == f1).astype(f32)
    cnt = oh0 + oh1
    upper = (lax.broadcasted_iota(i32, (TM, TM), 0) < lax.broadcasted_iota(i32, (TM, TM), 1))
    before = jnp.dot(cnt.astype(bf16), upper.astype(f32).astype(bf16), preferred_element_type=f32)
    base = running[...] + before
    r0 = jnp.sum(oh0 * base, axis=0, keepdims=True).astype(i32)
    r1 = jnp.sum(oh1 * base, axis=0, keepdims=True).astype(i32)
    running[...] = running[...] + jnp.sum(cnt, axis=1, keepdims=True)

    io128 = lax.broadcasted_iota(i32, (LANES, TM), 0)
    gcol = jnp.where(io128 == 0, gate0, jnp.where(io128 == 1, gate1, 0.0)).T
    return f0, f1, r0, r1, gcol


HALF = D // 2
PK = HALF // LANES


def _pack_rows(v, out2d):
    rows = v.shape[0]
    bits = pltpu.bitcast(v.astype(bf16).astype(f32), u32)
    word = bits[:, HALF:] | lax.shift_right_logical(bits[:, :HALF], jnp.uint32(16))
    for s in range(PK):
        out2d[pl.ds(s, rows, stride=PK), :] = word[:, s * LANES:(s + 1) * LANES]


def _unpack_rows(in2d, rows, dtype):
    lo, hi = [], []
    for s in range(PK):
        w = in2d[pl.ds(s, rows, stride=PK), :]
        lo.append(pltpu.bitcast(lax.shift_left(w, jnp.uint32(16)), f32).astype(dtype))
        hi.append(pltpu.bitcast(w & jnp.uint32(0xFFFF0000), f32).astype(dtype))
    return jnp.concatenate(lo, axis=1), jnp.concatenate(hi, axis=1)


def _proj_ln_route_body(i, a, res, bias_ref, g_ref, b_ref, wr_ref, br_ref,
                        h_ref, hp_ref, eidx_ref, rank_ref, gcol_ref, cnt_ref, running):
    @pl.when(i == 0)
    def _():
        running[...] = jnp.zeros_like(running)

    mix = a + bias_ref[...]
    h = _layer_norm(ALPHA * res + mix, g_ref[...], b_ref[...])
    h_ref[...] = h
    _pack_rows(h, hp_ref)
    f0, f1, r0, r1, gcol = _route(h, wr_ref, br_ref, running)
    eidx_ref[0, 0:1, :] = f0
    eidx_ref[0, 1:2, :] = f1
    rank_ref[0, 0:1, :] = r0
    rank_ref[0, 1:2, :] = r1
    gcol_ref[...] = gcol
    cnt_ref[...] = running[...]


def _proj_ln_route_kernel_l0(a_ref, x_ref, meta_ref, w_ref, *rest):
    i = pl.program_id(0)
    a = jnp.dot(a_ref[...], w_ref[...], preferred_element_type=f32)
    _proj_ln_route_body(i, a, _x_or_meta(i, x_ref, meta_ref), *rest)


def _proj_ln_route_kernel_l1(aT_ref, res_ref, w_ref, *rest):
    a = lax.dot_general(aT_ref[...], w_ref[...], (((0,), (0,)), ((), ())), preferred_element_type=f32)
    _proj_ln_route_body(pl.program_id(0), a, res_ref[...], *rest)


def _proj_ln_route(a, res, w, bias, ln_g, ln_b, wr, br):
    first = isinstance(res, tuple)
    if first:
        a_spec = pl.BlockSpec((TM, D), lambda i: (i, 0))
        res_specs = [pl.BlockSpec((TM, D), lambda i: (jnp.minimum(i, NXT - 1), 0)), _resident((TM, D))]
        res_args = list(res)
        body = _proj_ln_route_kernel_l0
    else:
        a_spec = pl.BlockSpec((D, TM), lambda i: (0, i))
        res_specs = [pl.BlockSpec((TM, D), lambda i: (i, 0))]
        res_args = [res]
        body = _proj_ln_route_kernel_l1
    tile3 = pl.BlockSpec((1, 2, TM), lambda i: (i, 0, 0))
    return pl.pallas_call(
        body,
        grid=(NT,),
        in_specs=[a_spec] + res_specs + [
            _resident((D, D)), _resident((1, D)), _resident((1, D)), _resident((1, D)),
            _resident((D, LANES)), _resident((1, LANES)),
        ],
        out_specs=[
            pl.BlockSpec((TM, D), lambda i: (i, 0)),
            pl.BlockSpec((TM * PK, LANES), lambda i: (i, 0)),
            tile3, tile3,
            pl.BlockSpec((TM, LANES), lambda i: (i, 0)),
            pl.BlockSpec((N_EXP, TM), lambda i: (0, 0)),
        ],
        out_shape=[
            jax.ShapeDtypeStruct((TP, D), f32),
            jax.ShapeDtypeStruct((TP * PK, LANES), u32),
            jax.ShapeDtypeStruct((NT, 2, TM), i32),
            jax.ShapeDtypeStruct((NT, 2, TM), i32),
            jax.ShapeDtypeStruct((TP, LANES), f32),
            jax.ShapeDtypeStruct((N_EXP, TM), f32),
        ],
        scratch_shapes=[pltpu.VMEM((N_EXP, TM), f32)],
        compiler_params=_cparams(),
        name="proj_ln_route",
    )(a, *res_args, w, bias, ln_g, ln_b, wr, br)


def _plan(eidx, rank, cnt):
    counts = cnt[:, 0].astype(i32)
    padded = ((counts + TME - 1) // TME) * TME
    ends = jnp.cumsum(padded)
    offs = ends - padded
    ntiles = ends[-1] // TME
    off_of = jnp.sum(jnp.where(eidx[..., None] == jnp.arange(N_EXP, dtype=i32), offs, 0), axis=-1)
    dest = (off_of + rank).reshape(-1)
    tile_start = jnp.minimum(jnp.arange(NTE, dtype=i32), ntiles - 1) * TME
    tile_expert = jnp.minimum(jnp.sum(tile_start[:, None] >= ends[None, :], axis=1), N_EXP - 1).astype(i32)
    zstart = jnp.where(padded > 0, ends - TME, 0).astype(i32)
    zflag = (padded > 0).astype(i32)
    return dest.astype(i32), tile_expert, ntiles.reshape(1).astype(i32), zstart, zflag


ISSUE_UNROLL = 8


def _scatter_kernel(dest_ref, zstart_ref, zflag_ref, hp_ref, xs_hbm, zeros, sem, zsem):
    i = pl.program_id(0)

    @pl.when(i == 0)
    def _():
        zeros[...] = jnp.zeros_like(zeros)
        for e in range(N_EXP):
            @pl.when(zflag_ref[e] > 0)
            def _():
                start = pl.multiple_of(zstart_ref[e], TME)
                cp = pltpu.make_async_copy(zeros, xs_hbm.at[pl.ds(start, TME)], zsem)
                cp.start()
                cp.wait()

    base = i * (2 * TM)

    def row(r, carry):
        for k in range(2):
            d = dest_ref[base + k * TM + r]
            pltpu.make_async_copy(hp_ref.at[pl.ds(r, 1)], xs_hbm.at[pl.ds(d, 1)], sem).start()
        return carry

    lax.fori_loop(0, TM, row, 0, unroll=ISSUE_UNROLL)

    def drain(r, carry):
        pltpu.make_async_copy(hp_ref.at[pl.ds(0, 1)], xs_hbm.at[pl.ds(0, 1)], sem).wait()
        return carry

    lax.fori_loop(0, 2 * TM, drain, 0, unroll=ISSUE_UNROLL)


def _scatter(dest, zstart, zflag, hp3):
    return pl.pallas_call(
        _scatter_kernel,
        grid_spec=pltpu.PrefetchScalarGridSpec(
            num_scalar_prefetch=3,
            grid=(NT,),
            in_specs=[pl.BlockSpec((TM, PK, LANES), lambda i, *_: (i, 0, 0))],
            out_specs=pl.BlockSpec(memory_space=pl.ANY),
            scratch_shapes=[pltpu.VMEM((TME, PK, LANES), u32), pltpu.SemaphoreType.DMA(()),
                            pltpu.SemaphoreType.DMA(())],
        ),
        out_shape=jax.ShapeDtypeStruct((NS, PK, LANES), u32),
        compiler_params=pltpu.CompilerParams(dimension_semantics=("arbitrary",), vmem_limit_bytes=VMEM_LIMIT,
                                             has_side_effects=True),
        name="moe_scatter",
    )(dest, zstart, zflag, hp3)


def _expert_kernel(te_ref, nt_ref, xs_ref, w1_ref, w3_ref, w2_ref, ys_ref, w1c, w3c, w2c):
    i = pl.program_id(0)
    prev = te_ref[jnp.maximum(i - 1, 0)]

    @pl.when(jnp.logical_and(i < nt_ref[0], jnp.logical_or(i == 0, te_ref[i] != prev)))
    def _():
        w1c[...] = w1_ref[0, 0].astype(bf16)
        w3c[...] = w3_ref[0, 0].astype(bf16)
        w2c[...] = w2_ref[0, 0].astype(bf16)

    @pl.when(i < nt_ref[0])
    def _():
        xlo, xhi = _unpack_rows(xs_ref, TME, bf16)
        a = (jnp.dot(xlo, w1c[0:HALF, :], preferred_element_type=f32)
             + jnp.dot(xhi, w1c[HALF:D, :], preferred_element_type=f32))
        b = (jnp.dot(xlo, w3c[0:HALF, :], preferred_element_type=f32)
             + jnp.dot(xhi, w3c[HALF:D, :], preferred_element_type=f32))
        hid = (a * jax.nn.sigmoid(a) * b).astype(bf16)
        _pack_rows(jnp.dot(hid, w2c[...], preferred_element_type=f32), ys_ref)


def _experts(layer, tile_expert, ntiles, xs2d, w1, w3, w2):
    def row_map(i, te, nt):
        return (jnp.minimum(i, nt[0] - 1), 0)

    def w_map(i, te, nt):
        return (layer, te[i], 0, 0)

    return pl.pallas_call(
        _expert_kernel,
        grid_spec=pltpu.PrefetchScalarGridSpec(
            num_scalar_prefetch=2,
            grid=(NTE,),
            in_specs=[
                pl.BlockSpec((TME * PK, LANES), row_map),
                pl.BlockSpec((1, 1, D, FF), w_map),
                pl.BlockSpec((1, 1, D, FF), w_map),
                pl.BlockSpec((1, 1, FF, D), w_map),
            ],
            out_specs=pl.BlockSpec((TME * PK, LANES), row_map),
            scratch_shapes=[pltpu.VMEM((D, FF), bf16), pltpu.VMEM((D, FF), bf16), pltpu.VMEM((FF, D), bf16)],
        ),
        out_shape=jax.ShapeDtypeStruct((NS * PK, LANES), u32),
        compiler_params=_cparams(),
        name="moe_experts",
    )(tile_expert, ntiles, xs2d, w1, w3, w2)


def _combine_kernel(n, dest_ref, ys_hbm, h_ref, gcol_ref, g_ref, b_ref, o_ref, buf, sem):
    i = pl.program_id(0)
    slot = lax.rem(i, 2)

    def issue(tile, s):
        base = tile * (2 * TM)

        def row(r, carry):
            for k in range(2):
                d = pl.multiple_of(dest_ref[base + k * TM + r] * PK, PK)
                pltpu.make_async_copy(ys_hbm.at[pl.ds(d, PK)],
                                      buf.at[s, k, pl.ds(pl.multiple_of(r * PK, PK), PK)], sem.at[s]).start()
            return carry

        lax.fori_loop(0, TM, row, 0, unroll=ISSUE_UNROLL)

    @pl.when(i == 0)
    def _():
        issue(0, 0)

    @pl.when(i + 1 < n)
    def _():
        issue(i + 1, 1 - slot)

    def drain(r, carry):
        pltpu.make_async_copy(ys_hbm.at[pl.ds(0, PK)], buf.at[slot, 0, pl.ds(0, PK)], sem.at[slot]).wait()
        return carry

    lax.fori_loop(0, 2 * TM, drain, 0, unroll=ISSUE_UNROLL)

    lo0, hi0 = _unpack_rows(buf.at[slot, 0], TM, f32)
    lo1, hi1 = _unpack_rows(buf.at[slot, 1], TM, f32)
    g0, g1 = gcol_ref[:, 0:1], gcol_ref[:, 1:2]
    ffn = jnp.concatenate([lo0 * g0 + lo1 * g1, hi0 * g0 + hi1 * g1], axis=1)
    o_ref[...] = _layer_norm(ALPHA * h_ref[...] + ffn, g_ref[...], b_ref[...])


def _combine(dest, ys2d, h, gcol, ln_g, ln_b, ntiles_out):
    return pl.pallas_call(
        functools.partial(_combine_kernel, ntiles_out),
        grid_spec=pltpu.PrefetchScalarGridSpec(
            num_scalar_prefetch=1,
            grid=(ntiles_out,),
            in_specs=[
                pl.BlockSpec(memory_space=pl.ANY),
                pl.BlockSpec((TM, D), lambda i, *_: (i, 0)),
                pl.BlockSpec((TM, LANES), lambda i, *_: (i, 0)),
                pl.BlockSpec((1, D), lambda i, *_: (0, 0)),
                pl.BlockSpec((1, D), lambda i, *_: (0, 0)),
            ],
            out_specs=pl.BlockSpec((TM, D), lambda i, *_: (i, 0)),
            scratch_shapes=[pltpu.VMEM((2, 2, TM * PK, LANES), u32), pltpu.SemaphoreType.DMA((2,))],
        ),
        out_shape=jax.ShapeDtypeStruct((ntiles_out * TM, D), f32),
        compiler_params=_cparams(),
        name="moe_combine_ln",
    )(dest, ys2d, h, gcol, ln_g, ln_b)


def _moe(layer, h, hp2d, eidx, rank, gcol, cnt, w1, w3, w2, ln_g, ln_b, ntiles_out):
    dest, tile_expert, ntiles, zstart, zflag = _plan(eidx, rank, cnt)
    xs = _scatter(dest, zstart, zflag, hp2d.reshape(TP, PK, LANES))
    ys2d = _experts(layer, tile_expert, ntiles, xs.reshape(NS * PK, LANES), w1, w3, w2)
    return _combine(dest, ys2d, h, gcol, ln_g, ln_b, ntiles_out)


NT_DIMS = (((1,), (1,)), ((), ()))


def _rope_rows(t, cos, sa, sb):
    w = t.shape[1]
    reps = w // LANES
    c = jnp.tile(cos, (1, reps))
    a = jnp.tile(sa, (1, reps))
    b = jnp.tile(sb, (1, reps))
    return t * c + pltpu.roll(t, w - ROT // 2, 1) * a + pltpu.roll(t, ROT // 2, 1) * b


def _qkv_kernel(h_ref, wqT_ref, bq_ref, wk_ref, bk_ref, wvT_ref, bv_ref, cosT_ref, sinT_ref,
                cos_ref, sa_ref, sb_ref, qT_ref, k_ref, vT_ref):
    hb = h_ref[...].astype(bf16)
    scale = 1.0 / math.sqrt(HEAD_DIM)
    half = ROT // 2
    cosT = cosT_ref[...][None]
    sinT = sinT_ref[...][None]
    rows = GQA * HEAD_DIM
    for c in range(D // rows):
        lo, hi = c * rows, (c + 1) * rows
        t = lax.dot_general(wqT_ref[lo:hi, :], hb, NT_DIMS, preferred_element_type=f32) + bq_ref[lo:hi, :]
        t3 = t.reshape(GQA, HEAD_DIM, TM)
        x1, x2 = t3[:, 0:half, :], t3[:, half:ROT, :]
        r = jnp.concatenate([x1 * cosT - x2 * sinT, x2 * cosT + x1 * sinT, t3[:, ROT:, :]], axis=1)
        qT_ref[lo:hi, :] = (r * scale).reshape(rows, TM).astype(bf16)
    t = jnp.dot(hb, wk_ref[...], preferred_element_type=f32) + bk_ref[...]
    k_ref[...] = _rope_rows(t, cos_ref[...], sa_ref[...], sb_ref[...]).astype(bf16)
    t = lax.dot_general(wvT_ref[...], hb, NT_DIMS, preferred_element_type=f32) + bv_ref[...]
    vT_ref[...] = t.astype(bf16)


def _qkv(h, wqT, bq_col, wk, bk, wvT, bv_col, tables):
    cosT, sinT, cos_t, sa_t, sb_t = tables
    tabT = pl.BlockSpec((ROT // 2, TM), lambda i: (0, i))
    tab = pl.BlockSpec((TM, LANES), lambda i: (i, 0))
    return pl.pallas_call(
        _qkv_kernel,
        grid=(NT,),
        in_specs=[
            pl.BlockSpec((TM, D), lambda i: (i, 0)),
            _resident((D, D)), _resident((D, 1)),
            _resident((D, KVW)), _resident((1, KVW)),
            _resident((KVW, D)), _resident((KVW, 1)),
            tabT, tabT, tab, tab, tab,
        ],
        out_specs=[
            pl.BlockSpec((D, TM), lambda i: (0, i)),
            pl.BlockSpec((TM, KVW), lambda i: (i, 0)),
            pl.BlockSpec((KVW, TM), lambda i: (0, i)),
        ],
        out_shape=[
            jax.ShapeDtypeStruct((D, TP), bf16),
            jax.ShapeDtypeStruct((TP, KVW), bf16),
            jax.ShapeDtypeStruct((KVW, TP), bf16),
        ],
        compiler_params=_cparams(),
        name="qkv_rope",
    )(h, wqT, bq_col, wk, bk, wvT, bv_col, cosT, sinT, cos_t, sa_t, sb_t)


NKEY = 2 * QB + N_META
HPAIR = 2 * QB


def _attn_kernel(qT_ref, kc_ref, kp_ref, km_ref, vTc_ref, vTp_ref, vTm_ref, sink_ref, oT_ref):
    i = pl.program_id(0)
    is_meta = i == NT - 1
    ck = lax.broadcasted_iota(i32, (NKEY, QB), 0)
    rq = lax.broadcasted_iota(i32, (NKEY, QB), 1)
    in_band = jnp.logical_and(ck > rq, ck <= rq + QB)
    meta_ok = jnp.logical_and(ck >= 2 * QB, jnp.logical_or(jnp.logical_not(is_meta), ck - 2 * QB <= rq))

    for blk in range(TM // QB):
        lo = jnp.where(is_meta, 2 * QB, jnp.where(jnp.logical_and(i == 0, blk == 0), QB, 0))
        valid = jnp.logical_or(meta_ok, jnp.logical_and(in_band, ck >= lo))
        bias = jnp.where(valid, 0.0, -jnp.inf)
        bias = jnp.concatenate([bias, bias], axis=1)
        c0 = blk * QB
        for g in range(N_KV):
            gs = slice(g * HEAD_DIM, (g + 1) * HEAD_DIM)
            if blk == 0:
                kprev, vprevT = kp_ref[:, gs], vTp_ref[gs, :]
            else:
                kprev, vprevT = kc_ref[c0 - QB:c0, gs], vTc_ref[gs, c0 - QB:c0]
            kcat = jnp.concatenate([kprev, kc_ref[c0:c0 + QB, gs], km_ref[:, gs]], axis=0)
            vcatT = jnp.concatenate([vprevT, vTc_ref[gs, c0:c0 + QB], vTm_ref[gs, 0:N_META]], axis=1)
            for hp in range(GQA // 2):
                h0 = g * GQA + 2 * hp
                r0, r1, r2 = h0 * HEAD_DIM, (h0 + 1) * HEAD_DIM, (h0 + 2) * HEAD_DIM
                qpair = jnp.concatenate([qT_ref[r0:r1, c0:c0 + QB], qT_ref[r1:r2, c0:c0 + QB]], axis=1)
                s = jnp.dot(kcat, qpair, preferred_element_type=f32) + bias
                sink = sink_ref[h0 // 2:h0 // 2 + 1, :]
                m = jnp.maximum(jnp.max(s, axis=0, keepdims=True), sink)
                p = jnp.exp(s - m)
                den = jnp.sum(p, axis=0, keepdims=True) + jnp.exp(sink - m)
                o = jnp.dot(vcatT, p.astype(bf16), preferred_element_type=f32) * (1.0 / den)
                oT_ref[r0:r1, c0:c0 + QB] = o[:, 0:QB].astype(bf16)
                oT_ref[r1:r2, c0:c0 + QB] = o[:, QB:HPAIR].astype(bf16)


def _attention(qT, k, vT, sink_pairs):
    prev_blk = lambda i: jnp.maximum(i * (TM // QB) - 1, 0)
    return pl.pallas_call(
        _attn_kernel,
        grid=(NT,),
        in_specs=[
            pl.BlockSpec((D, TM), lambda i: (0, i)),
            pl.BlockSpec((TM, KVW), lambda i: (i, 0)),
            pl.BlockSpec((QB, KVW), lambda i: (prev_blk(i), 0)),
            pl.BlockSpec((N_META, KVW), lambda i: (META_ROW // N_META, 0)),
            pl.BlockSpec((KVW, TM), lambda i: (0, i)),
            pl.BlockSpec((KVW, QB), lambda i: (0, prev_blk(i))),
            pl.BlockSpec((KVW, LANES), lambda i: (0, META_ROW // LANES)),
            _resident((N_HEADS // 2, HPAIR)),
        ],
        out_specs=pl.BlockSpec((D, TM), lambda i: (0, i)),
        out_shape=jax.ShapeDtypeStruct((D, TP), bf16),
        compiler_params=_cparams(),
        name="swa_attention",
    )(qT, k, k, k, vT, vT, vT, sink_pairs)


def _router_weights(wg, bg, we, be):
    wr = jnp.zeros((D, LANES), f32).at[:, 0:N_GROUPS].set(wg).at[:, R_EXP0:R_EXP0 + N_EXP].set(we)
    br = jnp.zeros((1, LANES), f32).at[0, 0:N_GROUPS].set(bg).at[0, R_EXP0:R_EXP0 + N_EXP].set(be)
    return wr.astype(bf16), br


def _rope_tables():
    pos = jnp.concatenate([jnp.arange(SEQ) + N_META, jnp.arange(TM)]).astype(f32)
    half = ROT // 2
    inv_freq = ROPE_THETA ** (-jnp.arange(0, ROT, 2, dtype=f32) / ROT)
    ang = pos[:, None] * inv_freq[None, :]
    cos, sin = jnp.cos(ang), jnp.sin(ang)
    ones = jnp.ones((TP, HEAD_DIM - ROT), f32)
    zeros = jnp.zeros((TP, HEAD_DIM - ROT), f32)
    z8 = jnp.zeros((TP, half), f32)
    cos_h = jnp.concatenate([cos, cos, ones], axis=1)
    sa_h = jnp.concatenate([-sin, z8, zeros], axis=1)
    sb_h = jnp.concatenate([z8, sin, zeros], axis=1)
    rep = LANES // HEAD_DIM
    return cos.T, sin.T, jnp.tile(cos_h, (1, rep)), jnp.tile(sa_h, (1, rep)), jnp.tile(sb_h, (1, rep))


def kernel(x, meta_tokens, conv_w_in, conv_b_in, conv_w_dw, conv_b_dw, conv_ln_g, conv_ln_b, conv_w_out,
           conv_b_out, w_k, b_k, w_v, b_v, w_q, b_q, w_o, b_o, sinks, ln_mix_g, ln_mix_b, ln_ffn_g, ln_ffn_b,
           router_group_w, router_group_b, router_expert_w, router_expert_b, expert_w1, expert_w3, expert_w2):
    assert x.shape == (1, SEQ, D)
    row = lambda v: v.reshape(1, -1)
    col = lambda v: v.reshape(-1, 1)
    x2d = x.reshape(SEQ, D)
    meta_pad = jnp.pad(meta_tokens.astype(f32), ((0, TM - N_META), (0, 0)))

    y = _glu(x2d, meta_pad, conv_w_in[0].astype(bf16), row(conv_b_in[0]))
    z = _conv(y, conv_w_dw[0].reshape(CONV_W, CHUNKS, LANES), row(conv_b_dw[0]), row(conv_ln_g[0]),
              row(conv_ln_b[0]))
    wr, br = _router_weights(router_group_w[0], router_group_b[0], router_expert_w[0], router_expert_b[0])
    h, hp, eidx, rank, gcol, cnt = _proj_ln_route(
        z, (x2d, meta_pad), conv_w_out[0].astype(bf16), row(conv_b_out[0]), row(ln_mix_g[0]), row(ln_mix_b[0]),
        wr, br)
    h = _moe(0, h, hp, eidx, rank, gcol, cnt, expert_w1, expert_w3, expert_w2,
             row(ln_ffn_g[0]), row(ln_ffn_b[0]), NT)

    qT, k, vT = _qkv(h, w_q[0].T.astype(bf16), col(b_q[0]), w_k.astype(bf16), row(b_k),
                     w_v.T.astype(bf16), col(b_v), _rope_tables())
    sink_pairs = jnp.repeat(sinks[0].astype(f32).reshape(N_HEADS // 2, 2), QB, axis=1)
    attT = _attention(qT, k, vT, sink_pairs)
    wr, br = _router_weights(router_group_w[1], router_group_b[1], router_expert_w[1], router_expert_b[1])
    h, hp, eidx, rank, gcol, cnt = _proj_ln_route(
        attT, h, w_o[0].astype(bf16), row(b_o[0]), row(ln_mix_g[1]), row(ln_mix_b[1]), wr, br)
    out = _moe(1, h, hp, eidx, rank, gcol, cnt, expert_w1, expert_w3, expert_w2,
               row(ln_ffn_g[1]), row(ln_ffn_b[1]), NXT)
    return out.reshape(1, SEQ, D)
```

```python
import functools
import math

import jax
import jax.numpy as jnp
from jax import lax
from jax.experimental import pallas as pl
from jax.experimental.pallas import tpu as pltpu

f32 = jnp.float32
bf16 = jnp.bfloat16
i32 = jnp.int32
u32 = jnp.uint32

D = 2048
SEQ = 8192
DEPTH = 2
N_META = 16
CONV_W = 31
HEAD_DIM = 64
N_HEADS = 32
N_KV = 4
GQA = 8
KVW = N_KV * HEAD_DIM
WINDOW = 128
ROT = 16
ROPE_THETA = 500000.0
N_GROUPS = 4
EPG = 8
N_EXP = 32
FF = 256
ALPHA = (2.0 * DEPTH) ** 0.25
LN_EPS = 1e-5

LANES = 128
TM = 256
NXT = SEQ // TM
NT = NXT + 1
TP = NT * TM
META_ROW = SEQ
CHUNKS = D // LANES
HALO = 32
TME = 256
NTE = (2 * TP) // TME + N_EXP
NS = NTE * TME
QB = 128
VMEM_LIMIT = 52 * 1024 * 1024


def _cparams():
    return pltpu.CompilerParams(dimension_semantics=("arbitrary",), vmem_limit_bytes=VMEM_LIMIT)


def _resident(shape):
    nd = len(shape)
    return pl.BlockSpec(shape, lambda *a: (0,) * nd, pipeline_mode=pl.Buffered(1))


def _layer_norm(x, g, b):
    mu = jnp.mean(x, axis=-1, keepdims=True)
    xc = x - mu
    var = jnp.mean(xc * xc, axis=-1, keepdims=True)
    return xc * lax.rsqrt(var + LN_EPS) * g + b


def _x_or_meta(i, x_ref, meta_ref):
    return jnp.where(i == NXT, meta_ref[...], x_ref[...])


HALF = D // 2
PK = HALF // LANES


def _pack_rows(v, out2d):
    rows = v.shape[0]
    bits = pltpu.bitcast(v.astype(bf16).astype(f32), u32)
    word = bits[:, HALF:] | lax.shift_right_logical(bits[:, :HALF], jnp.uint32(16))
    for s in range(PK):
        out2d[pl.ds(s, rows, stride=PK), :] = word[:, s * LANES:(s + 1) * LANES]


def _unpack_rows(in2d, rows, dtype):
    lo, hi = [], []
    for s in range(PK):
        w = in2d[pl.ds(s, rows, stride=PK), :]
        lo.append(pltpu.bitcast(lax.shift_left(w, jnp.uint32(16)), f32).astype(dtype))
        hi.append(pltpu.bitcast(w & jnp.uint32(0xFFFF0000), f32).astype(dtype))
    return jnp.concatenate(lo, axis=1), jnp.concatenate(hi, axis=1)


def _glu_kernel(x_ref, meta_ref, w_ref, b_ref, yp_ref, ybuf):
    i = pl.program_id(0)
    xb = _x_or_meta(i, x_ref, meta_ref).astype(bf16)
    cw = 512
    for c in range(D // cw):
        lo, hi = c * cw, (c + 1) * cw
        a = jnp.dot(xb, w_ref[:, lo:hi], preferred_element_type=f32) + b_ref[:, lo:hi]
        g = jnp.dot(xb, w_ref[:, D + lo:D + hi], preferred_element_type=f32) + b_ref[:, D + lo:D + hi]
        ybuf[:, lo:hi] = a * jax.nn.sigmoid(g)
    _pack_rows(ybuf[...], yp_ref)


def _glu(x2d, meta_pad, w_in, b_in):
    return pl.pallas_call(
        _glu_kernel,
        grid=(NT,),
        in_specs=[
            pl.BlockSpec((TM, D), lambda i: (jnp.minimum(i, NXT - 1), 0)),
            _resident((TM, D)),
            _resident((D, 2 * D)),
            _resident((1, 2 * D)),
        ],
        out_specs=pl.BlockSpec((TM * PK, LANES), lambda i: (i, 0)),
        out_shape=jax.ShapeDtypeStruct((TP * PK, LANES), u32),
        scratch_shapes=[pltpu.VMEM((TM, D), f32)],
        compiler_params=_cparams(),
        name="glu",
    )(x2d, meta_pad, w_in, b_in)


def _chunk_row(c):
    return 2 * (c % PK) + c // PK


def _conv_kernel(yp_ref, ymeta_ref, w_ref, bdw_ref, g_ref, b_ref, z_ref, scr, accs):
    i = pl.program_id(0)
    hrows = HALO * PK
    trows = TM * PK

    @pl.when(i == 0)
    def _():
        scr[0:(HALO - N_META) * PK, :] = jnp.zeros(((HALO - N_META) * PK, LANES), u32)
        scr[(HALO - N_META) * PK:hrows, :] = ymeta_ref[...]

    @pl.when(i == NT - 1)
    def _():
        scr[0:hrows, :] = jnp.zeros((hrows, LANES), u32)

    @pl.when(jnp.logical_and(i > 0, i < NT - 1))
    def _():
        scr[0:hrows, :] = scr[trows:trows + hrows, :]

    scr[hrows:hrows + trows, :] = yp_ref[...]

    tb = 8
    first = (HALO - (CONV_W - 1)) * PK

    def block(t, carry):
        base = pl.multiple_of(t * (tb * PK), tb * PK)
        acc = jnp.zeros((tb, CHUNKS, LANES), f32)
        for j in range(CONV_W):
            words = scr[pl.ds(base + first + j * PK, tb * PK), :]
            sl = pltpu.bitcast(words, bf16).reshape(tb, CHUNKS, LANES)
            acc = acc + sl.astype(f32) * w_ref[j].astype(f32)[None]
        accs[pl.ds(pl.multiple_of(t * (tb * CHUNKS), tb * CHUNKS), tb * CHUNKS), :] = acc.reshape(tb * CHUNKS, LANES)
        return carry

    lax.fori_loop(0, TM // tb, block, 0)

    rb = 16

    def finish(t, carry):
        r0 = pl.multiple_of(t * rb, rb)
        cols = [accs[pl.ds(r0 * CHUNKS + _chunk_row(c), rb, stride=CHUNKS), :] for c in range(CHUNKS)]
        v = _layer_norm(jnp.concatenate(cols, axis=1) + bdw_ref[...], g_ref[...], b_ref[...])
        z_ref[pl.ds(r0, rb), :] = (v * jax.nn.sigmoid(v)).astype(bf16)
        return carry

    lax.fori_loop(0, TM // rb, finish, 0, unroll=8)


def _conv(yp, w_dw3, b_dw, ln_g, ln_b):
    return pl.pallas_call(
        _conv_kernel,
        grid=(NT,),
        in_specs=[
            pl.BlockSpec((TM * PK, LANES), lambda i: (i, 0)),
            pl.BlockSpec((N_META * PK, LANES), lambda i: (META_ROW // N_META, 0)),
            _resident((CONV_W, CHUNKS, LANES)),
            _resident((1, D)),
            _resident((1, D)),
            _resident((1, D)),
        ],
        out_specs=pl.BlockSpec((TM, D), lambda i: (i, 0)),
        out_shape=jax.ShapeDtypeStruct((TP, D), bf16),
        scratch_shapes=[
            pltpu.VMEM(((TM + HALO) * PK, LANES), u32),
            pltpu.VMEM((TM * CHUNKS, LANES), f32),
        ],
        compiler_params=_cparams(),
        name="conv_ln_swish",
    )(yp, yp, w_dw3, b_dw, ln_g, ln_b)


R_EXP0 = 8


def _route(h, wr_ref, br_ref, running):
    logits = jnp.dot(h.astype(bf16), wr_ref[...], preferred_element_type=f32) + br_ref[...]
    lt = logits.T
    gl = [lt[k:k + 1, :] for k in range(N_GROUPS)]
    gm = functools.reduce(jnp.maximum, gl)
    gex = [jnp.exp(v - gm) for v in gl]
    gden = functools.reduce(lambda a, b: a + b, gex)
    gp = [v / gden for v in gex]
    best = gp[0]
    gi = jnp.zeros((1, TM), i32)
    for k in range(1, N_GROUPS):
        better = gp[k] > best
        gi = jnp.where(better, k, gi)
        best = jnp.where(better, gp[k], best)
    esel = lt[R_EXP0:R_EXP0 + EPG, :]
    for k in range(1, N_GROUPS):
        esel = jnp.where(gi == k, lt[R_EXP0 + EPG * k:R_EXP0 + EPG * (k + 1), :], esel)
    em = jnp.max(esel, axis=0, keepdims=True)
    eex = jnp.exp(esel - em)
    ep = eex / jnp.sum(eex, axis=0, keepdims=True)
    io8 = lax.broadcasted_iota(i32, (EPG, TM), 0)
    v1 = jnp.max(ep, axis=0, keepdims=True)
    i1 = jnp.min(jnp.where(ep == v1, io8, EPG), axis=0, keepdims=True)
    ep2 = jnp.where(io8 == i1, -1.0, ep)
    v2 = jnp.max(ep2, axis=0, keepdims=True)
    i2 = jnp.min(jnp.where(ep2 == v2, io8, EPG), axis=0, keepdims=True)
    s = v1 + v2
    gate0 = best * (v1 / s)
    gate1 = best * (v2 / s)
    f0 = gi * EPG + i1
    f1 = gi * EPG + i2

    io32 = lax.broadcasted_iota(i32, (N_EXP, TM), 0)
    oh0 = (io32 == f0).astype(f32)
    oh1 = (io32 == f1).astype(f32)
    cnt = oh0 + oh1
    upper = (lax.broadcasted_iota(i32, (TM, TM), 0) < lax.broadcasted_iota(i32, (TM, TM), 1))
    before = jnp.dot(cnt.astype(bf16), upper.astype(f32).astype(bf16), preferred_element_type=f32)
    base = running[...] + before
    r0 = jnp.sum(oh0 * base, axis=0, keepdims=True).astype(i32)
    r1 = jnp.sum(oh1 * base, axis=0, keepdims=True).astype(i32)
    running[...] = running[...] + jnp.sum(cnt, axis=1, keepdims=True)

    io128 = lax.broadcasted_iota(i32, (LANES, TM), 0)
    gcol = jnp.where(io128 == 0, gate0, jnp.where(io128 == 1, gate1, 0.0)).T
    return f0, f1, r0, r1, gcol


def _proj_ln_route_body(i, a, res, bias_ref, g_ref, b_ref, wr_ref, br_ref,
                        h_ref, hp_ref, eidx_ref, rank_ref, gcol_ref, cnt_ref, running):
    @pl.when(i == 0)
    def _():
        running[...] = jnp.zeros_like(running)

    mix = a + bias_ref[...]
    h = _layer_norm(ALPHA * res + mix, g_ref[...], b_ref[...])
    h_ref[...] = h
    _pack_rows(h, hp_ref)
    f0, f1, r0, r1, gcol = _route(h, wr_ref, br_ref, running)
    eidx_ref[0, 0:1, :] = f0
    eidx_ref[0, 1:2, :] = f1
    rank_ref[0, 0:1, :] = r0
    rank_ref[0, 1:2, :] = r1
    gcol_ref[...] = gcol
    cnt_ref[...] = running[...]


def _proj_ln_route_kernel_l0(a_ref, x_ref, meta_ref, w_ref, *rest):
    i = pl.program_id(0)
    a = jnp.dot(a_ref[...], w_ref[...], preferred_element_type=f32)
    _proj_ln_route_body(i, a, _x_or_meta(i, x_ref, meta_ref), *rest)


def _proj_ln_route_kernel_l1(aT_ref, res_ref, w_ref, *rest):
    a = lax.dot_general(aT_ref[...], w_ref[...], (((0,), (0,)), ((), ())), preferred_element_type=f32)
    _proj_ln_route_body(pl.program_id(0), a, res_ref[...], *rest)


def _proj_ln_route(a, res, w, bias, ln_g, ln_b, wr, br):
    first = isinstance(res, tuple)
    if first:
        a_spec = pl.BlockSpec((TM, D), lambda i: (i, 0))
        res_specs = [pl.BlockSpec((TM, D), lambda i: (jnp.minimum(i, NXT - 1), 0)), _resident((TM, D))]
        res_args = list(res)
        body = _proj_ln_route_kernel_l0
    else:
        a_spec = pl.BlockSpec((D, TM), lambda i: (0, i))
        res_specs = [pl.BlockSpec((TM, D), lambda i: (i, 0))]
        res_args = [res]
        body = _proj_ln_route_kernel_l1
    tile3 = pl.BlockSpec((1, 2, TM), lambda i: (i, 0, 0))
    return pl.pallas_call(
        body,
        grid=(NT,),
        in_specs=[a_spec] + res_specs + [
            _resident((D, D)), _resident((1, D)), _resident((1, D)), _resident((1, D)),
            _resident((D, LANES)), _resident((1, LANES)),
        ],
        out_specs=[
            pl.BlockSpec((TM, D), lambda i: (i, 0)),
            pl.BlockSpec((TM * PK, LANES), lambda i: (i, 0)),
            tile3, tile3,
            pl.BlockSpec((TM, LANES), lambda i: (i, 0)),
            pl.BlockSpec((N_EXP, TM), lambda i: (0, 0)),
        ],
        out_shape=[
            jax.ShapeDtypeStruct((TP, D), f32),
            jax.ShapeDtypeStruct((TP * PK, LANES), u32),
            jax.ShapeDtypeStruct((NT, 2, TM), i32),
            jax.ShapeDtypeStruct((NT, 2, TM), i32),
            jax.ShapeDtypeStruct((TP, LANES), f32),
            jax.ShapeDtypeStruct((N_EXP, TM), f32),
        ],
        scratch_shapes=[pltpu.VMEM((N_EXP, TM), f32)],
        compiler_params=_cparams(),
        name="proj_ln_route",
    )(a, *res_args, w, bias, ln_g, ln_b, wr, br)


def _plan(eidx, rank, cnt):
    counts = cnt[:, 0].astype(i32)
    padded = ((counts + TME - 1) // TME) * TME
    ends = jnp.cumsum(padded)
    offs = ends - padded
    ntiles = ends[-1] // TME
    off_of = jnp.sum(jnp.where(eidx[..., None] == jnp.arange(N_EXP, dtype=i32), offs, 0), axis=-1)
    dest = (off_of + rank).reshape(-1)
    tile_start = jnp.minimum(jnp.arange(NTE, dtype=i32), ntiles - 1) * TME
    tile_expert = jnp.minimum(jnp.sum(tile_start[:, None] >= ends[None, :], axis=1), N_EXP - 1).astype(i32)
    zstart = jnp.where(padded > 0, ends - TME, 0).astype(i32)
    zflag = (padded > 0).astype(i32)
    return dest.astype(i32), tile_expert, ntiles.reshape(1).astype(i32), zstart, zflag


ISSUE_UNROLL = 8


def _scatter_kernel(dest_ref, zstart_ref, zflag_ref, hp_ref, xs_hbm, zeros, sem, zsem):
    i = pl.program_id(0)

    @pl.when(i == 0)
    def _():
        zeros[...] = jnp.zeros_like(zeros)
        for e in range(N_EXP):
            @pl.when(zflag_ref[e] > 0)
            def _():
                start = pl.multiple_of(zstart_ref[e], TME)
                cp = pltpu.make_async_copy(zeros, xs_hbm.at[pl.ds(start, TME)], zsem)
                cp.start()
                cp.wait()

    base = i * (2 * TM)

    def row(r, carry):
        for k in range(2):
            d = dest_ref[base + k * TM + r]
            pltpu.make_async_copy(hp_ref.at[pl.ds(r, 1)], xs_hbm.at[pl.ds(d, 1)], sem).start(priority=k)
        return carry

    lax.fori_loop(0, TM, row, 0, unroll=ISSUE_UNROLL)

    def drain(r, carry):
        pltpu.make_async_copy(hp_ref.at[pl.ds(0, 1)], xs_hbm.at[pl.ds(0, 1)], sem).wait()
        return carry

    lax.fori_loop(0, 2 * TM, drain, 0, unroll=ISSUE_UNROLL)


def _scatter(dest, zstart, zflag, hp3):
    return pl.pallas_call(
        _scatter_kernel,
        grid_spec=pltpu.PrefetchScalarGridSpec(
            num_scalar_prefetch=3,
            grid=(NT,),
            in_specs=[pl.BlockSpec((TM, PK, LANES), lambda i, *_: (i, 0, 0))],
            out_specs=pl.BlockSpec(memory_space=pl.ANY),
            scratch_shapes=[pltpu.VMEM((TME, PK, LANES), u32), pltpu.SemaphoreType.DMA(()),
                            pltpu.SemaphoreType.DMA(())],
        ),
        out_shape=jax.ShapeDtypeStruct((NS, PK, LANES), u32),
        compiler_params=pltpu.CompilerParams(dimension_semantics=("arbitrary",), vmem_limit_bytes=VMEM_LIMIT,
                                             has_side_effects=True),
        name="moe_scatter",
    )(dest, zstart, zflag, hp3)


def _expert_kernel(te_ref, nt_ref, xs_ref, w1_ref, w3_ref, w2_ref, ys_ref, w1c, w3c, w2c):
    i = pl.program_id(0)
    prev = te_ref[jnp.maximum(i - 1, 0)]

    @pl.when(jnp.logical_and(i < nt_ref[0], jnp.logical_or(i == 0, te_ref[i] != prev)))
    def _():
        w1c[...] = w1_ref[0, 0].astype(bf16)
        w3c[...] = w3_ref[0, 0].astype(bf16)
        w2c[...] = w2_ref[0, 0].astype(bf16)

    @pl.when(i < nt_ref[0])
    def _():
        xlo, xhi = _unpack_rows(xs_ref, TME, bf16)
        a = (jnp.dot(xlo, w1c[0:HALF, :], preferred_element_type=f32)
             + jnp.dot(xhi, w1c[HALF:D, :], preferred_element_type=f32))
        b = (jnp.dot(xlo, w3c[0:HALF, :], preferred_element_type=f32)
             + jnp.dot(xhi, w3c[HALF:D, :], preferred_element_type=f32))
        hid = (a * jax.nn.sigmoid(a) * b).astype(bf16)
        _pack_rows(jnp.dot(hid, w2c[...], preferred_element_type=f32), ys_ref)


def _experts(layer, tile_expert, ntiles, xs2d, w1, w3, w2):
    def row_map(i, te, nt):
        return (jnp.minimum(i, nt[0] - 1), 0)

    def w_map(i, te, nt):
        return (layer, te[i], 0, 0)

    return pl.pallas_call(
        _expert_kernel,
        grid_spec=pltpu.PrefetchScalarGridSpec(
            num_scalar_prefetch=2,
            grid=(NTE,),
            in_specs=[
                pl.BlockSpec((TME * PK, LANES), row_map),
                pl.BlockSpec((1, 1, D, FF), w_map),
                pl.BlockSpec((1, 1, D, FF), w_map),
                pl.BlockSpec((1, 1, FF, D), w_map),
            ],
            out_specs=pl.BlockSpec((TME * PK, LANES), row_map),
            scratch_shapes=[pltpu.VMEM((D, FF), bf16), pltpu.VMEM((D, FF), bf16), pltpu.VMEM((FF, D), bf16)],
        ),
        out_shape=jax.ShapeDtypeStruct((NS * PK, LANES), u32),
        compiler_params=_cparams(),
        name="moe_experts",
    )(tile_expert, ntiles, xs2d, w1, w3, w2)


def _combine_kernel(n, dest_ref, ys_hbm, h_ref, gcol_ref, g_ref, b_ref, o_ref, buf, sem):
    i = pl.program_id(0)
    slot = lax.rem(i, 2)

    def issue(tile, s):
        base = tile * (2 * TM)

        def row(r, carry):
            for k in range(2):
                d = pl.multiple_of(dest_ref[base + k * TM + r] * PK, PK)
                pltpu.make_async_copy(ys_hbm.at[pl.ds(d, PK)],
                                      buf.at[s, k, pl.ds(pl.multiple_of(r * PK, PK), PK)],
                                      sem.at[s]).start(priority=k)
            return carry

        lax.fori_loop(0, TM, row, 0, unroll=ISSUE_UNROLL)

    @pl.when(i == 0)
    def _():
        issue(0, 0)

    @pl.when(i + 1 < n)
    def _():
        issue(i + 1, 1 - slot)

    def drain(r, carry):
        pltpu.make_async_copy(ys_hbm.at[pl.ds(0, PK)], buf.at[slot, 0, pl.ds(0, PK)], sem.at[slot]).wait()
        return carry

    lax.fori_loop(0, 2 * TM, drain, 0, unroll=ISSUE_UNROLL)

    lo0, hi0 = _unpack_rows(buf.at[slot, 0], TM, f32)
    lo1, hi1 = _unpack_rows(buf.at[slot, 1], TM, f32)
    g0, g1 = gcol_ref[:, 0:1], gcol_ref[:, 1:2]
    ffn = jnp.concatenate([lo0 * g0 + lo1 * g1, hi0 * g0 + hi1 * g1], axis=1)
    o_ref[...] = _layer_norm(ALPHA * h_ref[...] + ffn, g_ref[...], b_ref[...])


def _combine(dest, ys2d, h, gcol, ln_g, ln_b, ntiles_out):
    return pl.pallas_call(
        functools.partial(_combine_kernel, ntiles_out),
        grid_spec=pltpu.PrefetchScalarGridSpec(
            num_scalar_prefetch=1,
            grid=(ntiles_out,),
            in_specs=[
                pl.BlockSpec(memory_space=pl.ANY),
                pl.BlockSpec((TM, D), lambda i, *_: (i, 0)),
                pl.BlockSpec((TM, LANES), lambda i, *_: (i, 0)),
                pl.BlockSpec((1, D), lambda i, *_: (0, 0)),
                pl.BlockSpec((1, D), lambda i, *_: (0, 0)),
            ],
            out_specs=pl.BlockSpec((TM, D), lambda i, *_: (i, 0)),
            scratch_shapes=[pltpu.VMEM((2, 2, TM * PK, LANES), u32), pltpu.SemaphoreType.DMA((2,))],
        ),
        out_shape=jax.ShapeDtypeStruct((ntiles_out * TM, D), f32),
        compiler_params=_cparams(),
        name="moe_combine_ln",
    )(dest, ys2d, h, gcol, ln_g, ln_b)


def _moe(layer, h, hp2d, eidx, rank, gcol, cnt, w1, w3, w2, ln_g, ln_b, ntiles_out):
    dest, tile_expert, ntiles, zstart, zflag = _plan(eidx, rank, cnt)
    xs = _scatter(dest, zstart, zflag, hp2d.reshape(TP, PK, LANES))
    ys2d = _experts(layer, tile_expert, ntiles, xs.reshape(NS * PK, LANES), w1, w3, w2)
    return _combine(dest, ys2d, h, gcol, ln_g, ln_b, ntiles_out)


NT_DIMS = (((1,), (1,)), ((), ()))


def _rope_rows(t, cos, sa, sb):
    w = t.shape[1]
    reps = w // LANES
    c = jnp.tile(cos, (1, reps))
    a = jnp.tile(sa, (1, reps))
    b = jnp.tile(sb, (1, reps))
    return t * c + pltpu.roll(t, w - ROT // 2, 1) * a + pltpu.roll(t, ROT // 2, 1) * b


def _qkv_kernel(h_ref, wqT_ref, bq_ref, wk_ref, bk_ref, wvT_ref, bv_ref, cosT_ref, sinT_ref,
                cos_ref, sa_ref, sb_ref, qT_ref, k_ref, vT_ref):
    hb = h_ref[...].astype(bf16)
    scale = 1.0 / math.sqrt(HEAD_DIM)
    half = ROT // 2
    cosT = cosT_ref[...][None]
    sinT = sinT_ref[...][None]
    rows = GQA * HEAD_DIM
    for c in range(D // rows):
        lo, hi = c * rows, (c + 1) * rows
        t = lax.dot_general(wqT_ref[lo:hi, :], hb, NT_DIMS, preferred_element_type=f32) + bq_ref[lo:hi, :]
        t3 = t.reshape(GQA, HEAD_DIM, TM)
        x1, x2 = t3[:, 0:half, :], t3[:, half:ROT, :]
        r = jnp.concatenate([x1 * cosT - x2 * sinT, x2 * cosT + x1 * sinT, t3[:, ROT:, :]], axis=1)
        qT_ref[lo:hi, :] = (r * scale).reshape(rows, TM).astype(bf16)
    t = jnp.dot(hb, wk_ref[...], preferred_element_type=f32) + bk_ref[...]
    k_ref[...] = _rope_rows(t, cos_ref[...], sa_ref[...], sb_ref[...]).astype(bf16)
    t = lax.dot_general(wvT_ref[...], hb, NT_DIMS, preferred_element_type=f32) + bv_ref[...]
    vT_ref[...] = t.astype(bf16)


def _qkv(h, wqT, bq_col, wk, bk, wvT, bv_col, tables):
    cosT, sinT, cos_t, sa_t, sb_t = tables
    tabT = pl.BlockSpec((ROT // 2, TM), lambda i: (0, i))
    tab = pl.BlockSpec((TM, LANES), lambda i: (i, 0))
    return pl.pallas_call(
        _qkv_kernel,
        grid=(NT,),
        in_specs=[
            pl.BlockSpec((TM, D), lambda i: (i, 0)),
            _resident((D, D)), _resident((D, 1)),
            _resident((D, KVW)), _resident((1, KVW)),
            _resident((KVW, D)), _resident((KVW, 1)),
            tabT, tabT, tab, tab, tab,
        ],
        out_specs=[
            pl.BlockSpec((D, TM), lambda i: (0, i)),
            pl.BlockSpec((TM, KVW), lambda i: (i, 0)),
            pl.BlockSpec((KVW, TM), lambda i: (0, i)),
        ],
        out_shape=[
            jax.ShapeDtypeStruct((D, TP), bf16),
            jax.ShapeDtypeStruct((TP, KVW), bf16),
            jax.ShapeDtypeStruct((KVW, TP), bf16),
        ],
        compiler_params=_cparams(),
        name="qkv_rope",
    )(h, wqT, bq_col, wk, bk, wvT, bv_col, cosT, sinT, cos_t, sa_t, sb_t)


NKEY = 2 * QB + N_META
HC = 8
LW = HC * QB
SUB = 8


def _col_max(s):
    parts = [s[r * SUB:(r + 1) * SUB] for r in range(NKEY // SUB)]
    while len(parts) > 1:
        nxt = [jnp.maximum(parts[j], parts[j + 1]) for j in range(0, len(parts) - 1, 2)]
        if len(parts) % 2:
            nxt.append(parts[-1])
        parts = nxt
    return jnp.max(parts[0], axis=0, keepdims=True)


def _attn_kernel(qT_ref, kc_ref, kp_ref, km_ref, vTc_ref, vTp_ref, vTm_ref, sink_ref, oT_ref):
    i = pl.program_id(0)
    is_meta = i == NT - 1
    ck = lax.broadcasted_iota(i32, (NKEY, QB), 0)
    rq = lax.broadcasted_iota(i32, (NKEY, QB), 1)
    in_band = jnp.logical_and(ck > rq, ck <= rq + QB)
    meta_ok = jnp.logical_and(ck >= 2 * QB, jnp.logical_or(jnp.logical_not(is_meta), ck - 2 * QB <= rq))
    ones = jnp.ones((SUB, NKEY), bf16)

    for blk in range(TM // QB):
        lo = jnp.where(is_meta, 2 * QB, jnp.where(jnp.logical_and(i == 0, blk == 0), QB, 0))
        valid = jnp.logical_or(meta_ok, jnp.logical_and(in_band, ck >= lo))
        bias = jnp.where(valid, 0.0, -jnp.inf)
        bias = jnp.concatenate([bias] * HC, axis=1)
        c0 = blk * QB
        for g in range(N_KV):
            gs = slice(g * HEAD_DIM, (g + 1) * HEAD_DIM)
            if blk == 0:
                kprev, vprevT = kp_ref[:, gs], vTp_ref[gs, :]
            else:
                kprev, vprevT = kc_ref[c0 - QB:c0, gs], vTc_ref[gs, c0 - QB:c0]
            kcat = jnp.concatenate([kprev, kc_ref[c0:c0 + QB, gs], km_ref[:, gs]], axis=0)
            vcatT = jnp.concatenate([vprevT, vTc_ref[gs, c0:c0 + QB], vTm_ref[gs, 0:N_META]], axis=1)
            vext = jnp.concatenate([vcatT, ones], axis=0)
            for c in range(GQA // HC):
                h0 = g * GQA + c * HC
                heads = [qT_ref[(h0 + j) * HEAD_DIM:(h0 + j + 1) * HEAD_DIM, c0:c0 + QB] for j in range(HC)]
                s = jnp.dot(kcat, jnp.concatenate(heads, axis=1), preferred_element_type=f32) + bias
                sink = sink_ref[h0 // HC:h0 // HC + 1, :]
                m = jnp.maximum(_col_max(s), sink)
                p = jnp.exp(s - m).astype(bf16)
                oe = jnp.dot(vext, p, preferred_element_type=f32)
                den = oe[HEAD_DIM:HEAD_DIM + 1, :] + jnp.exp(sink - m)
                o = (oe[0:HEAD_DIM, :] * (1.0 / den)).astype(bf16)
                for j in range(HC):
                    oT_ref[(h0 + j) * HEAD_DIM:(h0 + j + 1) * HEAD_DIM, c0:c0 + QB] = o[:, j * QB:(j + 1) * QB]


def _attention(qT, k, vT, sink_lanes):
    prev_blk = lambda i: jnp.maximum(i * (TM // QB) - 1, 0)
    return pl.pallas_call(
        _attn_kernel,
        grid=(NT,),
        in_specs=[
            pl.BlockSpec((D, TM), lambda i: (0, i)),
            pl.BlockSpec((TM, KVW), lambda i: (i, 0)),
            pl.BlockSpec((QB, KVW), lambda i: (prev_blk(i), 0)),
            pl.BlockSpec((N_META, KVW), lambda i: (META_ROW // N_META, 0)),
            pl.BlockSpec((KVW, TM), lambda i: (0, i)),
            pl.BlockSpec((KVW, QB), lambda i: (0, prev_blk(i))),
            pl.BlockSpec((KVW, LANES), lambda i: (0, META_ROW // LANES)),
            _resident((N_HEADS // HC, LW)),
        ],
        out_specs=pl.BlockSpec((D, TM), lambda i: (0, i)),
        out_shape=jax.ShapeDtypeStruct((D, TP), bf16),
        compiler_params=_cparams(),
        name="swa_attention",
    )(qT, k, k, k, vT, vT, vT, sink_lanes)


def _router_weights(wg, bg, we, be):
    wr = jnp.zeros((D, LANES), f32).at[:, 0:N_GROUPS].set(wg).at[:, R_EXP0:R_EXP0 + N_EXP].set(we)
    br = jnp.zeros((1, LANES), f32).at[0, 0:N_GROUPS].set(bg).at[0, R_EXP0:R_EXP0 + N_EXP].set(be)
    return wr.astype(bf16), br


def _rope_tables():
    pos = jnp.concatenate([jnp.arange(SEQ) + N_META, jnp.arange(TM)]).astype(f32)
    half = ROT // 2
    inv_freq = ROPE_THETA ** (-jnp.arange(0, ROT, 2, dtype=f32) / ROT)
    ang = pos[:, None] * inv_freq[None, :]
    cos, sin = jnp.cos(ang), jnp.sin(ang)
    ones = jnp.ones((TP, HEAD_DIM - ROT), f32)
    zeros = jnp.zeros((TP, HEAD_DIM - ROT), f32)
    z8 = jnp.zeros((TP, half), f32)
    cos_h = jnp.concatenate([cos, cos, ones], axis=1)
    sa_h = jnp.concatenate([-sin, z8, zeros], axis=1)
    sb_h = jnp.concatenate([z8, sin, zeros], axis=1)
    rep = LANES // HEAD_DIM
    return cos.T, sin.T, jnp.tile(cos_h, (1, rep)), jnp.tile(sa_h, (1, rep)), jnp.tile(sb_h, (1, rep))


def kernel(x, meta_tokens, conv_w_in, conv_b_in, conv_w_dw, conv_b_dw, conv_ln_g, conv_ln_b, conv_w_out,
           conv_b_out, w_k, b_k, w_v, b_v, w_q, b_q, w_o, b_o, sinks, ln_mix_g, ln_mix_b, ln_ffn_g, ln_ffn_b,
           router_group_w, router_group_b, router_expert_w, router_expert_b, expert_w1, expert_w3, expert_w2):
    assert x.shape == (1, SEQ, D)
    row = lambda v: v.reshape(1, -1)
    col = lambda v: v.reshape(-1, 1)
    x2d = x.reshape(SEQ, D)
    meta_pad = jnp.pad(meta_tokens.astype(f32), ((0, TM - N_META), (0, 0)))

    y = _glu(x2d, meta_pad, conv_w_in[0].astype(bf16), row(conv_b_in[0]))
    w_dw = conv_w_dw[0].reshape(CONV_W, 2, PK, LANES).transpose(0, 2, 1, 3).reshape(CONV_W, CHUNKS, LANES)
    z = _conv(y, w_dw.astype(bf16), row(conv_b_dw[0]), row(conv_ln_g[0]),
              row(conv_ln_b[0]))
    wr, br = _router_weights(router_group_w[0], router_group_b[0], router_expert_w[0], router_expert_b[0])
    h, hp, eidx, rank, gcol, cnt = _proj_ln_route(
        z, (x2d, meta_pad), conv_w_out[0].astype(bf16), row(conv_b_out[0]), row(ln_mix_g[0]), row(ln_mix_b[0]),
        wr, br)
    h = _moe(0, h, hp, eidx, rank, gcol, cnt, expert_w1, expert_w3, expert_w2,
             row(ln_ffn_g[0]), row(ln_ffn_b[0]), NT)

    qT, k, vT = _qkv(h, w_q[0].T.astype(bf16), col(b_q[0]), w_k.astype(bf16), row(b_k),
                     w_v.T.astype(bf16), col(b_v), _rope_tables())
    sink_lanes = jnp.repeat(sinks[0].astype(f32).reshape(N_HEADS // HC, HC), QB, axis=1)
    attT = _attention(qT, k, vT, sink_lanes)
    wr, br = _router_weights(router_group_w[1], router_group_b[1], router_expert_w[1], router_expert_b[1])
    h, hp, eidx, rank, gcol, cnt = _proj_ln_route(
        attT, h, w_o[0].astype(bf16), row(b_o[0]), row(ln_mix_g[1]), row(ln_mix_b[1]), wr, br)
    out = _moe(1, h, hp, eidx, rank, gcol, cnt, expert_w1, expert_w3, expert_w2,
               row(ln_ffn_g[1]), row(ln_ffn_b[1]), NXT)
    return out.reshape(1, SEQ, D)
```

```python
import functools
import math

import jax
import jax.numpy as jnp
from jax import lax
from jax.experimental import pallas as pl
from jax.experimental.pallas import tpu as pltpu

f32 = jnp.float32
bf16 = jnp.bfloat16
i32 = jnp.int32
u32 = jnp.uint32

D = 2048
SEQ = 8192
DEPTH = 2
N_META = 16
CONV_W = 31
HEAD_DIM = 64
N_HEADS = 32
N_KV = 4
GQA = 8
KVW = N_KV * HEAD_DIM
WINDOW = 128
ROT = 16
ROPE_THETA = 500000.0
N_GROUPS = 4
EPG = 8
N_EXP = 32
FF = 256
ALPHA = (2.0 * DEPTH) ** 0.25
LN_EPS = 1e-5

LANES = 128
TM = 256
NXT = SEQ // TM
NT = NXT + 1
TP = NT * TM
META_ROW = SEQ
CHUNKS = D // LANES
HALO = 32
TME = 256
NTE = (2 * TP) // TME + N_EXP
NS = NTE * TME
QB = 128
VMEM_LIMIT = 52 * 1024 * 1024


def _cparams():
    return pltpu.CompilerParams(dimension_semantics=("arbitrary",), vmem_limit_bytes=VMEM_LIMIT)


def _resident(shape):
    nd = len(shape)
    return pl.BlockSpec(shape, lambda *a: (0,) * nd, pipeline_mode=pl.Buffered(1))


def _layer_norm(x, g, b):
    mu = jnp.mean(x, axis=-1, keepdims=True)
    xc = x - mu
    var = jnp.mean(xc * xc, axis=-1, keepdims=True)
    return xc * lax.rsqrt(var + LN_EPS) * g + b


def _x_or_meta(i, x_ref, meta_ref):
    return jnp.where(i == NXT, meta_ref[...], x_ref[...])


HALF = D // 2
PK = HALF // LANES


def _pack_rows(v, out2d):
    rows = v.shape[0]
    bits = pltpu.bitcast(v.astype(bf16).astype(f32), u32)
    word = bits[:, HALF:] | lax.shift_right_logical(bits[:, :HALF], jnp.uint32(16))
    for s in range(PK):
        out2d[pl.ds(s, rows, stride=PK), :] = word[:, s * LANES:(s + 1) * LANES]


def _unpack_rows(in2d, rows, dtype):
    lo, hi = [], []
    for s in range(PK):
        w = in2d[pl.ds(s, rows, stride=PK), :]
        lo.append(pltpu.bitcast(lax.shift_left(w, jnp.uint32(16)), f32).astype(dtype))
        hi.append(pltpu.bitcast(w & jnp.uint32(0xFFFF0000), f32).astype(dtype))
    return jnp.concatenate(lo, axis=1), jnp.concatenate(hi, axis=1)


def _glu_kernel(x_ref, meta_ref, w_ref, b_ref, yp_ref, ybuf):
    i = pl.program_id(0)
    xb = _x_or_meta(i, x_ref, meta_ref).astype(bf16)
    cw = 512
    for c in range(D // cw):
        lo, hi = c * cw, (c + 1) * cw
        a = jnp.dot(xb, w_ref[:, lo:hi], preferred_element_type=f32) + b_ref[:, lo:hi]
        g = jnp.dot(xb, w_ref[:, D + lo:D + hi], preferred_element_type=f32) + b_ref[:, D + lo:D + hi]
        ybuf[:, lo:hi] = a * jax.nn.sigmoid(g)
    _pack_rows(ybuf[...], yp_ref)


def _glu(x2d, meta_pad, w_in, b_in):
    return pl.pallas_call(
        _glu_kernel,
        grid=(NT,),
        in_specs=[
            pl.BlockSpec((TM, D), lambda i: (jnp.minimum(i, NXT - 1), 0)),
            _resident((TM, D)),
            _resident((D, 2 * D)),
            _resident((1, 2 * D)),
        ],
        out_specs=pl.BlockSpec((TM * PK, LANES), lambda i: (i, 0)),
        out_shape=jax.ShapeDtypeStruct((TP * PK, LANES), u32),
        scratch_shapes=[pltpu.VMEM((TM, D), f32)],
        compiler_params=_cparams(),
        name="glu",
    )(x2d, meta_pad, w_in, b_in)


def _chunk_row(c):
    return 2 * (c % PK) + c // PK


def _conv_kernel(yp_ref, ymeta_ref, w_ref, bdw_ref, g_ref, b_ref, z_ref, scr, accs):
    i = pl.program_id(0)
    hrows = HALO * PK
    trows = TM * PK

    @pl.when(i == 0)
    def _():
        scr[0:(HALO - N_META) * PK, :] = jnp.zeros(((HALO - N_META) * PK, LANES), u32)
        scr[(HALO - N_META) * PK:hrows, :] = ymeta_ref[...]

    @pl.when(i == NT - 1)
    def _():
        scr[0:hrows, :] = jnp.zeros((hrows, LANES), u32)

    @pl.when(jnp.logical_and(i > 0, i < NT - 1))
    def _():
        scr[0:hrows, :] = scr[trows:trows + hrows, :]

    scr[hrows:hrows + trows, :] = yp_ref[...]

    tb = 8
    first = (HALO - (CONV_W - 1)) * PK

    def block(t, carry):
        base = pl.multiple_of(t * (tb * PK), tb * PK)
        acc = jnp.zeros((tb, CHUNKS, LANES), f32)
        for j in range(CONV_W):
            words = scr[pl.ds(base + first + j * PK, tb * PK), :]
            sl = pltpu.bitcast(words, bf16).reshape(tb, CHUNKS, LANES)
            acc = acc + sl.astype(f32) * w_ref[j].astype(f32)[None]
        accs[pl.ds(pl.multiple_of(t * (tb * CHUNKS), tb * CHUNKS), tb * CHUNKS), :] = acc.reshape(tb * CHUNKS, LANES)
        return carry

    lax.fori_loop(0, TM // tb, block, 0)

    rb = 16

    def finish(t, carry):
        r0 = pl.multiple_of(t * rb, rb)
        cols = [accs[pl.ds(r0 * CHUNKS + _chunk_row(c), rb, stride=CHUNKS), :] for c in range(CHUNKS)]
        v = _layer_norm(jnp.concatenate(cols, axis=1) + bdw_ref[...], g_ref[...], b_ref[...])
        z_ref[pl.ds(r0, rb), :] = (v * jax.nn.sigmoid(v)).astype(bf16)
        return carry

    lax.fori_loop(0, TM // rb, finish, 0, unroll=8)


def _conv(yp, w_dw3, b_dw, ln_g, ln_b):
    return pl.pallas_call(
        _conv_kernel,
        grid=(NT,),
        in_specs=[
            pl.BlockSpec((TM * PK, LANES), lambda i: (i, 0)),
            pl.BlockSpec((N_META * PK, LANES), lambda i: (META_ROW // N_META, 0)),
            _resident((CONV_W, CHUNKS, LANES)),
            _resident((1, D)),
            _resident((1, D)),
            _resident((1, D)),
        ],
        out_specs=pl.BlockSpec((TM, D), lambda i: (i, 0)),
        out_shape=jax.ShapeDtypeStruct((TP, D), bf16),
        scratch_shapes=[
            pltpu.VMEM(((TM + HALO) * PK, LANES), u32),
            pltpu.VMEM((TM * CHUNKS, LANES), f32),
        ],
        compiler_params=_cparams(),
        name="conv_ln_swish",
    )(yp, yp, w_dw3, b_dw, ln_g, ln_b)


R_EXP0 = 8


def _route(h, wr_ref, br_ref, running, valid):
    logits = jnp.dot(h.astype(bf16), wr_ref[...], preferred_element_type=f32) + br_ref[...]
    lt = logits.T
    gl = [lt[k:k + 1, :] for k in range(N_GROUPS)]
    gm = functools.reduce(jnp.maximum, gl)
    gex = [jnp.exp(v - gm) for v in gl]
    gden = functools.reduce(lambda a, b: a + b, gex)
    gp = [v / gden for v in gex]
    best = gp[0]
    gi = jnp.zeros((1, TM), i32)
    for k in range(1, N_GROUPS):
        better = gp[k] > best
        gi = jnp.where(better, k, gi)
        best = jnp.where(better, gp[k], best)
    esel = lt[R_EXP0:R_EXP0 + EPG, :]
    for k in range(1, N_GROUPS):
        esel = jnp.where(gi == k, lt[R_EXP0 + EPG * k:R_EXP0 + EPG * (k + 1), :], esel)
    em = jnp.max(esel, axis=0, keepdims=True)
    eex = jnp.exp(esel - em)
    ep = eex / jnp.sum(eex, axis=0, keepdims=True)
    io8 = lax.broadcasted_iota(i32, (EPG, TM), 0)
    v1 = jnp.max(ep, axis=0, keepdims=True)
    i1 = jnp.min(jnp.where(ep == v1, io8, EPG), axis=0, keepdims=True)
    ep2 = jnp.where(io8 == i1, -1.0, ep)
    v2 = jnp.max(ep2, axis=0, keepdims=True)
    i2 = jnp.min(jnp.where(ep2 == v2, io8, EPG), axis=0, keepdims=True)
    s = v1 + v2
    gate0 = best * (v1 / s)
    gate1 = best * (v2 / s)
    f0 = gi * EPG + i1
    f1 = gi * EPG + i2

    io32 = lax.broadcasted_iota(i32, (N_EXP, TM), 0)
    oh0 = (io32 == f0).astype(f32)
    oh1 = (io32 == f1).astype(f32)
    cnt = oh0 + oh1
    upper = (lax.broadcasted_iota(i32, (TM, TM), 0) < lax.broadcasted_iota(i32, (TM, TM), 1))
    before = jnp.dot(cnt.astype(bf16), upper.astype(f32).astype(bf16), preferred_element_type=f32)
    base = running[...] + before
    r0 = jnp.sum(oh0 * base, axis=0, keepdims=True).astype(i32)
    r1 = jnp.sum(oh1 * base, axis=0, keepdims=True).astype(i32)
    running[...] = running[...] + valid * jnp.sum(cnt, axis=1, keepdims=True)

    io128 = lax.broadcasted_iota(i32, (LANES, TM), 0)
    gcol = jnp.where(io128 == 0, gate0, jnp.where(io128 == 1, gate1, 0.0)).T
    return f0, f1, r0, r1, gcol


def _proj_ln_route_epilogue(valid, a, res, bias_ref, g_ref, b_ref, wr_ref, br_ref,
                            h_ref, hp_ref, eidx_ref, rank_ref, gcol_ref, cnt_ref, running):
    mix = a + bias_ref[...]
    h = _layer_norm(ALPHA * res + mix, g_ref[...], b_ref[...])
    h_ref[...] = h
    _pack_rows(h, hp_ref)
    f0, f1, r0, r1, gcol = _route(h, wr_ref, br_ref, running, valid)
    eidx_ref[0, 0:1, :] = f0
    eidx_ref[0, 1:2, :] = f1
    rank_ref[0, 0:1, :] = r0
    rank_ref[0, 1:2, :] = r1
    gcol_ref[...] = gcol
    cnt_ref[...] = running[...]


def _proj_ln_route_kernel(first, a_ref, *refs):
    if first:
        x_ref, meta_ref, w_ref, *rest = refs
    else:
        res_ref, w_ref, *rest = refs
    *rest, running, accbuf = rest
    i = pl.program_id(0)

    @pl.when(i == 0)
    def _():
        running[...] = jnp.zeros_like(running)
        accbuf[1] = jnp.zeros((TM, D), f32)

    valid = (i > 0).astype(f32)
    for parity in range(2):
        @pl.when(lax.rem(i, 2) == parity)
        def _():
            prev = accbuf[1 - parity]
            if first:
                accbuf[parity] = jnp.dot(a_ref[...], w_ref[...], preferred_element_type=f32)
                res = jnp.where(i - 1 == NXT, meta_ref[...], x_ref[...])
            else:
                accbuf[parity] = lax.dot_general(a_ref[...], w_ref[...], (((0,), (0,)), ((), ())),
                                                 preferred_element_type=f32)
                res = res_ref[...]
            _proj_ln_route_epilogue(valid, prev, res, *rest, running)


def _proj_ln_route(a, res, w, bias, ln_g, ln_b, wr, br):
    first = isinstance(res, tuple)
    cur = lambda i: jnp.minimum(i, NT - 1)
    prv = lambda i: jnp.maximum(i - 1, 0)
    if first:
        a_spec = pl.BlockSpec((TM, D), lambda i: (cur(i), 0))
        res_specs = [pl.BlockSpec((TM, D), lambda i: (jnp.minimum(prv(i), NXT - 1), 0)), _resident((TM, D))]
        res_args = list(res)
    else:
        a_spec = pl.BlockSpec((D, TM), lambda i: (0, cur(i)))
        res_specs = [pl.BlockSpec((TM, D), lambda i: (prv(i), 0))]
        res_args = [res]
    tile3 = pl.BlockSpec((1, 2, TM), lambda i: (prv(i), 0, 0))
    return pl.pallas_call(
        functools.partial(_proj_ln_route_kernel, first),
        grid=(NT + 1,),
        in_specs=[a_spec] + res_specs + [
            _resident((D, D)), _resident((1, D)), _resident((1, D)), _resident((1, D)),
            _resident((D, LANES)), _resident((1, LANES)),
        ],
        out_specs=[
            pl.BlockSpec((TM, D), lambda i: (prv(i), 0)),
            pl.BlockSpec((TM * PK, LANES), lambda i: (prv(i), 0)),
            tile3, tile3,
            pl.BlockSpec((TM, LANES), lambda i: (prv(i), 0)),
            pl.BlockSpec((N_EXP, TM), lambda i: (0, 0)),
        ],
        out_shape=[
            jax.ShapeDtypeStruct((TP, D), f32),
            jax.ShapeDtypeStruct((TP * PK, LANES), u32),
            jax.ShapeDtypeStruct((NT, 2, TM), i32),
            jax.ShapeDtypeStruct((NT, 2, TM), i32),
            jax.ShapeDtypeStruct((TP, LANES), f32),
            jax.ShapeDtypeStruct((N_EXP, TM), f32),
        ],
        scratch_shapes=[pltpu.VMEM((N_EXP, TM), f32), pltpu.VMEM((2, TM, D), f32)],
        compiler_params=_cparams(),
        name="proj_ln_route",
    )(a, *res_args, w, bias, ln_g, ln_b, wr, br)


def _plan(eidx, rank, cnt):
    counts = cnt[:, 0].astype(i32)
    padded = ((counts + TME - 1) // TME) * TME
    ends = jnp.cumsum(padded)
    offs = ends - padded
    ntiles = ends[-1] // TME
    off_of = jnp.sum(jnp.where(eidx[..., None] == jnp.arange(N_EXP, dtype=i32), offs, 0), axis=-1)
    dest = (off_of + rank).reshape(-1)
    tile_start = jnp.minimum(jnp.arange(NTE, dtype=i32), ntiles - 1) * TME
    tile_expert = jnp.minimum(jnp.sum(tile_start[:, None] >= ends[None, :], axis=1), N_EXP - 1).astype(i32)
    zstart = jnp.where(padded > 0, ends - TME, 0).astype(i32)
    zflag = (padded > 0).astype(i32)
    tid = jnp.arange(NTE, dtype=i32)
    live = tid < ntiles
    first = jnp.logical_and(live, jnp.logical_or(tid == 0, tile_expert != jnp.roll(tile_expert, 1)))
    slot = (jnp.cumsum(first.astype(i32)) - 1) % 2
    nxt_first = lax.cummin(jnp.where(first, tid, NTE), reverse=True)
    after = jnp.concatenate([nxt_first[1:], jnp.full((1,), NTE, i32)])
    nxt = jnp.where(after < NTE, tile_expert[jnp.minimum(after, NTE - 1)], -1)
    return (dest.astype(i32), tile_expert, ntiles.reshape(1).astype(i32), zstart, zflag,
            first.astype(i32), slot.astype(i32), nxt.astype(i32))


ISSUE_UNROLL = 8


def _scatter_kernel(dest_ref, zstart_ref, zflag_ref, hp_ref, xs_hbm, zeros, sem, zsem):
    i = pl.program_id(0)

    @pl.when(i == 0)
    def _():
        zeros[...] = jnp.zeros_like(zeros)
        for e in range(N_EXP):
            @pl.when(zflag_ref[e] > 0)
            def _():
                start = pl.multiple_of(zstart_ref[e], TME)
                cp = pltpu.make_async_copy(zeros, xs_hbm.at[pl.ds(start, TME)], zsem)
                cp.start()
                cp.wait()

    base = i * (2 * TM)

    def row(r, carry):
        for k in range(2):
            d = dest_ref[base + k * TM + r]
            pltpu.make_async_copy(hp_ref.at[pl.ds(r, 1)], xs_hbm.at[pl.ds(d, 1)], sem).start(priority=k)
        return carry

    lax.fori_loop(0, TM, row, 0, unroll=ISSUE_UNROLL)

    def drain(r, carry):
        pltpu.make_async_copy(hp_ref.at[pl.ds(0, 1)], xs_hbm.at[pl.ds(0, 1)], sem).wait()
        return carry

    lax.fori_loop(0, 2 * TM, drain, 0, unroll=ISSUE_UNROLL)


def _scatter(dest, zstart, zflag, hp3):
    return pl.pallas_call(
        _scatter_kernel,
        grid_spec=pltpu.PrefetchScalarGridSpec(
            num_scalar_prefetch=3,
            grid=(NT,),
            in_specs=[pl.BlockSpec((TM, PK, LANES), lambda i, *_: (i, 0, 0))],
            out_specs=pl.BlockSpec(memory_space=pl.ANY),
            scratch_shapes=[pltpu.VMEM((TME, PK, LANES), u32), pltpu.SemaphoreType.DMA(()),
                            pltpu.SemaphoreType.DMA(())],
        ),
        out_shape=jax.ShapeDtypeStruct((NS, PK, LANES), u32),
        compiler_params=pltpu.CompilerParams(dimension_semantics=("arbitrary",), vmem_limit_bytes=VMEM_LIMIT,
                                             has_side_effects=True),
        name="moe_scatter",
    )(dest, zstart, zflag, hp3)


def _expert_kernel(layer, te_ref, nt_ref, first_ref, slot_ref, nxt_ref, xs_ref, w1_hbm, w3_hbm, w2_hbm, ys_ref,
                   wb1, wb3, wb2, w1c, w3c, w2c, sem):
    i = pl.program_id(0)

    def copies(e, s):
        return (pltpu.make_async_copy(w1_hbm.at[layer, e], wb1.at[s], sem.at[s]),
                pltpu.make_async_copy(w3_hbm.at[layer, e], wb3.at[s], sem.at[s]),
                pltpu.make_async_copy(w2_hbm.at[layer, e], wb2.at[s], sem.at[s]))

    @pl.when(i == 0)
    def _():
        for cp in copies(te_ref[0], 0):
            cp.start()

    @pl.when(jnp.logical_and(i < nt_ref[0], first_ref[i] > 0))
    def _():
        s = slot_ref[i]
        for cp in copies(te_ref[i], s):
            cp.wait()

        @pl.when(nxt_ref[i] >= 0)
        def _():
            for cp in copies(nxt_ref[i], 1 - s):
                cp.start()

        w1c[...] = wb1[s].astype(bf16)
        w3c[...] = wb3[s].astype(bf16)
        w2c[...] = wb2[s].astype(bf16)

    @pl.when(i < nt_ref[0])
    def _():
        xlo, xhi = _unpack_rows(xs_ref, TME, bf16)
        a = (jnp.dot(xlo, w1c[0:HALF, :], preferred_element_type=f32)
             + jnp.dot(xhi, w1c[HALF:D, :], preferred_element_type=f32))
        b = (jnp.dot(xlo, w3c[0:HALF, :], preferred_element_type=f32)
             + jnp.dot(xhi, w3c[HALF:D, :], preferred_element_type=f32))
        hid = (a * jax.nn.sigmoid(a) * b).astype(bf16)
        _pack_rows(jnp.dot(hid, w2c[...], preferred_element_type=f32), ys_ref)


def _experts(layer, tile_expert, ntiles, first, slot, nxt, xs2d, w1, w3, w2):
    def row_map(i, te, nt, *_):
        return (jnp.minimum(i, nt[0] - 1), 0)

    hbm = pl.BlockSpec(memory_space=pl.ANY)
    return pl.pallas_call(
        functools.partial(_expert_kernel, layer),
        grid_spec=pltpu.PrefetchScalarGridSpec(
            num_scalar_prefetch=5,
            grid=(NTE,),
            in_specs=[pl.BlockSpec((TME * PK, LANES), row_map), hbm, hbm, hbm],
            out_specs=pl.BlockSpec((TME * PK, LANES), row_map),
            scratch_shapes=[
                pltpu.VMEM((2, D, FF), f32), pltpu.VMEM((2, D, FF), f32), pltpu.VMEM((2, FF, D), f32),
                pltpu.VMEM((D, FF), bf16), pltpu.VMEM((D, FF), bf16), pltpu.VMEM((FF, D), bf16),
                pltpu.SemaphoreType.DMA((2,)),
            ],
        ),
        out_shape=jax.ShapeDtypeStruct((NS * PK, LANES), u32),
        compiler_params=_cparams(),
        name="moe_experts",
    )(tile_expert, ntiles, first, slot, nxt, xs2d, w1, w3, w2)


def _combine_kernel(n, dest_ref, ys_hbm, h_ref, gcol_ref, g_ref, b_ref, o_ref, buf, sem):
    i = pl.program_id(0)
    slot = lax.rem(i, 2)

    def issue(tile, s):
        base = tile * (2 * TM)

        def row(r, carry):
            for k in range(2):
                d = pl.multiple_of(dest_ref[base + k * TM + r] * PK, PK)
                pltpu.make_async_copy(ys_hbm.at[pl.ds(d, PK)],
                                      buf.at[s, k, pl.ds(pl.multiple_of(r * PK, PK), PK)],
                                      sem.at[s]).start(priority=k)
            return carry

        lax.fori_loop(0, TM, row, 0, unroll=ISSUE_UNROLL)

    @pl.when(i == 0)
    def _():
        issue(0, 0)

    @pl.when(i + 1 < n)
    def _():
        issue(i + 1, 1 - slot)

    def drain(r, carry):
        pltpu.make_async_copy(ys_hbm.at[pl.ds(0, PK)], buf.at[slot, 0, pl.ds(0, PK)], sem.at[slot]).wait()
        return carry

    lax.fori_loop(0, 2 * TM, drain, 0, unroll=ISSUE_UNROLL)

    lo0, hi0 = _unpack_rows(buf.at[slot, 0], TM, f32)
    lo1, hi1 = _unpack_rows(buf.at[slot, 1], TM, f32)
    g0, g1 = gcol_ref[:, 0:1], gcol_ref[:, 1:2]
    ffn = jnp.concatenate([lo0 * g0 + lo1 * g1, hi0 * g0 + hi1 * g1], axis=1)
    o_ref[...] = _layer_norm(ALPHA * h_ref[...] + ffn, g_ref[...], b_ref[...])


def _combine(dest, ys2d, h, gcol, ln_g, ln_b, ntiles_out):
    return pl.pallas_call(
        functools.partial(_combine_kernel, ntiles_out),
        grid_spec=pltpu.PrefetchScalarGridSpec(
            num_scalar_prefetch=1,
            grid=(ntiles_out,),
            in_specs=[
                pl.BlockSpec(memory_space=pl.ANY),
                pl.BlockSpec((TM, D), lambda i, *_: (i, 0)),
                pl.BlockSpec((TM, LANES), lambda i, *_: (i, 0)),
                pl.BlockSpec((1, D), lambda i, *_: (0, 0)),
                pl.BlockSpec((1, D), lambda i, *_: (0, 0)),
            ],
            out_specs=pl.BlockSpec((TM, D), lambda i, *_: (i, 0)),
            scratch_shapes=[pltpu.VMEM((2, 2, TM * PK, LANES), u32), pltpu.SemaphoreType.DMA((2,))],
        ),
        out_shape=jax.ShapeDtypeStruct((ntiles_out * TM, D), f32),
        compiler_params=_cparams(),
        name="moe_combine_ln",
    )(dest, ys2d, h, gcol, ln_g, ln_b)


def _moe(layer, h, hp2d, eidx, rank, gcol, cnt, w1, w3, w2, ln_g, ln_b, ntiles_out):
    dest, tile_expert, ntiles, zstart, zflag, first, slot, nxt = _plan(eidx, rank, cnt)
    xs = _scatter(dest, zstart, zflag, hp2d.reshape(TP, PK, LANES))
    ys2d = _experts(layer, tile_expert, ntiles, first, slot, nxt, xs.reshape(NS * PK, LANES), w1, w3, w2)
    return _combine(dest, ys2d, h, gcol, ln_g, ln_b, ntiles_out)


NT_DIMS = (((1,), (1,)), ((), ()))


def _rope_rows(t, cos, sa, sb):
    w = t.shape[1]
    reps = w // LANES
    c = jnp.tile(cos, (1, reps))
    a = jnp.tile(sa, (1, reps))
    b = jnp.tile(sb, (1, reps))
    return t * c + pltpu.roll(t, w - ROT // 2, 1) * a + pltpu.roll(t, ROT // 2, 1) * b


def _qkv_kernel(h_ref, wqT_ref, bq_ref, wk_ref, bk_ref, wvT_ref, bv_ref, cosT_ref, sinT_ref,
                cos_ref, sa_ref, sb_ref, qT_ref, k_ref, vT_ref):
    hb = h_ref[...].astype(bf16)
    scale = 1.0 / math.sqrt(HEAD_DIM)
    half = ROT // 2
    cosT = cosT_ref[...][None]
    sinT = sinT_ref[...][None]
    rows = GQA * HEAD_DIM
    for c in range(D // rows):
        lo, hi = c * rows, (c + 1) * rows
        t = lax.dot_general(wqT_ref[lo:hi, :], hb, NT_DIMS, preferred_element_type=f32) + bq_ref[lo:hi, :]
        t3 = t.reshape(GQA, HEAD_DIM, TM)
        x1, x2 = t3[:, 0:half, :], t3[:, half:ROT, :]
        r = jnp.concatenate([x1 * cosT - x2 * sinT, x2 * cosT + x1 * sinT, t3[:, ROT:, :]], axis=1)
        qT_ref[lo:hi, :] = (r * scale).reshape(rows, TM).astype(bf16)
    t = jnp.dot(hb, wk_ref[...], preferred_element_type=f32) + bk_ref[...]
    k_ref[...] = _rope_rows(t, cos_ref[...], sa_ref[...], sb_ref[...]).astype(bf16)
    t = lax.dot_general(wvT_ref[...], hb, NT_DIMS, preferred_element_type=f32) + bv_ref[...]
    vT_ref[...] = t.astype(bf16)


def _qkv(h, wqT, bq_col, wk, bk, wvT, bv_col, tables):
    cosT, sinT, cos_t, sa_t, sb_t = tables
    tabT = pl.BlockSpec((ROT // 2, TM), lambda i: (0, i))
    tab = pl.BlockSpec((TM, LANES), lambda i: (i, 0))
    return pl.pallas_call(
        _qkv_kernel,
        grid=(NT,),
        in_specs=[
            pl.BlockSpec((TM, D), lambda i: (i, 0)),
            _resident((D, D)), _resident((D, 1)),
            _resident((D, KVW)), _resident((1, KVW)),
            _resident((KVW, D)), _resident((KVW, 1)),
            tabT, tabT, tab, tab, tab,
        ],
        out_specs=[
            pl.BlockSpec((D, TM), lambda i: (0, i)),
            pl.BlockSpec((TM, KVW), lambda i: (i, 0)),
            pl.BlockSpec((KVW, TM), lambda i: (0, i)),
        ],
        out_shape=[
            jax.ShapeDtypeStruct((D, TP), bf16),
            jax.ShapeDtypeStruct((TP, KVW), bf16),
            jax.ShapeDtypeStruct((KVW, TP), bf16),
        ],
        compiler_params=_cparams(),
        name="qkv_rope",
    )(h, wqT, bq_col, wk, bk, wvT, bv_col, cosT, sinT, cos_t, sa_t, sb_t)


NKEY = 2 * QB + N_META
HC = 8
LW = HC * QB
SUB = 8


def _col_max(s):
    parts = [s[r * SUB:(r + 1) * SUB] for r in range(NKEY // SUB)]
    while len(parts) > 1:
        nxt = [jnp.maximum(parts[j], parts[j + 1]) for j in range(0, len(parts) - 1, 2)]
        if len(parts) % 2:
            nxt.append(parts[-1])
        parts = nxt
    return jnp.max(parts[0], axis=0, keepdims=True)


def _attn_kernel(qT_ref, kc_ref, kp_ref, km_ref, vTc_ref, vTp_ref, vTm_ref, sink_ref, oT_ref):
    i = pl.program_id(0)
    is_meta = i == NT - 1
    ck = lax.broadcasted_iota(i32, (NKEY, QB), 0)
    rq = lax.broadcasted_iota(i32, (NKEY, QB), 1)
    in_band = jnp.logical_and(ck > rq, ck <= rq + QB)
    meta_ok = jnp.logical_and(ck >= 2 * QB, jnp.logical_or(jnp.logical_not(is_meta), ck - 2 * QB <= rq))
    ones = jnp.ones((SUB, NKEY), bf16)

    for blk in range(TM // QB):
        lo = jnp.where(is_meta, 2 * QB, jnp.where(jnp.logical_and(i == 0, blk == 0), QB, 0))
        valid = jnp.logical_or(meta_ok, jnp.logical_and(in_band, ck >= lo))
        bias = jnp.where(valid, 0.0, -jnp.inf)
        bias = jnp.concatenate([bias] * HC, axis=1)
        c0 = blk * QB
        for g in range(N_KV):
            gs = slice(g * HEAD_DIM, (g + 1) * HEAD_DIM)
            if blk == 0:
                kprev, vprevT = kp_ref[:, gs], vTp_ref[gs, :]
            else:
                kprev, vprevT = kc_ref[c0 - QB:c0, gs], vTc_ref[gs, c0 - QB:c0]
            kcat = jnp.concatenate([kprev, kc_ref[c0:c0 + QB, gs], km_ref[:, gs]], axis=0)
            vcatT = jnp.concatenate([vprevT, vTc_ref[gs, c0:c0 + QB], vTm_ref[gs, 0:N_META]], axis=1)
            vext = jnp.concatenate([vcatT, ones], axis=0)
            for c in range(GQA // HC):
                h0 = g * GQA + c * HC
                heads = [qT_ref[(h0 + j) * HEAD_DIM:(h0 + j + 1) * HEAD_DIM, c0:c0 + QB] for j in range(HC)]
                s = jnp.dot(kcat, jnp.concatenate(heads, axis=1), preferred_element_type=f32) + bias
                sink = sink_ref[h0 // HC:h0 // HC + 1, :]
                m = jnp.maximum(_col_max(s), sink)
                p = jnp.exp(s - m).astype(bf16)
                oe = jnp.dot(vext, p, preferred_element_type=f32)
                den = oe[HEAD_DIM:HEAD_DIM + 1, :] + jnp.exp(sink - m)
                o = (oe[0:HEAD_DIM, :] * (1.0 / den)).astype(bf16)
                for j in range(HC):
                    oT_ref[(h0 + j) * HEAD_DIM:(h0 + j + 1) * HEAD_DIM, c0:c0 + QB] = o[:, j * QB:(j + 1) * QB]


def _attention(qT, k, vT, sink_lanes):
    prev_blk = lambda i: jnp.maximum(i * (TM // QB) - 1, 0)
    return pl.pallas_call(
        _attn_kernel,
        grid=(NT,),
        in_specs=[
            pl.BlockSpec((D, TM), lambda i: (0, i)),
            pl.BlockSpec((TM, KVW), lambda i: (i, 0)),
            pl.BlockSpec((QB, KVW), lambda i: (prev_blk(i), 0)),
            pl.BlockSpec((N_META, KVW), lambda i: (META_ROW // N_META, 0)),
            pl.BlockSpec((KVW, TM), lambda i: (0, i)),
            pl.BlockSpec((KVW, QB), lambda i: (0, prev_blk(i))),
            pl.BlockSpec((KVW, LANES), lambda i: (0, META_ROW // LANES)),
            _resident((N_HEADS // HC, LW)),
        ],
        out_specs=pl.BlockSpec((D, TM), lambda i: (0, i)),
        out_shape=jax.ShapeDtypeStruct((D, TP), bf16),
        compiler_params=_cparams(),
        name="swa_attention",
    )(qT, k, k, k, vT, vT, vT, sink_lanes)


def _router_weights(wg, bg, we, be):
    wr = jnp.zeros((D, LANES), f32).at[:, 0:N_GROUPS].set(wg).at[:, R_EXP0:R_EXP0 + N_EXP].set(we)
    br = jnp.zeros((1, LANES), f32).at[0, 0:N_GROUPS].set(bg).at[0, R_EXP0:R_EXP0 + N_EXP].set(be)
    return wr.astype(bf16), br


def _rope_tables():
    pos = jnp.concatenate([jnp.arange(SEQ) + N_META, jnp.arange(TM)]).astype(f32)
    half = ROT // 2
    inv_freq = ROPE_THETA ** (-jnp.arange(0, ROT, 2, dtype=f32) / ROT)
    ang = pos[:, None] * inv_freq[None, :]
    cos, sin = jnp.cos(ang), jnp.sin(ang)
    ones = jnp.ones((TP, HEAD_DIM - ROT), f32)
    zeros = jnp.zeros((TP, HEAD_DIM - ROT), f32)
    z8 = jnp.zeros((TP, half), f32)
    cos_h = jnp.concatenate([cos, cos, ones], axis=1)
    sa_h = jnp.concatenate([-sin, z8, zeros], axis=1)
    sb_h = jnp.concatenate([z8, sin, zeros], axis=1)
    rep = LANES // HEAD_DIM
    return cos.T, sin.T, jnp.tile(cos_h, (1, rep)), jnp.tile(sa_h, (1, rep)), jnp.tile(sb_h, (1, rep))


def kernel(x, meta_tokens, conv_w_in, conv_b_in, conv_w_dw, conv_b_dw, conv_ln_g, conv_ln_b, conv_w_out,
           conv_b_out, w_k, b_k, w_v, b_v, w_q, b_q, w_o, b_o, sinks, ln_mix_g, ln_mix_b, ln_ffn_g, ln_ffn_b,
           router_group_w, router_group_b, router_expert_w, router_expert_b, expert_w1, expert_w3, expert_w2):
    assert x.shape == (1, SEQ, D)
    row = lambda v: v.reshape(1, -1)
    col = lambda v: v.reshape(-1, 1)
    x2d = x.reshape(SEQ, D)
    meta_pad = jnp.pad(meta_tokens.astype(f32), ((0, TM - N_META), (0, 0)))

    y = _glu(x2d, meta_pad, conv_w_in[0].astype(bf16), row(conv_b_in[0]))
    w_dw = conv_w_dw[0].reshape(CONV_W, 2, PK, LANES).transpose(0, 2, 1, 3).reshape(CONV_W, CHUNKS, LANES)
    z = _conv(y, w_dw.astype(bf16), row(conv_b_dw[0]), row(conv_ln_g[0]),
              row(conv_ln_b[0]))
    wr, br = _router_weights(router_group_w[0], router_group_b[0], router_expert_w[0], router_expert_b[0])
    h, hp, eidx, rank, gcol, cnt = _proj_ln_route(
        z, (x2d, meta_pad), conv_w_out[0].astype(bf16), row(conv_b_out[0]), row(ln_mix_g[0]), row(ln_mix_b[0]),
        wr, br)
    h = _moe(0, h, hp, eidx, rank, gcol, cnt, expert_w1, expert_w3, expert_w2,
             row(ln_ffn_g[0]), row(ln_ffn_b[0]), NT)

    qT, k, vT = _qkv(h, w_q[0].T.astype(bf16), col(b_q[0]), w_k.astype(bf16), row(b_k),
                     w_v.T.astype(bf16), col(b_v), _rope_tables())
    sink_lanes = jnp.repeat(sinks[0].astype(f32).reshape(N_HEADS // HC, HC), QB, axis=1)
    attT = _attention(qT, k, vT, sink_lanes)
    wr, br = _router_weights(router_group_w[1], router_group_b[1], router_expert_w[1], router_expert_b[1])
    h, hp, eidx, rank, gcol, cnt = _proj_ln_route(
        attT, h, w_o[0].astype(bf16), row(b_o[0]), row(ln_mix_g[1]), row(ln_mix_b[1]), wr, br)
    out = _moe(1, h, hp, eidx, rank, gcol, cnt, expert_w1, expert_w3, expert_w2,
               row(ln_ffn_g[1]), row(ln_ffn_b[1]), NXT)
    return out.reshape(1, SEQ, D)
```

```python
import functools
import math

import jax
import jax.numpy as jnp
import numpy as np
from jax import lax
from jax.experimental import pallas as pl
from jax.experimental.pallas import tpu as pltpu

f32 = jnp.float32
bf16 = jnp.bfloat16
i32 = jnp.int32
u32 = jnp.uint32

D = 2048
SEQ = 8192
DEPTH = 2
N_META = 16
CONV_W = 31
HEAD_DIM = 64
N_HEADS = 32
N_KV = 4
GQA = 8
KVW = N_KV * HEAD_DIM
WINDOW = 128
ROT = 16
ROPE_THETA = 500000.0
N_GROUPS = 4
EPG = 8
N_EXP = 32
FF = 256
ALPHA = (2.0 * DEPTH) ** 0.25
LN_EPS = 1e-5

LANES = 128
TM = 256
NXT = SEQ // TM
NT = NXT + 1
TP = NT * TM
META_ROW = SEQ
CHUNKS = D // LANES
HALO = 32
TME = 256
NTE = (2 * TP) // TME + N_EXP
NS = NTE * TME
QB = 128
VMEM_LIMIT = 52 * 1024 * 1024


def _cparams():
    return pltpu.CompilerParams(dimension_semantics=("arbitrary",), vmem_limit_bytes=VMEM_LIMIT)


def _resident(shape):
    nd = len(shape)
    return pl.BlockSpec(shape, lambda *a: (0,) * nd, pipeline_mode=pl.Buffered(1))


def _layer_norm(x, g, b):
    mu = jnp.mean(x, axis=-1, keepdims=True)
    xc = x - mu
    var = jnp.mean(xc * xc, axis=-1, keepdims=True)
    return xc * lax.rsqrt(var + LN_EPS) * g + b


def _x_or_meta(i, x_ref, meta_ref):
    return jnp.where(i == NXT, meta_ref[...], x_ref[...])


HALF = D // 2
PK = HALF // LANES


def _pack_rows(v, out2d):
    rows = v.shape[0]
    bits = pltpu.bitcast(v.astype(bf16).astype(f32), u32)
    word = bits[:, HALF:] | lax.shift_right_logical(bits[:, :HALF], jnp.uint32(16))
    for s in range(PK):
        out2d[pl.ds(s, rows, stride=PK), :] = word[:, s * LANES:(s + 1) * LANES]


def _unpack_rows(in2d, rows, dtype):
    lo, hi = [], []
    for s in range(PK):
        w = in2d[pl.ds(s, rows, stride=PK), :]
        lo.append(pltpu.bitcast(lax.shift_left(w, jnp.uint32(16)), f32).astype(dtype))
        hi.append(pltpu.bitcast(w & jnp.uint32(0xFFFF0000), f32).astype(dtype))
    return jnp.concatenate(lo, axis=1), jnp.concatenate(hi, axis=1)


def _load_weights_bf16(w_hbm, w_vmem, stage, sem):
    rc = stage.shape[1]
    n = w_hbm.shape[0] // rc

    def cp(c):
        return pltpu.make_async_copy(w_hbm.at[pl.ds(c * rc, rc)], stage.at[c % 2], sem.at[c % 2])

    cp(0).start()
    for c in range(n):
        cp(c).wait()
        if c + 1 < n:
            cp(c + 1).start()
        w_vmem[c * rc:(c + 1) * rc, :] = stage[c % 2].astype(bf16)


def _weight_scratch(rows, cols, chunk_rows):
    return [pltpu.VMEM((rows, cols), bf16), pltpu.VMEM((2, chunk_rows, cols), f32), pltpu.SemaphoreType.DMA((2,))]


HBM = pl.BlockSpec(memory_space=pl.ANY)


def _glu_kernel(x_ref, meta_ref, w_hbm, b_ref, yp_ref, ybuf, w_ref, stage, sem):
    i = pl.program_id(0)

    @pl.when(i == 0)
    def _():
        _load_weights_bf16(w_hbm, w_ref, stage, sem)

    xb = _x_or_meta(i, x_ref, meta_ref).astype(bf16)
    cw = 512
    for c in range(D // cw):
        lo, hi = c * cw, (c + 1) * cw
        a = jnp.dot(xb, w_ref[:, lo:hi], preferred_element_type=f32) + b_ref[:, lo:hi]
        g = jnp.dot(xb, w_ref[:, D + lo:D + hi], preferred_element_type=f32) + b_ref[:, D + lo:D + hi]
        ybuf[:, lo:hi] = a * jax.nn.sigmoid(g)
    _pack_rows(ybuf[...], yp_ref)


def _glu(x2d, meta_pad, w_in, b_in):
    return pl.pallas_call(
        _glu_kernel,
        grid=(NT,),
        in_specs=[
            pl.BlockSpec((TM, D), lambda i: (jnp.minimum(i, NXT - 1), 0)),
            _resident((TM, D)),
            HBM,
            _resident((1, 2 * D)),
        ],
        out_specs=pl.BlockSpec((TM * PK, LANES), lambda i: (i, 0)),
        out_shape=jax.ShapeDtypeStruct((TP * PK, LANES), u32),
        scratch_shapes=[pltpu.VMEM((TM, D), f32)] + _weight_scratch(D, 2 * D, 256),
        compiler_params=_cparams(),
        name="glu",
    )(x2d, meta_pad, w_in, b_in)


def _chunk_row(c):
    return 2 * (c % PK) + c // PK


def _conv_kernel(yp_ref, ymeta_ref, w_ref, bdw_ref, g_ref, b_ref, z_ref, scr, accs):
    i = pl.program_id(0)
    hrows = HALO * PK
    trows = TM * PK

    @pl.when(i == 0)
    def _():
        scr[0:(HALO - N_META) * PK, :] = jnp.zeros(((HALO - N_META) * PK, LANES), u32)
        scr[(HALO - N_META) * PK:hrows, :] = ymeta_ref[...]

    @pl.when(i == NT - 1)
    def _():
        scr[0:hrows, :] = jnp.zeros((hrows, LANES), u32)

    @pl.when(jnp.logical_and(i > 0, i < NT - 1))
    def _():
        scr[0:hrows, :] = scr[trows:trows + hrows, :]

    scr[hrows:hrows + trows, :] = yp_ref[...]

    tb = 8
    first = (HALO - (CONV_W - 1)) * PK

    def block(t, carry):
        base = pl.multiple_of(t * (tb * PK), tb * PK)
        acc = jnp.zeros((tb, CHUNKS, LANES), f32)
        for j in range(CONV_W):
            words = scr[pl.ds(base + first + j * PK, tb * PK), :]
            sl = pltpu.bitcast(words, bf16).reshape(tb, CHUNKS, LANES)
            acc = acc + sl.astype(f32) * w_ref[j].astype(f32)[None]
        accs[pl.ds(pl.multiple_of(t * (tb * CHUNKS), tb * CHUNKS), tb * CHUNKS), :] = acc.reshape(tb * CHUNKS, LANES)
        return carry

    lax.fori_loop(0, TM // tb, block, 0)

    rb = 16

    def finish(t, carry):
        r0 = pl.multiple_of(t * rb, rb)
        cols = [accs[pl.ds(r0 * CHUNKS + _chunk_row(c), rb, stride=CHUNKS), :] for c in range(CHUNKS)]
        v = _layer_norm(jnp.concatenate(cols, axis=1) + bdw_ref[...], g_ref[...], b_ref[...])
        z_ref[pl.ds(r0, rb), :] = (v * jax.nn.sigmoid(v)).astype(bf16)
        return carry

    lax.fori_loop(0, TM // rb, finish, 0, unroll=8)


def _conv(yp, w_dw3, b_dw, ln_g, ln_b):
    return pl.pallas_call(
        _conv_kernel,
        grid=(NT,),
        in_specs=[
            pl.BlockSpec((TM * PK, LANES), lambda i: (i, 0)),
            pl.BlockSpec((N_META * PK, LANES), lambda i: (META_ROW // N_META, 0)),
            _resident((CONV_W, CHUNKS, LANES)),
            _resident((1, D)),
            _resident((1, D)),
            _resident((1, D)),
        ],
        out_specs=pl.BlockSpec((TM, D), lambda i: (i, 0)),
        out_shape=jax.ShapeDtypeStruct((TP, D), bf16),
        scratch_shapes=[
            pltpu.VMEM(((TM + HALO) * PK, LANES), u32),
            pltpu.VMEM((TM * CHUNKS, LANES), f32),
        ],
        compiler_params=_cparams(),
        name="conv_ln_swish",
    )(yp, yp, w_dw3, b_dw, ln_g, ln_b)


R_EXP0 = 8


def _route(h, wr_ref, br_ref, running, valid):
    logits = jnp.dot(h.astype(bf16), wr_ref[...], preferred_element_type=f32) + br_ref[...]
    lt = logits.T
    gl = [lt[k:k + 1, :] for k in range(N_GROUPS)]
    gm = functools.reduce(jnp.maximum, gl)
    gex = [jnp.exp(v - gm) for v in gl]
    gden = functools.reduce(lambda a, b: a + b, gex)
    gp = [v / gden for v in gex]
    best = gp[0]
    gi = jnp.zeros((1, TM), i32)
    for k in range(1, N_GROUPS):
        better = gp[k] > best
        gi = jnp.where(better, k, gi)
        best = jnp.where(better, gp[k], best)
    esel = lt[R_EXP0:R_EXP0 + EPG, :]
    for k in range(1, N_GROUPS):
        esel = jnp.where(gi == k, lt[R_EXP0 + EPG * k:R_EXP0 + EPG * (k + 1), :], esel)
    em = jnp.max(esel, axis=0, keepdims=True)
    eex = jnp.exp(esel - em)
    ep = eex / jnp.sum(eex, axis=0, keepdims=True)
    io8 = lax.broadcasted_iota(i32, (EPG, TM), 0)
    v1 = jnp.max(ep, axis=0, keepdims=True)
    i1 = jnp.min(jnp.where(ep == v1, io8, EPG), axis=0, keepdims=True)
    ep2 = jnp.where(io8 == i1, -1.0, ep)
    v2 = jnp.max(ep2, axis=0, keepdims=True)
    i2 = jnp.min(jnp.where(ep2 == v2, io8, EPG), axis=0, keepdims=True)
    s = v1 + v2
    gate0 = best * (v1 / s)
    gate1 = best * (v2 / s)
    f0 = gi * EPG + i1
    f1 = gi * EPG + i2

    io32 = lax.broadcasted_iota(i32, (N_EXP, TM), 0)
    oh0 = (io32 == f0).astype(f32)
    oh1 = (io32 == f1).astype(f32)
    cnt = oh0 + oh1
    upper = (lax.broadcasted_iota(i32, (TM, TM), 0) < lax.broadcasted_iota(i32, (TM, TM), 1))
    before = jnp.dot(cnt.astype(bf16), upper.astype(f32).astype(bf16), preferred_element_type=f32)
    base = running[...] + before
    r0 = jnp.sum(oh0 * base, axis=0, keepdims=True).astype(i32)
    r1 = jnp.sum(oh1 * base, axis=0, keepdims=True).astype(i32)
    running[...] = running[...] + valid * jnp.sum(cnt, axis=1, keepdims=True)

    io128 = lax.broadcasted_iota(i32, (LANES, TM), 0)
    gcol = jnp.where(io128 == 0, gate0, jnp.where(io128 == 1, gate1, 0.0)).T
    return f0, f1, r0, r1, gcol


def _proj_ln_route_epilogue(valid, a, res, bias_ref, g_ref, b_ref, wr_ref, br_ref,
                            h_ref, hp_ref, eidx_ref, rank_ref, gcol_ref, cnt_ref, running):
    mix = a + bias_ref[...]
    h = _layer_norm(ALPHA * res + mix, g_ref[...], b_ref[...])
    h_ref[...] = h
    _pack_rows(h, hp_ref)
    f0, f1, r0, r1, gcol = _route(h, wr_ref, br_ref, running, valid)
    eidx_ref[0, 0:1, :] = f0
    eidx_ref[0, 1:2, :] = f1
    rank_ref[0, 0:1, :] = r0
    rank_ref[0, 1:2, :] = r1
    gcol_ref[...] = gcol
    cnt_ref[...] = running[...]


def _proj_ln_route_kernel(first, a_ref, *refs):
    if first:
        x_ref, meta_ref, w_hbm, *rest = refs
    else:
        res_ref, w_hbm, *rest = refs
    *rest, running, accbuf, w_ref, stage, sem = rest
    i = pl.program_id(0)

    @pl.when(i == 0)
    def _():
        running[...] = jnp.zeros_like(running)
        accbuf[1] = jnp.zeros((TM, D), f32)
        _load_weights_bf16(w_hbm, w_ref, stage, sem)

    valid = (i > 0).astype(f32)
    for parity in range(2):
        @pl.when(lax.rem(i, 2) == parity)
        def _():
            prev = accbuf[1 - parity]
            if first:
                accbuf[parity] = jnp.dot(a_ref[...], w_ref[...], preferred_element_type=f32)
                res = jnp.where(i - 1 == NXT, meta_ref[...], x_ref[...])
            else:
                accbuf[parity] = lax.dot_general(a_ref[...], w_ref[...], (((0,), (0,)), ((), ())),
                                                 preferred_element_type=f32)
                res = res_ref[...]
            _proj_ln_route_epilogue(valid, prev, res, *rest, running)


def _proj_ln_route(a, res, w, bias, ln_g, ln_b, wr, br):
    first = isinstance(res, tuple)
    cur = lambda i: jnp.minimum(i, NT - 1)
    prv = lambda i: jnp.maximum(i - 1, 0)
    if first:
        a_spec = pl.BlockSpec((TM, D), lambda i: (cur(i), 0))
        res_specs = [pl.BlockSpec((TM, D), lambda i: (jnp.minimum(prv(i), NXT - 1), 0)), _resident((TM, D))]
        res_args = list(res)
    else:
        a_spec = pl.BlockSpec((D, TM), lambda i: (0, cur(i)))
        res_specs = [pl.BlockSpec((TM, D), lambda i: (prv(i), 0))]
        res_args = [res]
    tile3 = pl.BlockSpec((1, 2, TM), lambda i: (prv(i), 0, 0))
    return pl.pallas_call(
        functools.partial(_proj_ln_route_kernel, first),
        grid=(NT + 1,),
        in_specs=[a_spec] + res_specs + [
            HBM, _resident((1, D)), _resident((1, D)), _resident((1, D)),
            _resident((D, LANES)), _resident((1, LANES)),
        ],
        out_specs=[
            pl.BlockSpec((TM, D), lambda i: (prv(i), 0)),
            pl.BlockSpec((TM * PK, LANES), lambda i: (prv(i), 0)),
            tile3, tile3,
            pl.BlockSpec((TM, LANES), lambda i: (prv(i), 0)),
            pl.BlockSpec((N_EXP, TM), lambda i: (0, 0)),
        ],
        out_shape=[
            jax.ShapeDtypeStruct((TP, D), f32),
            jax.ShapeDtypeStruct((TP * PK, LANES), u32),
            jax.ShapeDtypeStruct((NT, 2, TM), i32),
            jax.ShapeDtypeStruct((NT, 2, TM), i32),
            jax.ShapeDtypeStruct((TP, LANES), f32),
            jax.ShapeDtypeStruct((N_EXP, TM), f32),
        ],
        scratch_shapes=[pltpu.VMEM((N_EXP, TM), f32), pltpu.VMEM((2, TM, D), f32)] + _weight_scratch(D, D, 512),
        compiler_params=_cparams(),
        name="proj_ln_route",
    )(a, *res_args, w, bias, ln_g, ln_b, wr, br)


def _plan(eidx, rank, cnt):
    counts = cnt[:, 0].astype(i32)
    padded = ((counts + TME - 1) // TME) * TME
    ends = jnp.cumsum(padded)
    offs = ends - padded
    ntiles = ends[-1] // TME
    off_of = jnp.sum(jnp.where(eidx[..., None] == jnp.arange(N_EXP, dtype=i32), offs, 0), axis=-1)
    dest = (off_of + rank).reshape(-1)
    tile_start = jnp.minimum(jnp.arange(NTE, dtype=i32), ntiles - 1) * TME
    tile_expert = jnp.minimum(jnp.sum(tile_start[:, None] >= ends[None, :], axis=1), N_EXP - 1).astype(i32)
    zstart = jnp.where(padded > 0, ends - TME, 0).astype(i32)
    zflag = (padded > 0).astype(i32)
    tid = jnp.arange(NTE, dtype=i32)
    live = tid < ntiles
    first = jnp.logical_and(live, jnp.logical_or(tid == 0, tile_expert != jnp.roll(tile_expert, 1)))
    slot = (jnp.cumsum(first.astype(i32)) - 1) % 2
    nxt_first = lax.cummin(jnp.where(first, tid, NTE), reverse=True)
    after = jnp.concatenate([nxt_first[1:], jnp.full((1,), NTE, i32)])
    nxt = jnp.where(after < NTE, tile_expert[jnp.minimum(after, NTE - 1)], -1)
    return (dest.astype(i32), tile_expert, ntiles.reshape(1).astype(i32), zstart, zflag,
            first.astype(i32), slot.astype(i32), nxt.astype(i32))


ISSUE_UNROLL = 8


def _scatter_kernel(dest_ref, zstart_ref, zflag_ref, hp_ref, xs_hbm, zeros, sem, zsem):
    i = pl.program_id(0)

    @pl.when(i == 0)
    def _():
        zeros[...] = jnp.zeros_like(zeros)
        def fill(e):
            start = pl.multiple_of(zstart_ref[e], TME)
            return pltpu.make_async_copy(zeros, xs_hbm.at[pl.ds(start, TME)], zsem)

        for e in range(N_EXP):
            @pl.when(zflag_ref[e] > 0)
            def _():
                fill(e).start()
        for e in range(N_EXP):
            @pl.when(zflag_ref[e] > 0)
            def _():
                fill(e).wait()

    base = i * (2 * TM)

    def row(r, carry):
        for k in range(2):
            d = dest_ref[base + k * TM + r]
            pltpu.make_async_copy(hp_ref.at[pl.ds(r, 1)], xs_hbm.at[pl.ds(d, 1)], sem).start(priority=k)
        return carry

    lax.fori_loop(0, TM, row, 0, unroll=ISSUE_UNROLL)

    def drain(r, carry):
        pltpu.make_async_copy(hp_ref.at[pl.ds(0, 1)], xs_hbm.at[pl.ds(0, 1)], sem).wait()
        return carry

    lax.fori_loop(0, 2 * TM, drain, 0, unroll=ISSUE_UNROLL)


def _scatter(dest, zstart, zflag, hp3):
    return pl.pallas_call(
        _scatter_kernel,
        grid_spec=pltpu.PrefetchScalarGridSpec(
            num_scalar_prefetch=3,
            grid=(NT,),
            in_specs=[pl.BlockSpec((TM, PK, LANES), lambda i, *_: (i, 0, 0))],
            out_specs=pl.BlockSpec(memory_space=pl.ANY),
            scratch_shapes=[pltpu.VMEM((TME, PK, LANES), u32), pltpu.SemaphoreType.DMA(()),
                            pltpu.SemaphoreType.DMA(())],
        ),
        out_shape=jax.ShapeDtypeStruct((NS, PK, LANES), u32),
        compiler_params=pltpu.CompilerParams(dimension_semantics=("arbitrary",), vmem_limit_bytes=VMEM_LIMIT,
                                             has_side_effects=True),
        name="moe_scatter",
    )(dest, zstart, zflag, hp3)


def _expert_kernel(layer, te_ref, nt_ref, first_ref, slot_ref, nxt_ref, xs_ref, w1_hbm, w3_hbm, w2_hbm, ys_ref,
                   wb1, wb3, wb2, w1c, w3c, w2c, sem):
    i = pl.program_id(0)

    def copies(e, s):
        return (pltpu.make_async_copy(w1_hbm.at[layer, e], wb1.at[s], sem.at[s]),
                pltpu.make_async_copy(w3_hbm.at[layer, e], wb3.at[s], sem.at[s]),
                pltpu.make_async_copy(w2_hbm.at[layer, e], wb2.at[s], sem.at[s]))

    @pl.when(i == 0)
    def _():
        for cp in copies(te_ref[0], 0):
            cp.start()

    @pl.when(jnp.logical_and(i < nt_ref[0], first_ref[i] > 0))
    def _():
        s = slot_ref[i]
        for cp in copies(te_ref[i], s):
            cp.wait()

        @pl.when(nxt_ref[i] >= 0)
        def _():
            for cp in copies(nxt_ref[i], 1 - s):
                cp.start()

        w1c[...] = wb1[s].astype(bf16)
        w3c[...] = wb3[s].astype(bf16)
        w2c[...] = wb2[s].astype(bf16)

    @pl.when(i < nt_ref[0])
    def _():
        xlo, xhi = _unpack_rows(xs_ref, TME, bf16)
        a = (jnp.dot(xlo, w1c[0:HALF, :], preferred_element_type=f32)
             + jnp.dot(xhi, w1c[HALF:D, :], preferred_element_type=f32))
        b = (jnp.dot(xlo, w3c[0:HALF, :], preferred_element_type=f32)
             + jnp.dot(xhi, w3c[HALF:D, :], preferred_element_type=f32))
        hid = (a * jax.nn.sigmoid(a) * b).astype(bf16)
        _pack_rows(jnp.dot(hid, w2c[...], preferred_element_type=f32), ys_ref)


def _experts(layer, tile_expert, ntiles, first, slot, nxt, xs2d, w1, w3, w2):
    def row_map(i, te, nt, *_):
        return (jnp.minimum(i, nt[0] - 1), 0)

    hbm = pl.BlockSpec(memory_space=pl.ANY)
    return pl.pallas_call(
        functools.partial(_expert_kernel, layer),
        grid_spec=pltpu.PrefetchScalarGridSpec(
            num_scalar_prefetch=5,
            grid=(NTE,),
            in_specs=[pl.BlockSpec((TME * PK, LANES), row_map), hbm, hbm, hbm],
            out_specs=pl.BlockSpec((TME * PK, LANES), row_map),
            scratch_shapes=[
                pltpu.VMEM((2, D, FF), f32), pltpu.VMEM((2, D, FF), f32), pltpu.VMEM((2, FF, D), f32),
                pltpu.VMEM((D, FF), bf16), pltpu.VMEM((D, FF), bf16), pltpu.VMEM((FF, D), bf16),
                pltpu.SemaphoreType.DMA((2,)),
            ],
        ),
        out_shape=jax.ShapeDtypeStruct((NS * PK, LANES), u32),
        compiler_params=_cparams(),
        name="moe_experts",
    )(tile_expert, ntiles, first, slot, nxt, xs2d, w1, w3, w2)


def _combine_kernel(n, dest_ref, ys_hbm, h_ref, gcol_ref, g_ref, b_ref, o_ref, buf, sem):
    i = pl.program_id(0)
    slot = lax.rem(i, 2)

    def issue(tile, s):
        base = tile * (2 * TM)

        def row(r, carry):
            for k in range(2):
                d = pl.multiple_of(dest_ref[base + k * TM + r] * PK, PK)
                pltpu.make_async_copy(ys_hbm.at[pl.ds(d, PK)],
                                      buf.at[s, k, pl.ds(pl.multiple_of(r * PK, PK), PK)],
                                      sem.at[s]).start(priority=k)
            return carry

        lax.fori_loop(0, TM, row, 0, unroll=ISSUE_UNROLL)

    @pl.when(i == 0)
    def _():
        issue(0, 0)

    @pl.when(i + 1 < n)
    def _():
        issue(i + 1, 1 - slot)

    def drain(r, carry):
        pltpu.make_async_copy(ys_hbm.at[pl.ds(0, PK)], buf.at[slot, 0, pl.ds(0, PK)], sem.at[slot]).wait()
        return carry

    lax.fori_loop(0, 2 * TM, drain, 0, unroll=ISSUE_UNROLL)

    lo0, hi0 = _unpack_rows(buf.at[slot, 0], TM, f32)
    lo1, hi1 = _unpack_rows(buf.at[slot, 1], TM, f32)
    g0, g1 = gcol_ref[:, 0:1], gcol_ref[:, 1:2]
    ffn = jnp.concatenate([lo0 * g0 + lo1 * g1, hi0 * g0 + hi1 * g1], axis=1)
    o_ref[...] = _layer_norm(ALPHA * h_ref[...] + ffn, g_ref[...], b_ref[...])


def _combine(dest, ys2d, h, gcol, ln_g, ln_b, ntiles_out):
    return pl.pallas_call(
        functools.partial(_combine_kernel, ntiles_out),
        grid_spec=pltpu.PrefetchScalarGridSpec(
            num_scalar_prefetch=1,
            grid=(ntiles_out,),
            in_specs=[
                pl.BlockSpec(memory_space=pl.ANY),
                pl.BlockSpec((TM, D), lambda i, *_: (i, 0)),
                pl.BlockSpec((TM, LANES), lambda i, *_: (i, 0)),
                pl.BlockSpec((1, D), lambda i, *_: (0, 0)),
                pl.BlockSpec((1, D), lambda i, *_: (0, 0)),
            ],
            out_specs=pl.BlockSpec((TM, D), lambda i, *_: (i, 0)),
            scratch_shapes=[pltpu.VMEM((2, 2, TM * PK, LANES), u32), pltpu.SemaphoreType.DMA((2,))],
        ),
        out_shape=jax.ShapeDtypeStruct((ntiles_out * TM, D), f32),
        compiler_params=_cparams(),
        name="moe_combine_ln",
    )(dest, ys2d, h, gcol, ln_g, ln_b)


def _moe(layer, h, hp2d, eidx, rank, gcol, cnt, w1, w3, w2, ln_g, ln_b, ntiles_out):
    dest, tile_expert, ntiles, zstart, zflag, first, slot, nxt = _plan(eidx, rank, cnt)
    xs = _scatter(dest, zstart, zflag, hp2d.reshape(TP, PK, LANES))
    ys2d = _experts(layer, tile_expert, ntiles, first, slot, nxt, xs.reshape(NS * PK, LANES), w1, w3, w2)
    return _combine(dest, ys2d, h, gcol, ln_g, ln_b, ntiles_out)


NT_DIMS = (((1,), (1,)), ((), ()))


def _rope_rows(t, cos, sa, sb):
    w = t.shape[1]
    reps = w // LANES
    c = jnp.tile(cos, (1, reps))
    a = jnp.tile(sa, (1, reps))
    b = jnp.tile(sb, (1, reps))
    return t * c + pltpu.roll(t, w - ROT // 2, 1) * a + pltpu.roll(t, ROT // 2, 1) * b


def _qkv_kernel(h_ref, wq_hbm, bq_ref, wk_ref, bk_ref, wvT_ref, bv_ref, cosT_ref, sinT_ref,
                cos_ref, sa_ref, sb_ref, qT_ref, k_ref, vT_ref, wq_ref, stage, sem):
    @pl.when(pl.program_id(0) == 0)
    def _():
        _load_weights_bf16(wq_hbm, wq_ref, stage, sem)

    hb = h_ref[...].astype(bf16)
    scale = 1.0 / math.sqrt(HEAD_DIM)
    half = ROT // 2
    cosT = cosT_ref[...][None]
    sinT = sinT_ref[...][None]
    rows = GQA * HEAD_DIM
    for c in range(D // rows):
        lo, hi = c * rows, (c + 1) * rows
        t = lax.dot_general(wq_ref[:, lo:hi], hb, (((0,), (1,)), ((), ())), preferred_element_type=f32) \
            + bq_ref[lo:hi, :]
        t3 = t.reshape(GQA, HEAD_DIM, TM)
        x1, x2 = t3[:, 0:half, :], t3[:, half:ROT, :]
        r = jnp.concatenate([x1 * cosT - x2 * sinT, x2 * cosT + x1 * sinT, t3[:, ROT:, :]], axis=1)
        qT_ref[lo:hi, :] = (r * scale).reshape(rows, TM).astype(bf16)
    t = jnp.dot(hb, wk_ref[...], preferred_element_type=f32) + bk_ref[...]
    k_ref[...] = _rope_rows(t, cos_ref[...], sa_ref[...], sb_ref[...]).astype(bf16)
    t = lax.dot_general(wvT_ref[...], hb, NT_DIMS, preferred_element_type=f32) + bv_ref[...]
    vT_ref[...] = t.astype(bf16)


def _qkv(h, wq, bq_col, wk, bk, wvT, bv_col, tables):
    cosT, sinT, cos_t, sa_t, sb_t = tables
    tabT = pl.BlockSpec((ROT // 2, TM), lambda i: (0, i))
    tab = pl.BlockSpec((TM, LANES), lambda i: (i, 0))
    return pl.pallas_call(
        _qkv_kernel,
        grid=(NT,),
        in_specs=[
            pl.BlockSpec((TM, D), lambda i: (i, 0)),
            HBM, _resident((D, 1)),
            _resident((D, KVW)), _resident((1, KVW)),
            _resident((KVW, D)), _resident((KVW, 1)),
            tabT, tabT, tab, tab, tab,
        ],
        out_specs=[
            pl.BlockSpec((D, TM), lambda i: (0, i)),
            pl.BlockSpec((TM, KVW), lambda i: (i, 0)),
            pl.BlockSpec((KVW, TM), lambda i: (0, i)),
        ],
        out_shape=[
            jax.ShapeDtypeStruct((D, TP), bf16),
            jax.ShapeDtypeStruct((TP, KVW), bf16),
            jax.ShapeDtypeStruct((KVW, TP), bf16),
        ],
        scratch_shapes=_weight_scratch(D, D, 512),
        compiler_params=_cparams(),
        name="qkv_rope",
    )(h, wq, bq_col, wk, bk, wvT, bv_col, cosT, sinT, cos_t, sa_t, sb_t)


NKEY = 2 * QB + N_META
HC = 8
LW = HC * QB
SUB = 8


def _col_max(s):
    parts = [s[r * SUB:(r + 1) * SUB] for r in range(NKEY // SUB)]
    while len(parts) > 1:
        nxt = [jnp.maximum(parts[j], parts[j + 1]) for j in range(0, len(parts) - 1, 2)]
        if len(parts) % 2:
            nxt.append(parts[-1])
        parts = nxt
    return jnp.max(parts[0], axis=0, keepdims=True)


def _attn_kernel(qT_ref, kc_ref, kp_ref, km_ref, vTc_ref, vTp_ref, vTm_ref, sink_ref, oT_ref):
    i = pl.program_id(0)
    is_meta = i == NT - 1
    ck = lax.broadcasted_iota(i32, (NKEY, QB), 0)
    rq = lax.broadcasted_iota(i32, (NKEY, QB), 1)
    in_band = jnp.logical_and(ck > rq, ck <= rq + QB)
    meta_ok = jnp.logical_and(ck >= 2 * QB, jnp.logical_or(jnp.logical_not(is_meta), ck - 2 * QB <= rq))
    ones = jnp.ones((SUB, NKEY), bf16)

    for blk in range(TM // QB):
        lo = jnp.where(is_meta, 2 * QB, jnp.where(jnp.logical_and(i == 0, blk == 0), QB, 0))
        valid = jnp.logical_or(meta_ok, jnp.logical_and(in_band, ck >= lo))
        bias = jnp.where(valid, 0.0, -jnp.inf)
        bias = jnp.concatenate([bias] * HC, axis=1)
        c0 = blk * QB
        for g in range(N_KV):
            gs = slice(g * HEAD_DIM, (g + 1) * HEAD_DIM)
            if blk == 0:
                kprev, vprevT = kp_ref[:, gs], vTp_ref[gs, :]
            else:
                kprev, vprevT = kc_ref[c0 - QB:c0, gs], vTc_ref[gs, c0 - QB:c0]
            kcat = jnp.concatenate([kprev, kc_ref[c0:c0 + QB, gs], km_ref[:, gs]], axis=0)
            vcatT = jnp.concatenate([vprevT, vTc_ref[gs, c0:c0 + QB], vTm_ref[gs, 0:N_META]], axis=1)
            vext = jnp.concatenate([vcatT, ones], axis=0)
            for c in range(GQA // HC):
                h0 = g * GQA + c * HC
                heads = [qT_ref[(h0 + j) * HEAD_DIM:(h0 + j + 1) * HEAD_DIM, c0:c0 + QB] for j in range(HC)]
                s = jnp.dot(kcat, jnp.concatenate(heads, axis=1), preferred_element_type=f32) + bias
                sink = sink_ref[h0 // HC:h0 // HC + 1, :]
                m = jnp.maximum(_col_max(s), sink)
                p = jnp.exp(s - m).astype(bf16)
                oe = jnp.dot(vext, p, preferred_element_type=f32)
                den = oe[HEAD_DIM:HEAD_DIM + 1, :] + jnp.exp(sink - m)
                o = (oe[0:HEAD_DIM, :] * (1.0 / den)).astype(bf16)
                for j in range(HC):
                    oT_ref[(h0 + j) * HEAD_DIM:(h0 + j + 1) * HEAD_DIM, c0:c0 + QB] = o[:, j * QB:(j + 1) * QB]


def _attention(qT, k, vT, sink_lanes):
    prev_blk = lambda i: jnp.maximum(i * (TM // QB) - 1, 0)
    return pl.pallas_call(
        _attn_kernel,
        grid=(NT,),
        in_specs=[
            pl.BlockSpec((D, TM), lambda i: (0, i)),
            pl.BlockSpec((TM, KVW), lambda i: (i, 0)),
            pl.BlockSpec((QB, KVW), lambda i: (prev_blk(i), 0)),
            pl.BlockSpec((N_META, KVW), lambda i: (META_ROW // N_META, 0)),
            pl.BlockSpec((KVW, TM), lambda i: (0, i)),
            pl.BlockSpec((KVW, QB), lambda i: (0, prev_blk(i))),
            pl.BlockSpec((KVW, LANES), lambda i: (0, META_ROW // LANES)),
            _resident((N_HEADS // HC, LW)),
        ],
        out_specs=pl.BlockSpec((D, TM), lambda i: (0, i)),
        out_shape=jax.ShapeDtypeStruct((D, TP), bf16),
        compiler_params=_cparams(),
        name="swa_attention",
    )(qT, k, k, k, vT, vT, vT, sink_lanes)


def _router_weights(wg, bg, we, be):
    wr = jnp.zeros((D, LANES), f32).at[:, 0:N_GROUPS].set(wg).at[:, R_EXP0:R_EXP0 + N_EXP].set(we)
    br = jnp.zeros((1, LANES), f32).at[0, 0:N_GROUPS].set(bg).at[0, R_EXP0:R_EXP0 + N_EXP].set(be)
    return wr.astype(bf16), br


def _rope_tables():
    pos = np.concatenate([np.arange(SEQ) + N_META, np.arange(TM)]).astype(np.float32)
    half = ROT // 2
    inv_freq = (np.float32(ROPE_THETA) ** (-np.arange(0, ROT, 2, dtype=np.float32) / np.float32(ROT)))
    ang = pos[:, None] * inv_freq.astype(np.float32)[None, :]
    cos, sin = np.cos(ang).astype(np.float32), np.sin(ang).astype(np.float32)
    ones = np.ones((TP, HEAD_DIM - ROT), np.float32)
    zeros = np.zeros((TP, HEAD_DIM - ROT), np.float32)
    z8 = np.zeros((TP, half), np.float32)
    cos_h = np.concatenate([cos, cos, ones], axis=1)
    sa_h = np.concatenate([-sin, z8, zeros], axis=1)
    sb_h = np.concatenate([z8, sin, zeros], axis=1)
    rep = LANES // HEAD_DIM
    tabs = (cos.T, sin.T, np.tile(cos_h, (1, rep)), np.tile(sa_h, (1, rep)), np.tile(sb_h, (1, rep)))
    return tuple(jnp.asarray(np.ascontiguousarray(t)) for t in tabs)


def kernel(x, meta_tokens, conv_w_in, conv_b_in, conv_w_dw, conv_b_dw, conv_ln_g, conv_ln_b, conv_w_out,
           conv_b_out, w_k, b_k, w_v, b_v, w_q, b_q, w_o, b_o, sinks, ln_mix_g, ln_mix_b, ln_ffn_g, ln_ffn_b,
           router_group_w, router_group_b, router_expert_w, router_expert_b, expert_w1, expert_w3, expert_w2):
    assert x.shape == (1, SEQ, D)
    row = lambda v: v.reshape(1, -1)
    col = lambda v: v.reshape(-1, 1)
    x2d = x.reshape(SEQ, D)
    meta_pad = jnp.pad(meta_tokens.astype(f32), ((0, TM - N_META), (0, 0)))

    y = _glu(x2d, meta_pad, conv_w_in[0], row(conv_b_in[0]))
    w_dw = conv_w_dw[0].reshape(CONV_W, 2, PK, LANES).transpose(0, 2, 1, 3).reshape(CONV_W, CHUNKS, LANES)
    z = _conv(y, w_dw.astype(bf16), row(conv_b_dw[0]), row(conv_ln_g[0]),
              row(conv_ln_b[0]))
    wr, br = _router_weights(router_group_w[0], router_group_b[0], router_expert_w[0], router_expert_b[0])
    h, hp, eidx, rank, gcol, cnt = _proj_ln_route(
        z, (x2d, meta_pad), conv_w_out[0], row(conv_b_out[0]), row(ln_mix_g[0]), row(ln_mix_b[0]),
        wr, br)
    h = _moe(0, h, hp, eidx, rank, gcol, cnt, expert_w1, expert_w3, expert_w2,
             row(ln_ffn_g[0]), row(ln_ffn_b[0]), NT)

    qT, k, vT = _qkv(h, w_q[0], col(b_q[0]), w_k.astype(bf16), row(b_k),
                     w_v.T.astype(bf16), col(b_v), _rope_tables())
    sink_lanes = jnp.repeat(sinks[0].astype(f32).reshape(N_HEADS // HC, HC), QB, axis=1)
    attT = _attention(qT, k, vT, sink_lanes)
    wr, br = _router_weights(router_group_w[1], router_group_b[1], router_expert_w[1], router_expert_b[1])
    h, hp, eidx, rank, gcol, cnt = _proj_ln_route(
        attT, h, w_o[0], row(b_o[0]), row(ln_mix_g[1]), row(ln_mix_b[1]), wr, br)
    out = _moe(1, h, hp, eidx, rank, gcol, cnt, expert_w1, expert_w3, expert_w2,
               row(ln_ffn_g[1]), row(ln_ffn_b[1]), NXT)
    return out.reshape(1, SEQ, D)
```

```python
import functools
import math

import jax
import jax.numpy as jnp
import numpy as np
from jax import lax
from jax.experimental import pallas as pl
from jax.experimental.pallas import tpu as pltpu

f32 = jnp.float32
bf16 = jnp.bfloat16
i32 = jnp.int32
u32 = jnp.uint32

D = 2048
SEQ = 8192
DEPTH = 2
N_META = 16
CONV_W = 31
HEAD_DIM = 64
N_HEADS = 32
N_KV = 4
GQA = 8
KVW = N_KV * HEAD_DIM
WINDOW = 128
ROT = 16
ROPE_THETA = 500000.0
N_GROUPS = 4
EPG = 8
N_EXP = 32
FF = 256
ALPHA = (2.0 * DEPTH) ** 0.25
LN_EPS = 1e-5

LANES = 128
TM = 256
NXT = SEQ // TM
NT = NXT + 1
TP = NT * TM
META_ROW = SEQ
CHUNKS = D // LANES
HALO = 32
TME = 256
NTE = (2 * TP) // TME + N_EXP
NS = NTE * TME
QB = 128
VMEM_LIMIT = 52 * 1024 * 1024


def _cparams():
    return pltpu.CompilerParams(dimension_semantics=("arbitrary",), vmem_limit_bytes=VMEM_LIMIT)


def _resident(shape):
    nd = len(shape)
    return pl.BlockSpec(shape, lambda *a: (0,) * nd, pipeline_mode=pl.Buffered(1))


def _layer_norm(x, g, b):
    mu = jnp.mean(x, axis=-1, keepdims=True)
    xc = x - mu
    var = jnp.mean(xc * xc, axis=-1, keepdims=True)
    return xc * lax.rsqrt(var + LN_EPS) * g + b


def _x_or_meta(i, x_ref, meta_ref):
    return jnp.where(i == NXT, meta_ref[...], x_ref[...])


HALF = D // 2
PK = HALF // LANES


def _pack_rows(v, out2d):
    rows = v.shape[0]
    bits = pltpu.bitcast(v.astype(bf16).astype(f32), u32)
    word = bits[:, HALF:] | lax.shift_right_logical(bits[:, :HALF], jnp.uint32(16))
    for s in range(PK):
        out2d[pl.ds(s, rows, stride=PK), :] = word[:, s * LANES:(s + 1) * LANES]


def _unpack_rows(in2d, rows, dtype):
    lo, hi = [], []
    for s in range(PK):
        w = in2d[pl.ds(s, rows, stride=PK), :]
        lo.append(pltpu.bitcast(lax.shift_left(w, jnp.uint32(16)), f32).astype(dtype))
        hi.append(pltpu.bitcast(w & jnp.uint32(0xFFFF0000), f32).astype(dtype))
    return jnp.concatenate(lo, axis=1), jnp.concatenate(hi, axis=1)


def _load_weights_bf16(w_hbm, w_vmem, stage, sem):
    rc = stage.shape[1]
    n = w_hbm.shape[0] // rc

    def cp(c):
        return pltpu.make_async_copy(w_hbm.at[pl.ds(c * rc, rc)], stage.at[c % 2], sem.at[c % 2])

    cp(0).start()
    for c in range(n):
        cp(c).wait()
        if c + 1 < n:
            cp(c + 1).start()
        w_vmem[c * rc:(c + 1) * rc, :] = stage[c % 2].astype(bf16)


def _weight_scratch(rows, cols, chunk_rows):
    return [pltpu.VMEM((rows, cols), bf16), pltpu.VMEM((2, chunk_rows, cols), f32), pltpu.SemaphoreType.DMA((2,))]


HBM = pl.BlockSpec(memory_space=pl.ANY)


def _glu_kernel(x_ref, meta_ref, w_hbm, b_ref, yp_ref, ybuf, w_ref, stage, sem):
    i = pl.program_id(0)

    @pl.when(i == 0)
    def _():
        _load_weights_bf16(w_hbm, w_ref, stage, sem)

    xb = _x_or_meta(i, x_ref, meta_ref).astype(bf16)
    cw = 512
    for c in range(D // cw):
        lo, hi = c * cw, (c + 1) * cw
        a = jnp.dot(xb, w_ref[:, lo:hi], preferred_element_type=f32) + b_ref[:, lo:hi]
        g = jnp.dot(xb, w_ref[:, D + lo:D + hi], preferred_element_type=f32) + b_ref[:, D + lo:D + hi]
        ybuf[:, lo:hi] = a * jax.nn.sigmoid(g)
    _pack_rows(ybuf[...], yp_ref)


def _glu(x2d, meta_pad, w_in, b_in):
    return pl.pallas_call(
        _glu_kernel,
        grid=(NT,),
        in_specs=[
            pl.BlockSpec((TM, D), lambda i: (jnp.minimum(i, NXT - 1), 0)),
            _resident((TM, D)),
            HBM,
            _resident((1, 2 * D)),
        ],
        out_specs=pl.BlockSpec((TM * PK, LANES), lambda i: (i, 0)),
        out_shape=jax.ShapeDtypeStruct((TP * PK, LANES), u32),
        scratch_shapes=[pltpu.VMEM((TM, D), f32)] + _weight_scratch(D, 2 * D, 256),
        compiler_params=_cparams(),
        name="glu",
    )(x2d, meta_pad, w_in, b_in)


def _chunk_row(c):
    return 2 * (c % PK) + c // PK


def _conv_kernel(yp_ref, ymeta_ref, w_ref, bdw_ref, g_ref, b_ref, z_ref, scr, accs):
    i = pl.program_id(0)
    hrows = HALO * PK
    trows = TM * PK

    @pl.when(i == 0)
    def _():
        scr[0:(HALO - N_META) * PK, :] = jnp.zeros(((HALO - N_META) * PK, LANES), u32)
        scr[(HALO - N_META) * PK:hrows, :] = ymeta_ref[...]

    @pl.when(i == NT - 1)
    def _():
        scr[0:hrows, :] = jnp.zeros((hrows, LANES), u32)

    @pl.when(jnp.logical_and(i > 0, i < NT - 1))
    def _():
        scr[0:hrows, :] = scr[trows:trows + hrows, :]

    scr[hrows:hrows + trows, :] = yp_ref[...]

    tb = 8
    first = (HALO - (CONV_W - 1)) * PK

    def block(t, carry):
        base = pl.multiple_of(t * (tb * PK), tb * PK)
        acc = jnp.zeros((tb, CHUNKS, LANES), f32)
        for j in range(CONV_W):
            words = scr[pl.ds(base + first + j * PK, tb * PK), :]
            sl = pltpu.bitcast(words, bf16).reshape(tb, CHUNKS, LANES)
            acc = acc + sl.astype(f32) * w_ref[j].astype(f32)[None]
        accs[pl.ds(pl.multiple_of(t * (tb * CHUNKS), tb * CHUNKS), tb * CHUNKS), :] = acc.reshape(tb * CHUNKS, LANES)
        return carry

    lax.fori_loop(0, TM // tb, block, 0)

    rb = 16

    def finish(t, carry):
        r0 = pl.multiple_of(t * rb, rb)
        cols = [accs[pl.ds(r0 * CHUNKS + _chunk_row(c), rb, stride=CHUNKS), :] for c in range(CHUNKS)]
        v = _layer_norm(jnp.concatenate(cols, axis=1) + bdw_ref[...], g_ref[...], b_ref[...])
        z_ref[pl.ds(r0, rb), :] = (v * jax.nn.sigmoid(v)).astype(bf16)
        return carry

    lax.fori_loop(0, TM // rb, finish, 0, unroll=8)


def _conv(yp, w_dw3, b_dw, ln_g, ln_b):
    return pl.pallas_call(
        _conv_kernel,
        grid=(NT,),
        in_specs=[
            pl.BlockSpec((TM * PK, LANES), lambda i: (i, 0)),
            pl.BlockSpec((N_META * PK, LANES), lambda i: (META_ROW // N_META, 0)),
            _resident((CONV_W, CHUNKS, LANES)),
            _resident((1, D)),
            _resident((1, D)),
            _resident((1, D)),
        ],
        out_specs=pl.BlockSpec((TM, D), lambda i: (i, 0)),
        out_shape=jax.ShapeDtypeStruct((TP, D), bf16),
        scratch_shapes=[
            pltpu.VMEM(((TM + HALO) * PK, LANES), u32),
            pltpu.VMEM((TM * CHUNKS, LANES), f32),
        ],
        compiler_params=_cparams(),
        name="conv_ln_swish",
    )(yp, yp, w_dw3, b_dw, ln_g, ln_b)


R_EXP0 = 8


def _route(h, wr_ref, br_ref, running, valid):
    logits = jnp.dot(h.astype(bf16), wr_ref[...], preferred_element_type=f32) + br_ref[...]
    lt = logits.T
    gl = [lt[k:k + 1, :] for k in range(N_GROUPS)]
    gm = functools.reduce(jnp.maximum, gl)
    gex = [jnp.exp(v - gm) for v in gl]
    gden = functools.reduce(lambda a, b: a + b, gex)
    gp = [v / gden for v in gex]
    best = gp[0]
    gi = jnp.zeros((1, TM), i32)
    for k in range(1, N_GROUPS):
        better = gp[k] > best
        gi = jnp.where(better, k, gi)
        best = jnp.where(better, gp[k], best)
    esel = lt[R_EXP0:R_EXP0 + EPG, :]
    for k in range(1, N_GROUPS):
        esel = jnp.where(gi == k, lt[R_EXP0 + EPG * k:R_EXP0 + EPG * (k + 1), :], esel)
    em = jnp.max(esel, axis=0, keepdims=True)
    eex = jnp.exp(esel - em)
    ep = eex / jnp.sum(eex, axis=0, keepdims=True)
    io8 = lax.broadcasted_iota(i32, (EPG, TM), 0)
    v1 = jnp.max(ep, axis=0, keepdims=True)
    i1 = jnp.min(jnp.where(ep == v1, io8, EPG), axis=0, keepdims=True)
    ep2 = jnp.where(io8 == i1, -1.0, ep)
    v2 = jnp.max(ep2, axis=0, keepdims=True)
    i2 = jnp.min(jnp.where(ep2 == v2, io8, EPG), axis=0, keepdims=True)
    s = v1 + v2
    gate0 = best * (v1 / s)
    gate1 = best * (v2 / s)
    f0 = gi * EPG + i1
    f1 = gi * EPG + i2

    io32 = lax.broadcasted_iota(i32, (N_EXP, TM), 0)
    oh0 = (io32 == f0).astype(f32)
    oh1 = (io32 == f1).astype(f32)
    cnt = oh0 + oh1
    upper = (lax.broadcasted_iota(i32, (TM, TM), 0) < lax.broadcasted_iota(i32, (TM, TM), 1))
    before = jnp.dot(cnt.astype(bf16), upper.astype(f32).astype(bf16), preferred_element_type=f32)
    base = running[...] + before
    r0 = jnp.sum(oh0 * base, axis=0, keepdims=True).astype(i32)
    r1 = jnp.sum(oh1 * base, axis=0, keepdims=True).astype(i32)
    running[...] = running[...] + valid * jnp.sum(cnt, axis=1, keepdims=True)

    io128 = lax.broadcasted_iota(i32, (LANES, TM), 0)
    gcol = jnp.where(io128 == 0, gate0, jnp.where(io128 == 1, gate1, 0.0)).T
    return f0, f1, r0, r1, gcol


def _proj_ln_route_epilogue(valid, a, res, bias_ref, g_ref, b_ref, wr_ref, br_ref,
                            h_ref, hp_ref, eidx_ref, rank_ref, gcol_ref, cnt_ref, running):
    mix = a + bias_ref[...]
    h = _layer_norm(ALPHA * res + mix, g_ref[...], b_ref[...])
    h_ref[...] = h
    _pack_rows(h, hp_ref)
    f0, f1, r0, r1, gcol = _route(h, wr_ref, br_ref, running, valid)
    eidx_ref[0, 0:1, :] = f0
    eidx_ref[0, 1:2, :] = f1
    rank_ref[0, 0:1, :] = r0
    rank_ref[0, 1:2, :] = r1
    gcol_ref[...] = gcol
    cnt_ref[...] = running[...]


def _proj_ln_route_kernel(first, a_ref, *refs):
    if first:
        x_ref, meta_ref, w_hbm, *rest = refs
    else:
        res_ref, w_hbm, *rest = refs
    *rest, running, accbuf, w_ref, stage, sem = rest
    i = pl.program_id(0)

    @pl.when(i == 0)
    def _():
        running[...] = jnp.zeros_like(running)
        accbuf[1] = jnp.zeros((TM, D), f32)
        _load_weights_bf16(w_hbm, w_ref, stage, sem)

    valid = (i > 0).astype(f32)
    for parity in range(2):
        @pl.when(lax.rem(i, 2) == parity)
        def _():
            prev = accbuf[1 - parity]
            if first:
                accbuf[parity] = jnp.dot(a_ref[...], w_ref[...], preferred_element_type=f32)
                res = jnp.where(i - 1 == NXT, meta_ref[...], x_ref[...])
            else:
                accbuf[parity] = lax.dot_general(a_ref[...], w_ref[...], (((0,), (0,)), ((), ())),
                                                 preferred_element_type=f32)
                res = res_ref[...]
            _proj_ln_route_epilogue(valid, prev, res, *rest, running)


def _proj_ln_route(a, res, w, bias, ln_g, ln_b, wr, br):
    first = isinstance(res, tuple)
    cur = lambda i: jnp.minimum(i, NT - 1)
    prv = lambda i: jnp.maximum(i - 1, 0)
    if first:
        a_spec = pl.BlockSpec((TM, D), lambda i: (cur(i), 0))
        res_specs = [pl.BlockSpec((TM, D), lambda i: (jnp.minimum(prv(i), NXT - 1), 0)), _resident((TM, D))]
        res_args = list(res)
    else:
        a_spec = pl.BlockSpec((D, TM), lambda i: (0, cur(i)))
        res_specs = [pl.BlockSpec((TM, D), lambda i: (prv(i), 0))]
        res_args = [res]
    tile3 = pl.BlockSpec((1, 2, TM), lambda i: (prv(i), 0, 0))
    return pl.pallas_call(
        functools.partial(_proj_ln_route_kernel, first),
        grid=(NT + 1,),
        in_specs=[a_spec] + res_specs + [
            HBM, _resident((1, D)), _resident((1, D)), _resident((1, D)),
            _resident((D, LANES)), _resident((1, LANES)),
        ],
        out_specs=[
            pl.BlockSpec((TM, D), lambda i: (prv(i), 0)),
            pl.BlockSpec((TM * PK, LANES), lambda i: (prv(i), 0)),
            tile3, tile3,
            pl.BlockSpec((TM, LANES), lambda i: (prv(i), 0)),
            pl.BlockSpec((N_EXP, TM), lambda i: (0, 0)),
        ],
        out_shape=[
            jax.ShapeDtypeStruct((TP, D), f32),
            jax.ShapeDtypeStruct((TP * PK, LANES), u32),
            jax.ShapeDtypeStruct((NT, 2, TM), i32),
            jax.ShapeDtypeStruct((NT, 2, TM), i32),
            jax.ShapeDtypeStruct((TP, LANES), f32),
            jax.ShapeDtypeStruct((N_EXP, TM), f32),
        ],
        scratch_shapes=[pltpu.VMEM((N_EXP, TM), f32), pltpu.VMEM((2, TM, D), f32)] + _weight_scratch(D, D, 512),
        compiler_params=_cparams(),
        name="proj_ln_route",
    )(a, *res_args, w, bias, ln_g, ln_b, wr, br)


def _plan(eidx, rank, cnt):
    counts = cnt[:, 0].astype(i32)
    padded = ((counts + TME - 1) // TME) * TME
    ends = jnp.cumsum(padded)
    offs = ends - padded
    ntiles = ends[-1] // TME
    off_of = jnp.sum(jnp.where(eidx[..., None] == jnp.arange(N_EXP, dtype=i32), offs, 0), axis=-1)
    dest = (off_of + rank).reshape(-1)
    tile_start = jnp.minimum(jnp.arange(NTE, dtype=i32), ntiles - 1) * TME
    tile_expert = jnp.minimum(jnp.sum(tile_start[:, None] >= ends[None, :], axis=1), N_EXP - 1).astype(i32)
    zstart = jnp.where(padded > 0, ends - TME, 0).astype(i32)
    zflag = (padded > 0).astype(i32)
    tid = jnp.arange(NTE, dtype=i32)
    live = tid < ntiles
    first = jnp.logical_and(live, jnp.logical_or(tid == 0, tile_expert != jnp.roll(tile_expert, 1)))
    slot = (jnp.cumsum(first.astype(i32)) - 1) % 2
    nxt_first = lax.cummin(jnp.where(first, tid, NTE), reverse=True)
    after = jnp.concatenate([nxt_first[1:], jnp.full((1,), NTE, i32)])
    nxt = jnp.where(after < NTE, tile_expert[jnp.minimum(after, NTE - 1)], -1)
    return (dest.astype(i32), tile_expert, ntiles.reshape(1).astype(i32), zstart, zflag,
            first.astype(i32), slot.astype(i32), nxt.astype(i32))


ISSUE_UNROLL = 8


def _scatter_kernel(dest_ref, zstart_ref, zflag_ref, hp_ref, xs_hbm, zeros, sem, zsem):
    i = pl.program_id(0)

    @pl.when(i == 0)
    def _():
        zeros[...] = jnp.zeros_like(zeros)
        def fill(e):
            start = pl.multiple_of(zstart_ref[e], TME)
            return pltpu.make_async_copy(zeros, xs_hbm.at[pl.ds(start, TME)], zsem)

        for e in range(N_EXP):
            @pl.when(zflag_ref[e] > 0)
            def _():
                fill(e).start()
        for e in range(N_EXP):
            @pl.when(zflag_ref[e] > 0)
            def _():
                fill(e).wait()

    base = i * (2 * TM)

    def row(r, carry):
        for k in range(2):
            d = dest_ref[base + k * TM + r]
            pltpu.make_async_copy(hp_ref.at[pl.ds(r, 1)], xs_hbm.at[pl.ds(d, 1)], sem).start(priority=k)
        return carry

    lax.fori_loop(0, TM, row, 0, unroll=ISSUE_UNROLL)

    def drain(r, carry):
        pltpu.make_async_copy(hp_ref.at[pl.ds(0, 1)], xs_hbm.at[pl.ds(0, 1)], sem).wait()
        return carry

    lax.fori_loop(0, 2 * TM, drain, 0, unroll=ISSUE_UNROLL)


def _scatter(dest, zstart, zflag, hp3):
    return pl.pallas_call(
        _scatter_kernel,
        grid_spec=pltpu.PrefetchScalarGridSpec(
            num_scalar_prefetch=3,
            grid=(NT,),
            in_specs=[pl.BlockSpec((TM, PK, LANES), lambda i, *_: (i, 0, 0))],
            out_specs=pl.BlockSpec(memory_space=pl.ANY),
            scratch_shapes=[pltpu.VMEM((TME, PK, LANES), u32), pltpu.SemaphoreType.DMA(()),
                            pltpu.SemaphoreType.DMA(())],
        ),
        out_shape=jax.ShapeDtypeStruct((NS, PK, LANES), u32),
        compiler_params=pltpu.CompilerParams(dimension_semantics=("arbitrary",), vmem_limit_bytes=VMEM_LIMIT,
                                             has_side_effects=True),
        name="moe_scatter",
    )(dest, zstart, zflag, hp3)


def _expert_kernel(layer, te_ref, nt_ref, first_ref, slot_ref, nxt_ref, xs_ref, w1_hbm, w3_hbm, w2_hbm, ys_ref,
                   wb1, wb3, wb2, w1c, w3c, w2c, sem):
    i = pl.program_id(0)

    def copies(e, s):
        return (pltpu.make_async_copy(w1_hbm.at[layer, e], wb1.at[s], sem.at[s]),
                pltpu.make_async_copy(w3_hbm.at[layer, e], wb3.at[s], sem.at[s]),
                pltpu.make_async_copy(w2_hbm.at[layer, e], wb2.at[s], sem.at[s]))

    @pl.when(i == 0)
    def _():
        for cp in copies(te_ref[0], 0):
            cp.start()

    @pl.when(jnp.logical_and(i < nt_ref[0], first_ref[i] > 0))
    def _():
        s = slot_ref[i]
        for cp in copies(te_ref[i], s):
            cp.wait()

        @pl.when(nxt_ref[i] >= 0)
        def _():
            for cp in copies(nxt_ref[i], 1 - s):
                cp.start()

        w1c[...] = wb1[s].astype(bf16)
        w3c[...] = wb3[s].astype(bf16)
        w2c[...] = wb2[s].astype(bf16)

    @pl.when(i < nt_ref[0])
    def _():
        xlo, xhi = _unpack_rows(xs_ref, TME, bf16)
        a = (jnp.dot(xlo, w1c[0:HALF, :], preferred_element_type=f32)
             + jnp.dot(xhi, w1c[HALF:D, :], preferred_element_type=f32))
        b = (jnp.dot(xlo, w3c[0:HALF, :], preferred_element_type=f32)
             + jnp.dot(xhi, w3c[HALF:D, :], preferred_element_type=f32))
        hid = (a * jax.nn.sigmoid(a) * b).astype(bf16)
        _pack_rows(jnp.dot(hid, w2c[...], preferred_element_type=f32), ys_ref)


def _experts(layer, tile_expert, ntiles, first, slot, nxt, xs2d, w1, w3, w2):
    def row_map(i, te, nt, *_):
        return (jnp.minimum(i, nt[0] - 1), 0)

    hbm = pl.BlockSpec(memory_space=pl.ANY)
    return pl.pallas_call(
        functools.partial(_expert_kernel, layer),
        grid_spec=pltpu.PrefetchScalarGridSpec(
            num_scalar_prefetch=5,
            grid=(NTE,),
            in_specs=[pl.BlockSpec((TME * PK, LANES), row_map), hbm, hbm, hbm],
            out_specs=pl.BlockSpec((TME * PK, LANES), row_map),
            scratch_shapes=[
                pltpu.VMEM((2, D, FF), f32), pltpu.VMEM((2, D, FF), f32), pltpu.VMEM((2, FF, D), f32),
                pltpu.VMEM((D, FF), bf16), pltpu.VMEM((D, FF), bf16), pltpu.VMEM((FF, D), bf16),
                pltpu.SemaphoreType.DMA((2,)),
            ],
        ),
        out_shape=jax.ShapeDtypeStruct((NS * PK, LANES), u32),
        compiler_params=_cparams(),
        name="moe_experts",
    )(tile_expert, ntiles, first, slot, nxt, xs2d, w1, w3, w2)


def _gather_combine(n, dest_ref, ys_hbm, h_ref, gcol_ref, g_ref, b_ref, buf, sem):
    i = pl.program_id(0)
    slot = lax.rem(i, 2)

    def issue(tile, s):
        base = tile * (2 * TM)

        def row(r, carry):
            for k in range(2):
                d = pl.multiple_of(dest_ref[base + k * TM + r] * PK, PK)
                pltpu.make_async_copy(ys_hbm.at[pl.ds(d, PK)],
                                      buf.at[s, k, pl.ds(pl.multiple_of(r * PK, PK), PK)],
                                      sem.at[s]).start(priority=k)
            return carry

        lax.fori_loop(0, TM, row, 0, unroll=ISSUE_UNROLL)

    @pl.when(i == 0)
    def _():
        issue(0, 0)

    @pl.when(i + 1 < n)
    def _():
        issue(i + 1, 1 - slot)

    def drain(r, carry):
        pltpu.make_async_copy(ys_hbm.at[pl.ds(0, PK)], buf.at[slot, 0, pl.ds(0, PK)], sem.at[slot]).wait()
        return carry

    lax.fori_loop(0, 2 * TM, drain, 0, unroll=ISSUE_UNROLL)

    lo0, hi0 = _unpack_rows(buf.at[slot, 0], TM, f32)
    lo1, hi1 = _unpack_rows(buf.at[slot, 1], TM, f32)
    g0, g1 = gcol_ref[:, 0:1], gcol_ref[:, 1:2]
    ffn = jnp.concatenate([lo0 * g0 + lo1 * g1, hi0 * g0 + hi1 * g1], axis=1)
    return _layer_norm(ALPHA * h_ref[...] + ffn, g_ref[...], b_ref[...])


def _combine_kernel(n, dest_ref, ys_hbm, h_ref, gcol_ref, g_ref, b_ref, o_ref, buf, sem):
    o_ref[...] = _gather_combine(n, dest_ref, ys_hbm, h_ref, gcol_ref, g_ref, b_ref, buf, sem)


def _combine(dest, ys2d, h, gcol, ln_g, ln_b, ntiles_out):
    return pl.pallas_call(
        functools.partial(_combine_kernel, ntiles_out),
        grid_spec=pltpu.PrefetchScalarGridSpec(
            num_scalar_prefetch=1,
            grid=(ntiles_out,),
            in_specs=[
                pl.BlockSpec(memory_space=pl.ANY),
                pl.BlockSpec((TM, D), lambda i, *_: (i, 0)),
                pl.BlockSpec((TM, LANES), lambda i, *_: (i, 0)),
                pl.BlockSpec((1, D), lambda i, *_: (0, 0)),
                pl.BlockSpec((1, D), lambda i, *_: (0, 0)),
            ],
            out_specs=pl.BlockSpec((TM, D), lambda i, *_: (i, 0)),
            scratch_shapes=[pltpu.VMEM((2, 2, TM * PK, LANES), u32), pltpu.SemaphoreType.DMA((2,))],
        ),
        out_shape=jax.ShapeDtypeStruct((ntiles_out * TM, D), f32),
        compiler_params=_cparams(),
        name="moe_combine_ln",
    )(dest, ys2d, h, gcol, ln_g, ln_b)


def _moe_experts(layer, hp2d, eidx, rank, cnt, w1, w3, w2):
    dest, tile_expert, ntiles, zstart, zflag, first, slot, nxt = _plan(eidx, rank, cnt)
    xs = _scatter(dest, zstart, zflag, hp2d.reshape(TP, PK, LANES))
    ys2d = _experts(layer, tile_expert, ntiles, first, slot, nxt, xs.reshape(NS * PK, LANES), w1, w3, w2)
    return dest, ys2d


NT_DIMS = (((1,), (1,)), ((), ()))


def _rope_rows(t, cos, sa, sb):
    w = t.shape[1]
    reps = w // LANES
    c = jnp.tile(cos, (1, reps))
    a = jnp.tile(sa, (1, reps))
    b = jnp.tile(sb, (1, reps))
    return t * c + pltpu.roll(t, w - ROT // 2, 1) * a + pltpu.roll(t, ROT // 2, 1) * b


def _combine_qkv_kernel(n, dest_ref, ys_hbm, h_ref, gcol_ref, g_ref, b_ref,
                        wq_hbm, bq_ref, wk_ref, bk_ref, wvT_ref, bv_ref, cosT_ref, sinT_ref, cos_ref, sa_ref, sb_ref,
                        h2_ref, qT_ref, k_ref, vT_ref, buf, gsem, wq_ref, stage, wsem):
    @pl.when(pl.program_id(0) == 0)
    def _():
        _load_weights_bf16(wq_hbm, wq_ref, stage, wsem)

    h2 = _gather_combine(n, dest_ref, ys_hbm, h_ref, gcol_ref, g_ref, b_ref, buf, gsem)
    h2_ref[...] = h2
    hb = h2.astype(bf16)
    scale = 1.0 / math.sqrt(HEAD_DIM)
    half = ROT // 2
    cosT = cosT_ref[...][None]
    sinT = sinT_ref[...][None]
    rows = GQA * HEAD_DIM
    for c in range(D // rows):
        lo, hi = c * rows, (c + 1) * rows
        t = lax.dot_general(wq_ref[:, lo:hi], hb, (((0,), (1,)), ((), ())), preferred_element_type=f32) \
            + bq_ref[lo:hi, :]
        t3 = t.reshape(GQA, HEAD_DIM, TM)
        x1, x2 = t3[:, 0:half, :], t3[:, half:ROT, :]
        r = jnp.concatenate([x1 * cosT - x2 * sinT, x2 * cosT + x1 * sinT, t3[:, ROT:, :]], axis=1)
        qT_ref[lo:hi, :] = (r * scale).reshape(rows, TM).astype(bf16)
    t = jnp.dot(hb, wk_ref[...], preferred_element_type=f32) + bk_ref[...]
    k_ref[...] = _rope_rows(t, cos_ref[...], sa_ref[...], sb_ref[...]).astype(bf16)
    t = lax.dot_general(wvT_ref[...], hb, NT_DIMS, preferred_element_type=f32) + bv_ref[...]
    vT_ref[...] = t.astype(bf16)


def _combine_qkv(dest, ys2d, h, gcol, ln_g, ln_b, wq, bq_col, wk, bk, wvT, bv_col, tables):
    cosT, sinT, cos_t, sa_t, sb_t = tables
    const = lambda shape: pl.BlockSpec(shape, lambda i, *_: (0,) * len(shape))
    tabT = pl.BlockSpec((ROT // 2, TM), lambda i, *_: (0, i))
    tab = pl.BlockSpec((TM, LANES), lambda i, *_: (i, 0))
    return pl.pallas_call(
        functools.partial(_combine_qkv_kernel, NT),
        grid_spec=pltpu.PrefetchScalarGridSpec(
            num_scalar_prefetch=1,
            grid=(NT,),
            in_specs=[
                HBM,
                pl.BlockSpec((TM, D), lambda i, *_: (i, 0)),
                pl.BlockSpec((TM, LANES), lambda i, *_: (i, 0)),
                const((1, D)), const((1, D)),
                HBM, const((D, 1)),
                const((D, KVW)), const((1, KVW)),
                const((KVW, D)), const((KVW, 1)),
                tabT, tabT, tab, tab, tab,
            ],
            out_specs=[
                pl.BlockSpec((TM, D), lambda i, *_: (i, 0)),
                pl.BlockSpec((D, TM), lambda i, *_: (0, i)),
                pl.BlockSpec((TM, KVW), lambda i, *_: (i, 0)),
                pl.BlockSpec((KVW, TM), lambda i, *_: (0, i)),
            ],
            scratch_shapes=[pltpu.VMEM((2, 2, TM * PK, LANES), u32), pltpu.SemaphoreType.DMA((2,))]
            + _weight_scratch(D, D, 512),
        ),
        out_shape=[
            jax.ShapeDtypeStruct((TP, D), f32),
            jax.ShapeDtypeStruct((D, TP), bf16),
            jax.ShapeDtypeStruct((TP, KVW), bf16),
            jax.ShapeDtypeStruct((KVW, TP), bf16),
        ],
        compiler_params=_cparams(),
        name="moe_combine_ln_qkv_rope",
    )(dest, ys2d, h, gcol, ln_g, ln_b, wq, bq_col, wk, bk, wvT, bv_col, cosT, sinT, cos_t, sa_t, sb_t)


NKEY = 2 * QB + N_META
HC = 8
LW = HC * QB
SUB = 8


def _col_max(s):
    parts = [s[r * SUB:(r + 1) * SUB] for r in range(NKEY // SUB)]
    while len(parts) > 1:
        nxt = [jnp.maximum(parts[j], parts[j + 1]) for j in range(0, len(parts) - 1, 2)]
        if len(parts) % 2:
            nxt.append(parts[-1])
        parts = nxt
    return jnp.max(parts[0], axis=0, keepdims=True)


def _attn_kernel(qT_ref, kc_ref, kp_ref, km_ref, vTc_ref, vTp_ref, vTm_ref, sink_ref, oT_ref):
    i = pl.program_id(0)
    is_meta = i == NT - 1
    ck = lax.broadcasted_iota(i32, (NKEY, QB), 0)
    rq = lax.broadcasted_iota(i32, (NKEY, QB), 1)
    in_band = jnp.logical_and(ck > rq, ck <= rq + QB)
    meta_ok = jnp.logical_and(ck >= 2 * QB, jnp.logical_or(jnp.logical_not(is_meta), ck - 2 * QB <= rq))
    ones = jnp.ones((SUB, NKEY), bf16)

    for blk in range(TM // QB):
        lo = jnp.where(is_meta, 2 * QB, jnp.where(jnp.logical_and(i == 0, blk == 0), QB, 0))
        valid = jnp.logical_or(meta_ok, jnp.logical_and(in_band, ck >= lo))
        bias = jnp.where(valid, 0.0, -jnp.inf)
        bias = jnp.concatenate([bias] * HC, axis=1)
        c0 = blk * QB
        for g in range(N_KV):
            gs = slice(g * HEAD_DIM, (g + 1) * HEAD_DIM)
            if blk == 0:
                kprev, vprevT = kp_ref[:, gs], vTp_ref[gs, :]
            else:
                kprev, vprevT = kc_ref[c0 - QB:c0, gs], vTc_ref[gs, c0 - QB:c0]
            kcat = jnp.concatenate([kprev, kc_ref[c0:c0 + QB, gs], km_ref[:, gs]], axis=0)
            vcatT = jnp.concatenate([vprevT, vTc_ref[gs, c0:c0 + QB], vTm_ref[gs, 0:N_META]], axis=1)
            vext = jnp.concatenate([vcatT, ones], axis=0)
            for c in range(GQA // HC):
                h0 = g * GQA + c * HC
                heads = [qT_ref[(h0 + j) * HEAD_DIM:(h0 + j + 1) * HEAD_DIM, c0:c0 + QB] for j in range(HC)]
                s = jnp.dot(kcat, jnp.concatenate(heads, axis=1), preferred_element_type=f32) + bias
                sink = sink_ref[h0 // HC:h0 // HC + 1, :]
                m = jnp.maximum(_col_max(s), sink)
                p = jnp.exp(s - m).astype(bf16)
                oe = jnp.dot(vext, p, preferred_element_type=f32)
                den = oe[HEAD_DIM:HEAD_DIM + 1, :] + jnp.exp(sink - m)
                o = (oe[0:HEAD_DIM, :] * (1.0 / den)).astype(bf16)
                for j in range(HC):
                    oT_ref[(h0 + j) * HEAD_DIM:(h0 + j + 1) * HEAD_DIM, c0:c0 + QB] = o[:, j * QB:(j + 1) * QB]


def _attention(qT, k, vT, sink_lanes):
    prev_blk = lambda i: jnp.maximum(i * (TM // QB) - 1, 0)
    return pl.pallas_call(
        _attn_kernel,
        grid=(NT,),
        in_specs=[
            pl.BlockSpec((D, TM), lambda i: (0, i)),
            pl.BlockSpec((TM, KVW), lambda i: (i, 0)),
            pl.BlockSpec((QB, KVW), lambda i: (prev_blk(i), 0)),
            pl.BlockSpec((N_META, KVW), lambda i: (META_ROW // N_META, 0)),
            pl.BlockSpec((KVW, TM), lambda i: (0, i)),
            pl.BlockSpec((KVW, QB), lambda i: (0, prev_blk(i))),
            pl.BlockSpec((KVW, LANES), lambda i: (0, META_ROW // LANES)),
            _resident((N_HEADS // HC, LW)),
        ],
        out_specs=pl.BlockSpec((D, TM), lambda i: (0, i)),
        out_shape=jax.ShapeDtypeStruct((D, TP), bf16),
        compiler_params=_cparams(),
        name="swa_attention",
    )(qT, k, k, k, vT, vT, vT, sink_lanes)


def _router_weights(wg, bg, we, be):
    gap = R_EXP0 - N_GROUPS
    tail = LANES - R_EXP0 - N_EXP
    wr = jnp.concatenate([wg, jnp.zeros((D, gap), f32), we, jnp.zeros((D, tail), f32)], axis=1)
    br = jnp.concatenate([bg, jnp.zeros((gap,), f32), be, jnp.zeros((tail,), f32)]).reshape(1, LANES)
    return wr.astype(bf16), br


def _rope_tables():
    pos = np.concatenate([np.arange(SEQ) + N_META, np.arange(TM)]).astype(np.float32)
    half = ROT // 2
    inv_freq = (np.float32(ROPE_THETA) ** (-np.arange(0, ROT, 2, dtype=np.float32) / np.float32(ROT)))
    ang = pos[:, None] * inv_freq.astype(np.float32)[None, :]
    cos, sin = np.cos(ang).astype(np.float32), np.sin(ang).astype(np.float32)
    ones = np.ones((TP, HEAD_DIM - ROT), np.float32)
    zeros = np.zeros((TP, HEAD_DIM - ROT), np.float32)
    z8 = np.zeros((TP, half), np.float32)
    cos_h = np.concatenate([cos, cos, ones], axis=1)
    sa_h = np.concatenate([-sin, z8, zeros], axis=1)
    sb_h = np.concatenate([z8, sin, zeros], axis=1)
    rep = LANES // HEAD_DIM
    tabs = (cos.T, sin.T, np.tile(cos_h, (1, rep)), np.tile(sa_h, (1, rep)), np.tile(sb_h, (1, rep)))
    return tuple(jnp.asarray(np.ascontiguousarray(t)) for t in tabs)


def kernel(x, meta_tokens, conv_w_in, conv_b_in, conv_w_dw, conv_b_dw, conv_ln_g, conv_ln_b, conv_w_out,
           conv_b_out, w_k, b_k, w_v, b_v, w_q, b_q, w_o, b_o, sinks, ln_mix_g, ln_mix_b, ln_ffn_g, ln_ffn_b,
           router_group_w, router_group_b, router_expert_w, router_expert_b, expert_w1, expert_w3, expert_w2):
    assert x.shape == (1, SEQ, D)
    row = lambda v: v.reshape(1, -1)
    col = lambda v: v.reshape(-1, 1)
    x2d = x.reshape(SEQ, D)
    meta_pad = jnp.pad(meta_tokens.astype(f32), ((0, TM - N_META), (0, 0)))

    y = _glu(x2d, meta_pad, conv_w_in[0], row(conv_b_in[0]))
    w_dw = conv_w_dw[0].reshape(CONV_W, 2, PK, LANES).transpose(0, 2, 1, 3).reshape(CONV_W, CHUNKS, LANES)
    z = _conv(y, w_dw.astype(bf16), row(conv_b_dw[0]), row(conv_ln_g[0]),
              row(conv_ln_b[0]))
    wr, br = _router_weights(router_group_w[0], router_group_b[0], router_expert_w[0], router_expert_b[0])
    h, hp, eidx, rank, gcol, cnt = _proj_ln_route(
        z, (x2d, meta_pad), conv_w_out[0], row(conv_b_out[0]), row(ln_mix_g[0]), row(ln_mix_b[0]),
        wr, br)
    dest, ys2d = _moe_experts(0, hp, eidx, rank, cnt, expert_w1, expert_w3, expert_w2)

    h, qT, k, vT = _combine_qkv(dest, ys2d, h, gcol, row(ln_ffn_g[0]), row(ln_ffn_b[0]),
                                w_q[0], col(b_q[0]), w_k.astype(bf16), row(b_k),
                                w_v.T.astype(bf16), col(b_v), _rope_tables())
    sink_lanes = jnp.repeat(sinks[0].astype(f32).reshape(N_HEADS // HC, HC), QB, axis=1)
    attT = _attention(qT, k, vT, sink_lanes)
    wr, br = _router_weights(router_group_w[1], router_group_b[1], router_expert_w[1], router_expert_b[1])
    h, hp, eidx, rank, gcol, cnt = _proj_ln_route(
        attT, h, w_o[0], row(b_o[0]), row(ln_mix_g[1]), row(ln_mix_b[1]), wr, br)
    dest, ys2d = _moe_experts(1, hp, eidx, rank, cnt, expert_w1, expert_w3, expert_w2)
    out = _combine(dest, ys2d, h, gcol, row(ln_ffn_g[1]), row(ln_ffn_b[1]), NXT)
    return out.reshape(1, SEQ, D)
```

```python
import functools
import math

import jax
import jax.numpy as jnp
import numpy as np
from jax import lax
from jax.experimental import pallas as pl
from jax.experimental.pallas import tpu as pltpu

f32 = jnp.float32
bf16 = jnp.bfloat16
i32 = jnp.int32
u32 = jnp.uint32

D = 2048
SEQ = 8192
DEPTH = 2
N_META = 16
CONV_W = 31
HEAD_DIM = 64
N_HEADS = 32
N_KV = 4
GQA = 8
KVW = N_KV * HEAD_DIM
WINDOW = 128
ROT = 16
ROPE_THETA = 500000.0
N_GROUPS = 4
EPG = 8
N_EXP = 32
FF = 256
ALPHA = (2.0 * DEPTH) ** 0.25
LN_EPS = 1e-5
LOG2E = math.log2(math.e)

LANES = 128
TM = 256
NXT = SEQ // TM
NT = NXT + 1
TP = NT * TM
META_ROW = SEQ
CHUNKS = D // LANES
HALO = 32
TME = 256
ESUB = 2
EBLK = ESUB * TME
NTE = (2 * TP) // EBLK + N_EXP
NS = NTE * EBLK
QB = 128
VMEM_LIMIT = 52 * 1024 * 1024


def _cparams():
    return pltpu.CompilerParams(dimension_semantics=("arbitrary",), vmem_limit_bytes=VMEM_LIMIT)


def _resident(shape):
    nd = len(shape)
    return pl.BlockSpec(shape, lambda *a: (0,) * nd, pipeline_mode=pl.Buffered(1))


def _layer_norm(x, g, b):
    mu = jnp.mean(x, axis=-1, keepdims=True)
    xc = x - mu
    var = jnp.mean(xc * xc, axis=-1, keepdims=True)
    return xc * lax.rsqrt(var + LN_EPS) * g + b


def _x_or_meta(i, x_ref, meta_ref):
    return jnp.where(i == NXT, meta_ref[...], x_ref[...])


HALF = D // 2
PK = HALF // LANES


def _pack_rows(v, out2d):
    rows = v.shape[0]
    bits = pltpu.bitcast(v.astype(bf16).astype(f32), u32)
    word = bits[:, HALF:] | lax.shift_right_logical(bits[:, :HALF], jnp.uint32(16))
    for s in range(PK):
        out2d[pl.ds(s, rows, stride=PK), :] = word[:, s * LANES:(s + 1) * LANES]


def _unpack_rows(in2d, rows, dtype):
    lo, hi = [], []
    for s in range(PK):
        w = in2d[pl.ds(s, rows, stride=PK), :]
        lo.append(pltpu.bitcast(lax.shift_left(w, jnp.uint32(16)), f32).astype(dtype))
        hi.append(pltpu.bitcast(w & jnp.uint32(0xFFFF0000), f32).astype(dtype))
    return jnp.concatenate(lo, axis=1), jnp.concatenate(hi, axis=1)


def _load_weights_bf16(w_hbm, w_vmem, stage, sem):
    rc = stage.shape[1]
    n = w_hbm.shape[0] // rc

    def cp(c):
        return pltpu.make_async_copy(w_hbm.at[pl.ds(c * rc, rc)], stage.at[c % 2], sem.at[c % 2])

    cp(0).start()
    for c in range(n):
        cp(c).wait()
        if c + 1 < n:
            cp(c + 1).start()
        w_vmem[c * rc:(c + 1) * rc, :] = stage[c % 2].astype(bf16)


def _weight_scratch(rows, cols, chunk_rows):
    return [pltpu.VMEM((rows, cols), bf16), pltpu.VMEM((2, chunk_rows, cols), f32), pltpu.SemaphoreType.DMA((2,))]


HBM = pl.BlockSpec(memory_space=pl.ANY)


def _glu_kernel(x_ref, meta_ref, w_hbm, b_ref, yp_ref, ybuf, w_ref, stage, sem):
    i = pl.program_id(0)

    @pl.when(i == 0)
    def _():
        _load_weights_bf16(w_hbm, w_ref, stage, sem)

    xb = _x_or_meta(i, x_ref, meta_ref).astype(bf16)
    cw = 512
    for c in range(D // cw):
        lo, hi = c * cw, (c + 1) * cw
        a = jnp.dot(xb, w_ref[:, lo:hi], preferred_element_type=f32) + b_ref[:, lo:hi]
        g = jnp.dot(xb, w_ref[:, D + lo:D + hi], preferred_element_type=f32) + b_ref[:, D + lo:D + hi]
        ybuf[:, lo:hi] = a * jax.nn.sigmoid(g)
    _pack_rows(ybuf[...], yp_ref)


def _glu(x2d, meta_pad, w_in, b_in):
    return pl.pallas_call(
        _glu_kernel,
        grid=(NT,),
        in_specs=[
            pl.BlockSpec((TM, D), lambda i: (jnp.minimum(i, NXT - 1), 0)),
            _resident((TM, D)),
            HBM,
            _resident((1, 2 * D)),
        ],
        out_specs=pl.BlockSpec((TM * PK, LANES), lambda i: (i, 0)),
        out_shape=jax.ShapeDtypeStruct((TP * PK, LANES), u32),
        scratch_shapes=[pltpu.VMEM((TM, D), f32)] + _weight_scratch(D, 2 * D, 256),
        compiler_params=_cparams(),
        name="glu",
    )(x2d, meta_pad, w_in, b_in)


def _chunk_row(c):
    return 2 * (c % PK) + c // PK


def _conv_kernel(yp_ref, ymeta_ref, w_ref, bdw_ref, g_ref, b_ref, z_ref, scr, accs):
    i = pl.program_id(0)
    hrows = HALO * PK
    trows = TM * PK

    @pl.when(i == 0)
    def _():
        scr[0:(HALO - N_META) * PK, :] = jnp.zeros(((HALO - N_META) * PK, LANES), u32)
        scr[(HALO - N_META) * PK:hrows, :] = ymeta_ref[...]

    @pl.when(i == NT - 1)
    def _():
        scr[0:hrows, :] = jnp.zeros((hrows, LANES), u32)

    @pl.when(jnp.logical_and(i > 0, i < NT - 1))
    def _():
        scr[0:hrows, :] = scr[trows:trows + hrows, :]

    scr[hrows:hrows + trows, :] = yp_ref[...]

    tb = 16
    first = (HALO - (CONV_W - 1)) * PK

    def block(t, carry):
        base = pl.multiple_of(t * (tb * PK), tb * PK)
        acc = jnp.zeros((tb, CHUNKS, LANES), f32)
        for j in range(CONV_W):
            words = scr[pl.ds(base + first + j * PK, tb * PK), :]
            sl = pltpu.bitcast(words, bf16).reshape(tb, CHUNKS, LANES)
            acc = acc + sl.astype(f32) * w_ref[j].astype(f32)[None]
        accs[pl.ds(pl.multiple_of(t * (tb * CHUNKS), tb * CHUNKS), tb * CHUNKS), :] = acc.reshape(tb * CHUNKS, LANES)
        return carry

    lax.fori_loop(0, TM // tb, block, 0)

    rb = 16

    def finish(t, carry):
        r0 = pl.multiple_of(t * rb, rb)
        cols = [accs[pl.ds(r0 * CHUNKS + _chunk_row(c), rb, stride=CHUNKS), :] for c in range(CHUNKS)]
        v = _layer_norm(jnp.concatenate(cols, axis=1) + bdw_ref[...], g_ref[...], b_ref[...])
        z_ref[pl.ds(r0, rb), :] = (v * jax.nn.sigmoid(v)).astype(bf16)
        return carry

    lax.fori_loop(0, TM // rb, finish, 0, unroll=8)


def _conv(yp, w_dw3, b_dw, ln_g, ln_b):
    return pl.pallas_call(
        _conv_kernel,
        grid=(NT,),
        in_specs=[
            pl.BlockSpec((TM * PK, LANES), lambda i: (i, 0)),
            pl.BlockSpec((N_META * PK, LANES), lambda i: (META_ROW // N_META, 0)),
            _resident((CONV_W, CHUNKS, LANES)),
            _resident((1, D)),
            _resident((1, D)),
            _resident((1, D)),
        ],
        out_specs=pl.BlockSpec((TM, D), lambda i: (i, 0)),
        out_shape=jax.ShapeDtypeStruct((TP, D), bf16),
        scratch_shapes=[
            pltpu.VMEM(((TM + HALO) * PK, LANES), u32),
            pltpu.VMEM((TM * CHUNKS, LANES), f32),
        ],
        compiler_params=_cparams(),
        name="conv_ln_swish",
    )(yp, yp, w_dw3, b_dw, ln_g, ln_b)


R_EXP0 = 8


def _route(h, wr_ref, br_ref, running, valid):
    logits = jnp.dot(h.astype(bf16), wr_ref[...], preferred_element_type=f32) + br_ref[...]
    lt = logits.T
    gl = [lt[k:k + 1, :] for k in range(N_GROUPS)]
    gm = functools.reduce(jnp.maximum, gl)
    gex = [jnp.exp(v - gm) for v in gl]
    gden = functools.reduce(lambda a, b: a + b, gex)
    gp = [v / gden for v in gex]
    best = gp[0]
    gi = jnp.zeros((1, TM), i32)
    for k in range(1, N_GROUPS):
        better = gp[k] > best
        gi = jnp.where(better, k, gi)
        best = jnp.where(better, gp[k], best)
    esel = lt[R_EXP0:R_EXP0 + EPG, :]
    for k in range(1, N_GROUPS):
        esel = jnp.where(gi == k, lt[R_EXP0 + EPG * k:R_EXP0 + EPG * (k + 1), :], esel)
    em = jnp.max(esel, axis=0, keepdims=True)
    eex = jnp.exp(esel - em)
    ep = eex / jnp.sum(eex, axis=0, keepdims=True)
    io8 = lax.broadcasted_iota(i32, (EPG, TM), 0)
    v1 = jnp.max(ep, axis=0, keepdims=True)
    i1 = jnp.min(jnp.where(ep == v1, io8, EPG), axis=0, keepdims=True)
    ep2 = jnp.where(io8 == i1, -1.0, ep)
    v2 = jnp.max(ep2, axis=0, keepdims=True)
    i2 = jnp.min(jnp.where(ep2 == v2, io8, EPG), axis=0, keepdims=True)
    s = v1 + v2
    gate0 = best * (v1 / s)
    gate1 = best * (v2 / s)
    f0 = gi * EPG + i1
    f1 = gi * EPG + i2

    io32 = lax.broadcasted_iota(i32, (N_EXP, TM), 0)
    oh0 = (io32 == f0).astype(f32)
    oh1 = (io32 == f1).astype(f32)
    cnt = oh0 + oh1
    upper = (lax.broadcasted_iota(i32, (TM, TM), 0) < lax.broadcasted_iota(i32, (TM, TM), 1))
    before = jnp.dot(cnt.astype(bf16), upper.astype(f32).astype(bf16), preferred_element_type=f32)
    base = running[...] + before
    r0 = jnp.sum(oh0 * base, axis=0, keepdims=True).astype(i32)
    r1 = jnp.sum(oh1 * base, axis=0, keepdims=True).astype(i32)
    running[...] = running[...] + valid * jnp.sum(cnt, axis=1, keepdims=True)

    io128 = lax.broadcasted_iota(i32, (LANES, TM), 0)
    gcol = jnp.where(io128 == 0, gate0, jnp.where(io128 == 1, gate1, 0.0)).T
    return f0, f1, r0, r1, gcol


def _proj_ln_route_epilogue(valid, a, res, bias_ref, g_ref, b_ref, wr_ref, br_ref,
                            h_ref, hp_ref, eidx_ref, rank_ref, gcol_ref, cnt_ref, running):
    mix = a + bias_ref[...]
    h = _layer_norm(ALPHA * res + mix, g_ref[...], b_ref[...])
    h_ref[...] = h
    _pack_rows(h, hp_ref)
    f0, f1, r0, r1, gcol = _route(h, wr_ref, br_ref, running, valid)
    eidx_ref[0, 0:1, :] = f0
    eidx_ref[0, 1:2, :] = f1
    rank_ref[0, 0:1, :] = r0
    rank_ref[0, 1:2, :] = r1
    gcol_ref[...] = gcol
    cnt_ref[...] = running[...]


def _proj_ln_route_kernel(first, a_ref, *refs):
    if first:
        x_ref, meta_ref, w_hbm, *rest = refs
    else:
        res_ref, w_hbm, *rest = refs
    *rest, running, accbuf, w_ref, stage, sem = rest
    i = pl.program_id(0)

    @pl.when(i == 0)
    def _():
        running[...] = jnp.zeros_like(running)
        accbuf[1] = jnp.zeros((TM, D), f32)
        _load_weights_bf16(w_hbm, w_ref, stage, sem)

    valid = (i > 0).astype(f32)
    for parity in range(2):
        @pl.when(lax.rem(i, 2) == parity)
        def _():
            prev = accbuf[1 - parity]
            if first:
                accbuf[parity] = jnp.dot(a_ref[...], w_ref[...], preferred_element_type=f32)
                res = jnp.where(i - 1 == NXT, meta_ref[...], x_ref[...])
            else:
                accbuf[parity] = lax.dot_general(a_ref[...], w_ref[...], (((0,), (0,)), ((), ())),
                                                 preferred_element_type=f32)
                res = res_ref[...]
            _proj_ln_route_epilogue(valid, prev, res, *rest, running)


def _proj_ln_route(a, res, w, bias, ln_g, ln_b, wr, br):
    first = isinstance(res, tuple)
    cur = lambda i: jnp.minimum(i, NT - 1)
    prv = lambda i: jnp.maximum(i - 1, 0)
    if first:
        a_spec = pl.BlockSpec((TM, D), lambda i: (cur(i), 0))
        res_specs = [pl.BlockSpec((TM, D), lambda i: (jnp.minimum(prv(i), NXT - 1), 0)), _resident((TM, D))]
        res_args = list(res)
    else:
        a_spec = pl.BlockSpec((D, TM), lambda i: (0, cur(i)))
        res_specs = [pl.BlockSpec((TM, D), lambda i: (prv(i), 0))]
        res_args = [res]
    tile3 = pl.BlockSpec((1, 2, TM), lambda i: (prv(i), 0, 0))
    return pl.pallas_call(
        functools.partial(_proj_ln_route_kernel, first),
        grid=(NT + 1,),
        in_specs=[a_spec] + res_specs + [
            HBM, _resident((1, D)), _resident((1, D)), _resident((1, D)),
            _resident((D, LANES)), _resident((1, LANES)),
        ],
        out_specs=[
            pl.BlockSpec((TM, D), lambda i: (prv(i), 0)),
            pl.BlockSpec((TM * PK, LANES), lambda i: (prv(i), 0)),
            tile3, tile3,
            pl.BlockSpec((TM, LANES), lambda i: (prv(i), 0)),
            pl.BlockSpec((N_EXP, TM), lambda i: (0, 0)),
        ],
        out_shape=[
            jax.ShapeDtypeStruct((TP, D), f32),
            jax.ShapeDtypeStruct((TP * PK, LANES), u32),
            jax.ShapeDtypeStruct((NT, 2, TM), i32),
            jax.ShapeDtypeStruct((NT, 2, TM), i32),
            jax.ShapeDtypeStruct((TP, LANES), f32),
            jax.ShapeDtypeStruct((N_EXP, TM), f32),
        ],
        scratch_shapes=[pltpu.VMEM((N_EXP, TM), f32), pltpu.VMEM((2, TM, D), f32)] + _weight_scratch(D, D, 512),
        compiler_params=_cparams(),
        name="proj_ln_route",
    )(a, *res_args, w, bias, ln_g, ln_b, wr, br)


def _plan(eidx, rank, cnt):
    counts = cnt[:, 0].astype(i32)
    used = ((counts + TME - 1) // TME) * TME
    padded = ((counts + EBLK - 1) // EBLK) * EBLK
    ends = jnp.cumsum(padded)
    offs = ends - padded
    ntiles = ends[-1] // EBLK
    off_of = jnp.sum(jnp.where(eidx[..., None] == jnp.arange(N_EXP, dtype=i32), offs, 0), axis=-1)
    dest = (off_of + rank).reshape(-1)
    tile_start = jnp.minimum(jnp.arange(NTE, dtype=i32), ntiles - 1) * EBLK
    tile_expert = jnp.minimum(jnp.sum(tile_start[:, None] >= ends[None, :], axis=1), N_EXP - 1).astype(i32)
    nsub = jnp.clip((offs[tile_expert] + used[tile_expert] - tile_start) // TME, 0, ESUB).astype(i32)
    zstart = jnp.where(used > 0, offs + used - TME, 0).astype(i32)
    zflag = (used > 0).astype(i32)
    tid = jnp.arange(NTE, dtype=i32)
    live = tid < ntiles
    first = jnp.logical_and(live, jnp.logical_or(tid == 0, tile_expert != jnp.roll(tile_expert, 1)))
    slot = (jnp.cumsum(first.astype(i32)) - 1) % 2
    nxt_first = lax.cummin(jnp.where(first, tid, NTE), reverse=True)
    after = jnp.concatenate([nxt_first[1:], jnp.full((1,), NTE, i32)])
    nxt = jnp.where(after < NTE, tile_expert[jnp.minimum(after, NTE - 1)], -1)
    return (dest.astype(i32), tile_expert, ntiles.reshape(1).astype(i32), zstart, zflag,
            first.astype(i32), slot.astype(i32), nxt.astype(i32), nsub)


ISSUE_UNROLL = 8


def _scatter_kernel(dest_ref, zstart_ref, zflag_ref, hp_ref, xs_hbm, zeros, sem, zsem):
    i = pl.program_id(0)

    @pl.when(i == 0)
    def _():
        zeros[...] = jnp.zeros_like(zeros)
        def fill(e):
            start = pl.multiple_of(zstart_ref[e], TME)
            return pltpu.make_async_copy(zeros, xs_hbm.at[pl.ds(start, TME)], zsem)

        for e in range(N_EXP):
            @pl.when(zflag_ref[e] > 0)
            def _():
                fill(e).start()
        for e in range(N_EXP):
            @pl.when(zflag_ref[e] > 0)
            def _():
                fill(e).wait()

    base = i * (2 * TM)

    def row(r, carry):
        for k in range(2):
            d = dest_ref[base + k * TM + r]
            pltpu.make_async_copy(hp_ref.at[pl.ds(r, 1)], xs_hbm.at[pl.ds(d, 1)], sem).start(priority=k)
        return carry

    lax.fori_loop(0, TM, row, 0, unroll=ISSUE_UNROLL)

    def drain(r, carry):
        pltpu.make_async_copy(hp_ref.at[pl.ds(0, 1)], xs_hbm.at[pl.ds(0, 1)], sem).wait()
        return carry

    lax.fori_loop(0, 2 * TM, drain, 0, unroll=ISSUE_UNROLL)


def _scatter(dest, zstart, zflag, hp3):
    return pl.pallas_call(
        _scatter_kernel,
        grid_spec=pltpu.PrefetchScalarGridSpec(
            num_scalar_prefetch=3,
            grid=(NT,),
            in_specs=[pl.BlockSpec((TM, PK, LANES), lambda i, *_: (i, 0, 0))],
            out_specs=pl.BlockSpec(memory_space=pl.ANY),
            scratch_shapes=[pltpu.VMEM((TME, PK, LANES), u32), pltpu.SemaphoreType.DMA(()),
                            pltpu.SemaphoreType.DMA(())],
        ),
        out_shape=jax.ShapeDtypeStruct((NS, PK, LANES), u32),
        compiler_params=pltpu.CompilerParams(dimension_semantics=("arbitrary",), vmem_limit_bytes=VMEM_LIMIT,
                                             has_side_effects=True),
        name="moe_scatter",
    )(dest, zstart, zflag, hp3)


def _expert_kernel(layer, te_ref, nt_ref, first_ref, slot_ref, nxt_ref, nsub_ref, xs_ref, w1_hbm, w3_hbm, w2_hbm,
                   ys_ref, wb1, wb3, wb2, w1c, w3c, w2c, sem):
    i = pl.program_id(0)

    def copies(e, s):
        return (pltpu.make_async_copy(w1_hbm.at[layer, e], wb1.at[s], sem.at[s]),
                pltpu.make_async_copy(w3_hbm.at[layer, e], wb3.at[s], sem.at[s]),
                pltpu.make_async_copy(w2_hbm.at[layer, e], wb2.at[s], sem.at[s]))

    @pl.when(i == 0)
    def _():
        for cp in copies(te_ref[0], 0):
            cp.start()

    @pl.when(jnp.logical_and(i < nt_ref[0], first_ref[i] > 0))
    def _():
        s = slot_ref[i]
        for cp in copies(te_ref[i], s):
            cp.wait()

        @pl.when(nxt_ref[i] >= 0)
        def _():
            for cp in copies(nxt_ref[i], 1 - s):
                cp.start()

        w1c[...] = wb1[s].astype(bf16)
        w3c[...] = wb3[s].astype(bf16)
        w2c[...] = wb2[s].astype(bf16)

    for sub in range(ESUB):
        @pl.when(jnp.logical_and(i < nt_ref[0], sub < nsub_ref[i]))
        def _():
            rows = pl.ds(sub * TME * PK, TME * PK)
            xlo, xhi = _unpack_rows(xs_ref.at[rows], TME, bf16)
            a = (jnp.dot(xlo, w1c[0:HALF, :], preferred_element_type=f32)
                 + jnp.dot(xhi, w1c[HALF:D, :], preferred_element_type=f32))
            b = (jnp.dot(xlo, w3c[0:HALF, :], preferred_element_type=f32)
                 + jnp.dot(xhi, w3c[HALF:D, :], preferred_element_type=f32))
            hid = (a * jax.nn.sigmoid(a) * b).astype(bf16)
            _pack_rows(jnp.dot(hid, w2c[...], preferred_element_type=f32), ys_ref.at[rows])


def _experts(layer, tile_expert, ntiles, first, slot, nxt, nsub, xs2d, w1, w3, w2):
    def row_map(i, te, nt, *_):
        return (jnp.minimum(i, nt[0] - 1), 0)

    hbm = pl.BlockSpec(memory_space=pl.ANY)
    return pl.pallas_call(
        functools.partial(_expert_kernel, layer),
        grid_spec=pltpu.PrefetchScalarGridSpec(
            num_scalar_prefetch=6,
            grid=(NTE,),
            in_specs=[pl.BlockSpec((EBLK * PK, LANES), row_map), hbm, hbm, hbm],
            out_specs=pl.BlockSpec((EBLK * PK, LANES), row_map),
            scratch_shapes=[
                pltpu.VMEM((2, D, FF), f32), pltpu.VMEM((2, D, FF), f32), pltpu.VMEM((2, FF, D), f32),
                pltpu.VMEM((D, FF), bf16), pltpu.VMEM((D, FF), bf16), pltpu.VMEM((FF, D), bf16),
                pltpu.SemaphoreType.DMA((2,)),
            ],
        ),
        out_shape=jax.ShapeDtypeStruct((NS * PK, LANES), u32),
        compiler_params=_cparams(),
        name="moe_experts",
    )(tile_expert, ntiles, first, slot, nxt, nsub, xs2d, w1, w3, w2)


def _gather_combine(n, dest_ref, ys_hbm, h_ref, gcol_ref, g_ref, b_ref, buf, sem):
    i = pl.program_id(0)
    slot = lax.rem(i, 2)

    def issue(tile, s):
        base = tile * (2 * TM)

        def row(r, carry):
            for k in range(2):
                d = pl.multiple_of(dest_ref[base + k * TM + r] * PK, PK)
                pltpu.make_async_copy(ys_hbm.at[pl.ds(d, PK)],
                                      buf.at[s, k, pl.ds(pl.multiple_of(r * PK, PK), PK)],
                                      sem.at[s]).start(priority=k)
            return carry

        lax.fori_loop(0, TM, row, 0, unroll=ISSUE_UNROLL)

    @pl.when(i == 0)
    def _():
        issue(0, 0)

    @pl.when(i + 1 < n)
    def _():
        issue(i + 1, 1 - slot)

    def drain(r, carry):
        pltpu.make_async_copy(ys_hbm.at[pl.ds(0, PK)], buf.at[slot, 0, pl.ds(0, PK)], sem.at[slot]).wait()
        return carry

    lax.fori_loop(0, 2 * TM, drain, 0, unroll=ISSUE_UNROLL)

    lo0, hi0 = _unpack_rows(buf.at[slot, 0], TM, f32)
    lo1, hi1 = _unpack_rows(buf.at[slot, 1], TM, f32)
    g0, g1 = gcol_ref[:, 0:1], gcol_ref[:, 1:2]
    ffn = jnp.concatenate([lo0 * g0 + lo1 * g1, hi0 * g0 + hi1 * g1], axis=1)
    return _layer_norm(ALPHA * h_ref[...] + ffn, g_ref[...], b_ref[...])


def _combine_kernel(n, dest_ref, ys_hbm, h_ref, gcol_ref, g_ref, b_ref, o_ref, buf, sem):
    o_ref[...] = _gather_combine(n, dest_ref, ys_hbm, h_ref, gcol_ref, g_ref, b_ref, buf, sem)


def _combine(dest, ys2d, h, gcol, ln_g, ln_b, ntiles_out):
    return pl.pallas_call(
        functools.partial(_combine_kernel, ntiles_out),
        grid_spec=pltpu.PrefetchScalarGridSpec(
            num_scalar_prefetch=1,
            grid=(ntiles_out,),
            in_specs=[
                pl.BlockSpec(memory_space=pl.ANY),
                pl.BlockSpec((TM, D), lambda i, *_: (i, 0)),
                pl.BlockSpec((TM, LANES), lambda i, *_: (i, 0)),
                pl.BlockSpec((1, D), lambda i, *_: (0, 0)),
                pl.BlockSpec((1, D), lambda i, *_: (0, 0)),
            ],
            out_specs=pl.BlockSpec((TM, D), lambda i, *_: (i, 0)),
            scratch_shapes=[pltpu.VMEM((2, 2, TM * PK, LANES), u32), pltpu.SemaphoreType.DMA((2,))],
        ),
        out_shape=jax.ShapeDtypeStruct((ntiles_out * TM, D), f32),
        compiler_params=_cparams(),
        name="moe_combine_ln",
    )(dest, ys2d, h, gcol, ln_g, ln_b)


def _moe_experts(layer, hp2d, eidx, rank, cnt, w1, w3, w2):
    dest, tile_expert, ntiles, zstart, zflag, first, slot, nxt, nsub = _plan(eidx, rank, cnt)
    xs = _scatter(dest, zstart, zflag, hp2d.reshape(TP, PK, LANES))
    ys2d = _experts(layer, tile_expert, ntiles, first, slot, nxt, nsub, xs.reshape(NS * PK, LANES), w1, w3, w2)
    return dest, ys2d


NT_DIMS = (((1,), (1,)), ((), ()))


def _rope_rows(t, cos, sa, sb):
    w = t.shape[1]
    reps = w // LANES
    c = jnp.tile(cos, (1, reps))
    a = jnp.tile(sa, (1, reps))
    b = jnp.tile(sb, (1, reps))
    return t * c + pltpu.roll(t, w - ROT // 2, 1) * a + pltpu.roll(t, ROT // 2, 1) * b


def _combine_qkv_kernel(n, dest_ref, ys_hbm, h_ref, gcol_ref, g_ref, b_ref,
                        wq_hbm, bq_ref, wk_ref, bk_ref, wvT_ref, bv_ref, cosT_ref, sinT_ref, cos_ref, sa_ref, sb_ref,
                        h2_ref, qT_ref, k_ref, vT_ref, buf, gsem, wq_ref, stage, wsem):
    @pl.when(pl.program_id(0) == 0)
    def _():
        _load_weights_bf16(wq_hbm, wq_ref, stage, wsem)

    h2 = _gather_combine(n, dest_ref, ys_hbm, h_ref, gcol_ref, g_ref, b_ref, buf, gsem)
    h2_ref[...] = h2
    hb = h2.astype(bf16)
    scale = LOG2E / math.sqrt(HEAD_DIM)
    half = ROT // 2
    cosT = cosT_ref[...][None]
    sinT = sinT_ref[...][None]
    rows = GQA * HEAD_DIM
    for c in range(D // rows):
        lo, hi = c * rows, (c + 1) * rows
        t = lax.dot_general(wq_ref[:, lo:hi], hb, (((0,), (1,)), ((), ())), preferred_element_type=f32) \
            + bq_ref[lo:hi, :]
        t3 = t.reshape(GQA, HEAD_DIM, TM)
        x1, x2 = t3[:, 0:half, :], t3[:, half:ROT, :]
        r = jnp.concatenate([x1 * cosT - x2 * sinT, x2 * cosT + x1 * sinT, t3[:, ROT:, :]], axis=1)
        qT_ref[lo:hi, :] = (r * scale).reshape(rows, TM).astype(bf16)
    t = jnp.dot(hb, wk_ref[...], preferred_element_type=f32) + bk_ref[...]
    k_ref[...] = _rope_rows(t, cos_ref[...], sa_ref[...], sb_ref[...]).astype(bf16)
    t = lax.dot_general(wvT_ref[...], hb, NT_DIMS, preferred_element_type=f32) + bv_ref[...]
    vT_ref[...] = t.astype(bf16)


def _combine_qkv(dest, ys2d, h, gcol, ln_g, ln_b, wq, bq_col, wk, bk, wvT, bv_col, tables):
    cosT, sinT, cos_t, sa_t, sb_t = tables
    const = lambda shape: pl.BlockSpec(shape, lambda i, *_: (0,) * len(shape))
    tabT = pl.BlockSpec((ROT // 2, TM), lambda i, *_: (0, i))
    tab = pl.BlockSpec((TM, LANES), lambda i, *_: (i, 0))
    return pl.pallas_call(
        functools.partial(_combine_qkv_kernel, NT),
        grid_spec=pltpu.PrefetchScalarGridSpec(
            num_scalar_prefetch=1,
            grid=(NT,),
            in_specs=[
                HBM,
                pl.BlockSpec((TM, D), lambda i, *_: (i, 0)),
                pl.BlockSpec((TM, LANES), lambda i, *_: (i, 0)),
                const((1, D)), const((1, D)),
                HBM, const((D, 1)),
                const((D, KVW)), const((1, KVW)),
                const((KVW, D)), const((KVW, 1)),
                tabT, tabT, tab, tab, tab,
            ],
            out_specs=[
                pl.BlockSpec((TM, D), lambda i, *_: (i, 0)),
                pl.BlockSpec((D, TM), lambda i, *_: (0, i)),
                pl.BlockSpec((TM, KVW), lambda i, *_: (i, 0)),
                pl.BlockSpec((KVW, TM), lambda i, *_: (0, i)),
            ],
            scratch_shapes=[pltpu.VMEM((2, 2, TM * PK, LANES), u32), pltpu.SemaphoreType.DMA((2,))]
            + _weight_scratch(D, D, 512),
        ),
        out_shape=[
            jax.ShapeDtypeStruct((TP, D), f32),
            jax.ShapeDtypeStruct((D, TP), bf16),
            jax.ShapeDtypeStruct((TP, KVW), bf16),
            jax.ShapeDtypeStruct((KVW, TP), bf16),
        ],
        compiler_params=_cparams(),
        name="moe_combine_ln_qkv_rope",
    )(dest, ys2d, h, gcol, ln_g, ln_b, wq, bq_col, wk, bk, wvT, bv_col, cosT, sinT, cos_t, sa_t, sb_t)


NKEY = 2 * QB + N_META
HC = 8
LW = HC * QB
SUB = 8


def _col_max(s):
    parts = [s[r * SUB:(r + 1) * SUB] for r in range(NKEY // SUB)]
    while len(parts) > 1:
        nxt = [jnp.maximum(parts[j], parts[j + 1]) for j in range(0, len(parts) - 1, 2)]
        if len(parts) % 2:
            nxt.append(parts[-1])
        parts = nxt
    return jnp.max(parts[0], axis=0, keepdims=True)


def _attn_kernel(qT_ref, kc_ref, kp_ref, km_ref, vTc_ref, vTp_ref, vTm_ref, sink_ref, oT_ref):
    i = pl.program_id(0)
    is_meta = i == NT - 1
    ck = lax.broadcasted_iota(i32, (NKEY, QB), 0)
    rq = lax.broadcasted_iota(i32, (NKEY, QB), 1)
    in_band = jnp.logical_and(ck > rq, ck <= rq + QB)
    meta_ok = jnp.logical_and(ck >= 2 * QB, jnp.logical_or(jnp.logical_not(is_meta), ck - 2 * QB <= rq))
    ones = jnp.ones((SUB, NKEY), bf16)

    for blk in range(TM // QB):
        lo = jnp.where(is_meta, 2 * QB, jnp.where(jnp.logical_and(i == 0, blk == 0), QB, 0))
        valid = jnp.logical_or(meta_ok, jnp.logical_and(in_band, ck >= lo))
        bias = jnp.where(valid, 0.0, -jnp.inf)
        bias = jnp.concatenate([bias] * HC, axis=1)
        c0 = blk * QB
        for g in range(N_KV):
            gs = slice(g * HEAD_DIM, (g + 1) * HEAD_DIM)
            if blk == 0:
                kprev, vprevT = kp_ref[:, gs], vTp_ref[gs, :]
            else:
                kprev, vprevT = kc_ref[c0 - QB:c0, gs], vTc_ref[gs, c0 - QB:c0]
            kcat = jnp.concatenate([kprev, kc_ref[c0:c0 + QB, gs], km_ref[:, gs]], axis=0)
            vcatT = jnp.concatenate([vprevT, vTc_ref[gs, c0:c0 + QB], vTm_ref[gs, 0:N_META]], axis=1)
            vext = jnp.concatenate([vcatT, ones], axis=0)
            for c in range(GQA // HC):
                h0 = g * GQA + c * HC
                heads = [qT_ref[(h0 + j) * HEAD_DIM:(h0 + j + 1) * HEAD_DIM, c0:c0 + QB] for j in range(HC)]
                s = jnp.dot(kcat, jnp.concatenate(heads, axis=1), preferred_element_type=f32) + bias
                sink = sink_ref[h0 // HC:h0 // HC + 1, :]
                m = jnp.maximum(_col_max(s), sink)
                p = jnp.exp2(s - m).astype(bf16)
                oe = jnp.dot(vext, p, preferred_element_type=f32)
                den = oe[HEAD_DIM:HEAD_DIM + 1, :] + jnp.exp2(sink - m)
                o = (oe[0:HEAD_DIM, :] * (1.0 / den)).astype(bf16)
                for j in range(HC):
                    oT_ref[(h0 + j) * HEAD_DIM:(h0 + j + 1) * HEAD_DIM, c0:c0 + QB] = o[:, j * QB:(j + 1) * QB]


def _attention(qT, k, vT, sink_lanes):
    prev_blk = lambda i: jnp.maximum(i * (TM // QB) - 1, 0)
    return pl.pallas_call(
        _attn_kernel,
        grid=(NT,),
        in_specs=[
            pl.BlockSpec((D, TM), lambda i: (0, i)),
            pl.BlockSpec((TM, KVW), lambda i: (i, 0)),
            pl.BlockSpec((QB, KVW), lambda i: (prev_blk(i), 0)),
            pl.BlockSpec((N_META, KVW), lambda i: (META_ROW // N_META, 0)),
            pl.BlockSpec((KVW, TM), lambda i: (0, i)),
            pl.BlockSpec((KVW, QB), lambda i: (0, prev_blk(i))),
            pl.BlockSpec((KVW, LANES), lambda i: (0, META_ROW // LANES)),
            _resident((N_HEADS // HC, LW)),
        ],
        out_specs=pl.BlockSpec((D, TM), lambda i: (0, i)),
        out_shape=jax.ShapeDtypeStruct((D, TP), bf16),
        compiler_params=_cparams(),
        name="swa_attention",
    )(qT, k, k, k, vT, vT, vT, sink_lanes)


def _router_weights(wg, bg, we, be):
    gap = R_EXP0 - N_GROUPS
    tail = LANES - R_EXP0 - N_EXP
    wr = jnp.concatenate([wg, jnp.zeros((D, gap), f32), we, jnp.zeros((D, tail), f32)], axis=1)
    br = jnp.concatenate([bg, jnp.zeros((gap,), f32), be, jnp.zeros((tail,), f32)]).reshape(1, LANES)
    return wr.astype(bf16), br


def _rope_tables():
    pos = np.concatenate([np.arange(SEQ) + N_META, np.arange(TM)]).astype(np.float32)
    half = ROT // 2
    inv_freq = (np.float32(ROPE_THETA) ** (-np.arange(0, ROT, 2, dtype=np.float32) / np.float32(ROT)))
    ang = pos[:, None] * inv_freq.astype(np.float32)[None, :]
    cos, sin = np.cos(ang).astype(np.float32), np.sin(ang).astype(np.float32)
    ones = np.ones((TP, HEAD_DIM - ROT), np.float32)
    zeros = np.zeros((TP, HEAD_DIM - ROT), np.float32)
    z8 = np.zeros((TP, half), np.float32)
    cos_h = np.concatenate([cos, cos, ones], axis=1)
    sa_h = np.concatenate([-sin, z8, zeros], axis=1)
    sb_h = np.concatenate([z8, sin, zeros], axis=1)
    rep = LANES // HEAD_DIM
    tabs = (cos.T, sin.T, np.tile(cos_h, (1, rep)), np.tile(sa_h, (1, rep)), np.tile(sb_h, (1, rep)))
    return tuple(jnp.asarray(np.ascontiguousarray(t)) for t in tabs)


def kernel(x, meta_tokens, conv_w_in, conv_b_in, conv_w_dw, conv_b_dw, conv_ln_g, conv_ln_b, conv_w_out,
           conv_b_out, w_k, b_k, w_v, b_v, w_q, b_q, w_o, b_o, sinks, ln_mix_g, ln_mix_b, ln_ffn_g, ln_ffn_b,
           router_group_w, router_group_b, router_expert_w, router_expert_b, expert_w1, expert_w3, expert_w2):
    assert x.shape == (1, SEQ, D)
    row = lambda v: v.reshape(1, -1)
    col = lambda v: v.reshape(-1, 1)
    x2d = x.reshape(SEQ, D)
    meta_pad = jnp.pad(meta_tokens.astype(f32), ((0, TM - N_META), (0, 0)))

    y = _glu(x2d, meta_pad, conv_w_in[0], row(conv_b_in[0]))
    w_dw = conv_w_dw[0].reshape(CONV_W, 2, PK, LANES).transpose(0, 2, 1, 3).reshape(CONV_W, CHUNKS, LANES)
    z = _conv(y, w_dw.astype(bf16), row(conv_b_dw[0]), row(conv_ln_g[0]),
              row(conv_ln_b[0]))
    wr, br = _router_weights(router_group_w[0], router_group_b[0], router_expert_w[0], router_expert_b[0])
    h, hp, eidx, rank, gcol, cnt = _proj_ln_route(
        z, (x2d, meta_pad), conv_w_out[0], row(conv_b_out[0]), row(ln_mix_g[0]), row(ln_mix_b[0]),
        wr, br)
    dest, ys2d = _moe_experts(0, hp, eidx, rank, cnt, expert_w1, expert_w3, expert_w2)

    h, qT, k, vT = _combine_qkv(dest, ys2d, h, gcol, row(ln_ffn_g[0]), row(ln_ffn_b[0]),
                                w_q[0], col(b_q[0]), w_k.astype(bf16), row(b_k),
                                w_v.T.astype(bf16), col(b_v), _rope_tables())
    sink_lanes = jnp.repeat((sinks[0].astype(f32) * LOG2E).reshape(N_HEADS // HC, HC), QB, axis=1)
    attT = _attention(qT, k, vT, sink_lanes)
    wr, br = _router_weights(router_group_w[1], router_group_b[1], router_expert_w[1], router_expert_b[1])
    h, hp, eidx, rank, gcol, cnt = _proj_ln_route(
        attT, h, w_o[0], row(b_o[0]), row(ln_mix_g[1]), row(ln_mix_b[1]), wr, br)
    dest, ys2d = _moe_experts(1, hp, eidx, rank, cnt, expert_w1, expert_w3, expert_w2)
    out = _combine(dest, ys2d, h, gcol, row(ln_ffn_g[1]), row(ln_ffn_b[1]), NXT)
    return out.reshape(1, SEQ, D)
```

```python
import functools
import math

import jax
import jax.numpy as jnp
import numpy as np
from jax import lax
from jax.experimental import pallas as pl
from jax.experimental.pallas import tpu as pltpu

f32 = jnp.float32
bf16 = jnp.bfloat16
i32 = jnp.int32
u32 = jnp.uint32

D = 2048
SEQ = 8192
DEPTH = 2
N_META = 16
CONV_W = 31
HEAD_DIM = 64
N_HEADS = 32
N_KV = 4
GQA = 8
KVW = N_KV * HEAD_DIM
WINDOW = 128
ROT = 16
ROPE_THETA = 500000.0
N_GROUPS = 4
EPG = 8
N_EXP = 32
FF = 256
ALPHA = (2.0 * DEPTH) ** 0.25
LN_EPS = 1e-5
LOG2E = math.log2(math.e)

LANES = 128
TM = 256
NXT = SEQ // TM
NT = NXT + 1
TP = NT * TM
META_ROW = SEQ
CHUNKS = D // LANES
HALO = 32
TME = 256
ESUB = 2
EBLK = ESUB * TME
NTE = (2 * TP) // EBLK + N_EXP
NS = NTE * EBLK
QB = 128
VMEM_LIMIT = 52 * 1024 * 1024


def _cparams():
    return pltpu.CompilerParams(dimension_semantics=("arbitrary",), vmem_limit_bytes=VMEM_LIMIT)


def _resident(shape):
    nd = len(shape)
    return pl.BlockSpec(shape, lambda *a: (0,) * nd, pipeline_mode=pl.Buffered(1))


def _layer_norm(x, g, b):
    mu = jnp.mean(x, axis=-1, keepdims=True)
    xc = x - mu
    var = jnp.mean(xc * xc, axis=-1, keepdims=True)
    return xc * lax.rsqrt(var + LN_EPS) * g + b


def _x_or_meta(i, x_ref, meta_ref):
    return jnp.where(i == NXT, meta_ref[...], x_ref[...])


HALF = D // 2
PK = HALF // LANES


def _pack_rows(v, out2d):
    rows = v.shape[0]
    bits = pltpu.bitcast(v.astype(bf16).astype(f32), u32)
    word = bits[:, HALF:] | lax.shift_right_logical(bits[:, :HALF], jnp.uint32(16))
    for s in range(PK):
        out2d[pl.ds(s, rows, stride=PK), :] = word[:, s * LANES:(s + 1) * LANES]


def _unpack_rows(in2d, rows, dtype):
    lo, hi = [], []
    for s in range(PK):
        w = in2d[pl.ds(s, rows, stride=PK), :]
        lo.append(pltpu.bitcast(lax.shift_left(w, jnp.uint32(16)), f32).astype(dtype))
        hi.append(pltpu.bitcast(w & jnp.uint32(0xFFFF0000), f32).astype(dtype))
    return jnp.concatenate(lo, axis=1), jnp.concatenate(hi, axis=1)


def _load_weights_bf16(w_hbm, w_vmem, stage, sem):
    rc = stage.shape[1]
    n = w_hbm.shape[0] // rc

    def cp(c):
        return pltpu.make_async_copy(w_hbm.at[pl.ds(c * rc, rc)], stage.at[c % 2], sem.at[c % 2])

    cp(0).start()
    for c in range(n):
        cp(c).wait()
        if c + 1 < n:
            cp(c + 1).start()
        w_vmem[c * rc:(c + 1) * rc, :] = stage[c % 2].astype(bf16)


def _weight_scratch(rows, cols, chunk_rows):
    return [pltpu.VMEM((rows, cols), bf16), pltpu.VMEM((2, chunk_rows, cols), f32), pltpu.SemaphoreType.DMA((2,))]


HBM = pl.BlockSpec(memory_space=pl.ANY)


def _glu_kernel(x_ref, meta_ref, w_hbm, b_ref, yp_ref, ybuf, w_ref, stage, sem):
    i = pl.program_id(0)

    @pl.when(i == 0)
    def _():
        _load_weights_bf16(w_hbm, w_ref, stage, sem)

    xb = _x_or_meta(i, x_ref, meta_ref).astype(bf16)
    cw = 512
    for c in range(D // cw):
        lo, hi = c * cw, (c + 1) * cw
        a = jnp.dot(xb, w_ref[:, lo:hi], preferred_element_type=f32) + b_ref[:, lo:hi]
        g = jnp.dot(xb, w_ref[:, D + lo:D + hi], preferred_element_type=f32) + b_ref[:, D + lo:D + hi]
        ybuf[:, lo:hi] = a * jax.nn.sigmoid(g)
    _pack_rows(ybuf[...], yp_ref)


def _glu(x2d, meta_pad, w_in, b_in):
    return pl.pallas_call(
        _glu_kernel,
        grid=(NT,),
        in_specs=[
            pl.BlockSpec((TM, D), lambda i: (jnp.minimum(i, NXT - 1), 0)),
            _resident((TM, D)),
            HBM,
            _resident((1, 2 * D)),
        ],
        out_specs=pl.BlockSpec((TM * PK, LANES), lambda i: (i, 0)),
        out_shape=jax.ShapeDtypeStruct((TP * PK, LANES), u32),
        scratch_shapes=[pltpu.VMEM((TM, D), f32)] + _weight_scratch(D, 2 * D, 256),
        compiler_params=_cparams(),
        name="glu",
    )(x2d, meta_pad, w_in, b_in)


def _chunk_row(c):
    return 2 * (c % PK) + c // PK


def _conv_kernel(yp_ref, ymeta_ref, w_ref, bdw_ref, g_ref, b_ref, z_ref, scr, accs):
    i = pl.program_id(0)
    hrows = HALO * PK
    trows = TM * PK

    @pl.when(i == 0)
    def _():
        scr[0:(HALO - N_META) * PK, :] = jnp.zeros(((HALO - N_META) * PK, LANES), u32)
        scr[(HALO - N_META) * PK:hrows, :] = ymeta_ref[...]

    @pl.when(i == NT - 1)
    def _():
        scr[0:hrows, :] = jnp.zeros((hrows, LANES), u32)

    @pl.when(jnp.logical_and(i > 0, i < NT - 1))
    def _():
        scr[0:hrows, :] = scr[trows:trows + hrows, :]

    scr[hrows:hrows + trows, :] = yp_ref[...]

    tb = 16
    first = (HALO - (CONV_W - 1)) * PK

    def block(t, carry):
        base = pl.multiple_of(t * (tb * PK), tb * PK)
        acc = jnp.zeros((tb, CHUNKS, LANES), f32)
        for j in range(CONV_W):
            words = scr[pl.ds(base + first + j * PK, tb * PK), :]
            sl = pltpu.bitcast(words, bf16).reshape(tb, CHUNKS, LANES)
            acc = acc + sl.astype(f32) * w_ref[j].astype(f32)[None]
        accs[pl.ds(pl.multiple_of(t * (tb * CHUNKS), tb * CHUNKS), tb * CHUNKS), :] = acc.reshape(tb * CHUNKS, LANES)
        return carry

    lax.fori_loop(0, TM // tb, block, 0)

    rb = 16

    def finish(t, carry):
        r0 = pl.multiple_of(t * rb, rb)
        cols = [accs[pl.ds(r0 * CHUNKS + _chunk_row(c), rb, stride=CHUNKS), :] for c in range(CHUNKS)]
        v = _layer_norm(jnp.concatenate(cols, axis=1) + bdw_ref[...], g_ref[...], b_ref[...])
        z_ref[pl.ds(r0, rb), :] = (v * jax.nn.sigmoid(v)).astype(bf16)
        return carry

    lax.fori_loop(0, TM // rb, finish, 0, unroll=8)


def _conv(yp, w_dw3, b_dw, ln_g, ln_b):
    return pl.pallas_call(
        _conv_kernel,
        grid=(NT,),
        in_specs=[
            pl.BlockSpec((TM * PK, LANES), lambda i: (i, 0)),
            pl.BlockSpec((N_META * PK, LANES), lambda i: (META_ROW // N_META, 0)),
            _resident((CONV_W, CHUNKS, LANES)),
            _resident((1, D)),
            _resident((1, D)),
            _resident((1, D)),
        ],
        out_specs=pl.BlockSpec((TM, D), lambda i: (i, 0)),
        out_shape=jax.ShapeDtypeStruct((TP, D), bf16),
        scratch_shapes=[
            pltpu.VMEM(((TM + HALO) * PK, LANES), u32),
            pltpu.VMEM((TM * CHUNKS, LANES), f32),
        ],
        compiler_params=_cparams(),
        name="conv_ln_swish",
    )(yp, yp, w_dw3, b_dw, ln_g, ln_b)


R_EXP0 = 8


def _route(h, wr_ref, br_ref, running, valid):
    logits = jnp.dot(h.astype(bf16), wr_ref[...], preferred_element_type=f32) + br_ref[...]
    lt = logits.T
    gl = [lt[k:k + 1, :] for k in range(N_GROUPS)]
    gm = functools.reduce(jnp.maximum, gl)
    gex = [jnp.exp(v - gm) for v in gl]
    gden = functools.reduce(lambda a, b: a + b, gex)
    gp = [v / gden for v in gex]
    best = gp[0]
    gi = jnp.zeros((1, TM), i32)
    for k in range(1, N_GROUPS):
        better = gp[k] > best
        gi = jnp.where(better, k, gi)
        best = jnp.where(better, gp[k], best)
    esel = lt[R_EXP0:R_EXP0 + EPG, :]
    for k in range(1, N_GROUPS):
        esel = jnp.where(gi == k, lt[R_EXP0 + EPG * k:R_EXP0 + EPG * (k + 1), :], esel)
    em = jnp.max(esel, axis=0, keepdims=True)
    eex = jnp.exp(esel - em)
    ep = eex / jnp.sum(eex, axis=0, keepdims=True)
    io8 = lax.broadcasted_iota(i32, (EPG, TM), 0)
    v1 = jnp.max(ep, axis=0, keepdims=True)
    i1 = jnp.min(jnp.where(ep == v1, io8, EPG), axis=0, keepdims=True)
    ep2 = jnp.where(io8 == i1, -1.0, ep)
    v2 = jnp.max(ep2, axis=0, keepdims=True)
    i2 = jnp.min(jnp.where(ep2 == v2, io8, EPG), axis=0, keepdims=True)
    s = v1 + v2
    gate0 = best * (v1 / s)
    gate1 = best * (v2 / s)
    f0 = gi * EPG + i1
    f1 = gi * EPG + i2

    io32 = lax.broadcasted_iota(i32, (N_EXP, TM), 0)
    oh0 = (io32 == f0).astype(f32)
    oh1 = (io32 == f1).astype(f32)
    cnt = oh0 + oh1
    upper = (lax.broadcasted_iota(i32, (TM, TM), 0) < lax.broadcasted_iota(i32, (TM, TM), 1))
    before = jnp.dot(cnt.astype(bf16), upper.astype(f32).astype(bf16), preferred_element_type=f32)
    base = running[...] + before
    r0 = jnp.sum(oh0 * base, axis=0, keepdims=True).astype(i32)
    r1 = jnp.sum(oh1 * base, axis=0, keepdims=True).astype(i32)
    running[...] = running[...] + valid * jnp.sum(cnt, axis=1, keepdims=True)

    io128 = lax.broadcasted_iota(i32, (LANES, TM), 0)
    gcol = jnp.where(io128 == 0, gate0, jnp.where(io128 == 1, gate1, 0.0)).T
    return f0, f1, r0, r1, gcol


def _proj_ln_route_epilogue(valid, a, res, bias_ref, g_ref, b_ref, wr_ref, br_ref,
                            h_ref, hp_ref, eidx_ref, rank_ref, gcol_ref, cnt_ref, running):
    mix = a + bias_ref[...]
    h = _layer_norm(ALPHA * res + mix, g_ref[...], b_ref[...])
    h_ref[...] = h
    _pack_rows(h, hp_ref)
    f0, f1, r0, r1, gcol = _route(h, wr_ref, br_ref, running, valid)
    eidx_ref[0, 0:1, :] = f0
    eidx_ref[0, 1:2, :] = f1
    rank_ref[0, 0:1, :] = r0
    rank_ref[0, 1:2, :] = r1
    gcol_ref[...] = gcol
    cnt_ref[...] = running[...]


def _proj_ln_route_kernel(first, a_ref, *refs):
    if first:
        x_ref, meta_ref, w_hbm, *rest = refs
    else:
        res_ref, w_hbm, *rest = refs
    *rest, running, accbuf, w_ref, stage, sem = rest
    i = pl.program_id(0)

    @pl.when(i == 0)
    def _():
        running[...] = jnp.zeros_like(running)
        accbuf[1] = jnp.zeros((TM, D), f32)
        _load_weights_bf16(w_hbm, w_ref, stage, sem)

    valid = (i > 0).astype(f32)
    for parity in range(2):
        @pl.when(lax.rem(i, 2) == parity)
        def _():
            prev = accbuf[1 - parity]
            if first:
                accbuf[parity] = jnp.dot(a_ref[...], w_ref[...], preferred_element_type=f32)
                res = jnp.where(i - 1 == NXT, meta_ref[...], x_ref[...])
            else:
                accbuf[parity] = lax.dot_general(a_ref[...], w_ref[...], (((0,), (0,)), ((), ())),
                                                 preferred_element_type=f32)
                res = res_ref[...]
            _proj_ln_route_epilogue(valid, prev, res, *rest, running)


def _proj_ln_route(a, res, w, bias, ln_g, ln_b, wr, br):
    first = isinstance(res, tuple)
    cur = lambda i: jnp.minimum(i, NT - 1)
    prv = lambda i: jnp.maximum(i - 1, 0)
    if first:
        a_spec = pl.BlockSpec((TM, D), lambda i: (cur(i), 0))
        res_specs = [pl.BlockSpec((TM, D), lambda i: (jnp.minimum(prv(i), NXT - 1), 0)), _resident((TM, D))]
        res_args = list(res)
    else:
        a_spec = pl.BlockSpec((D, TM), lambda i: (0, cur(i)))
        res_specs = [pl.BlockSpec((TM, D), lambda i: (prv(i), 0))]
        res_args = [res]
    tile3 = pl.BlockSpec((1, 2, TM), lambda i: (prv(i), 0, 0))
    return pl.pallas_call(
        functools.partial(_proj_ln_route_kernel, first),
        grid=(NT + 1,),
        in_specs=[a_spec] + res_specs + [
            HBM, _resident((1, D)), _resident((1, D)), _resident((1, D)),
            _resident((D, LANES)), _resident((1, LANES)),
        ],
        out_specs=[
            pl.BlockSpec((TM, D), lambda i: (prv(i), 0)),
            pl.BlockSpec((TM * PK, LANES), lambda i: (prv(i), 0)),
            tile3, tile3,
            pl.BlockSpec((TM, LANES), lambda i: (prv(i), 0)),
            pl.BlockSpec((N_EXP, TM), lambda i: (0, 0)),
        ],
        out_shape=[
            jax.ShapeDtypeStruct((TP, D), f32),
            jax.ShapeDtypeStruct((TP * PK, LANES), u32),
            jax.ShapeDtypeStruct((NT, 2, TM), i32),
            jax.ShapeDtypeStruct((NT, 2, TM), i32),
            jax.ShapeDtypeStruct((TP, LANES), f32),
            jax.ShapeDtypeStruct((N_EXP, TM), f32),
        ],
        scratch_shapes=[pltpu.VMEM((N_EXP, TM), f32), pltpu.VMEM((2, TM, D), f32)] + _weight_scratch(D, D, 512),
        compiler_params=_cparams(),
        name="proj_ln_route",
    )(a, *res_args, w, bias, ln_g, ln_b, wr, br)


def _plan(eidx, rank, cnt):
    counts = cnt[:, 0].astype(i32)
    used = ((counts + TME - 1) // TME) * TME
    padded = ((counts + EBLK - 1) // EBLK) * EBLK
    ends = jnp.cumsum(padded)
    offs = ends - padded
    ntiles = ends[-1] // EBLK
    off_of = jnp.sum(jnp.where(eidx[..., None] == jnp.arange(N_EXP, dtype=i32), offs, 0), axis=-1)
    dest = (off_of + rank).reshape(-1)
    tile_start = jnp.minimum(jnp.arange(NTE, dtype=i32), ntiles - 1) * EBLK
    tile_expert = jnp.minimum(jnp.sum(tile_start[:, None] >= ends[None, :], axis=1), N_EXP - 1).astype(i32)
    nsub = jnp.clip((offs[tile_expert] + used[tile_expert] - tile_start) // TME, 0, ESUB).astype(i32)
    zstart = jnp.where(used > 0, offs + used - TME, 0).astype(i32)
    zflag = (used > 0).astype(i32)
    tid = jnp.arange(NTE, dtype=i32)
    live = tid < ntiles
    first = jnp.logical_and(live, jnp.logical_or(tid == 0, tile_expert != jnp.roll(tile_expert, 1)))
    slot = (jnp.cumsum(first.astype(i32)) - 1) % 2
    nxt_first = lax.cummin(jnp.where(first, tid, NTE), reverse=True)
    after = jnp.concatenate([nxt_first[1:], jnp.full((1,), NTE, i32)])
    nxt = jnp.where(after < NTE, tile_expert[jnp.minimum(after, NTE - 1)], -1)
    return (dest.astype(i32), tile_expert, ntiles.reshape(1).astype(i32), zstart, zflag,
            first.astype(i32), slot.astype(i32), nxt.astype(i32), nsub)


ISSUE_UNROLL = 8


def _scatter_kernel(dest_ref, zstart_ref, zflag_ref, hp_ref, xs_hbm, zeros, sem, zsem):
    i = pl.program_id(0)

    @pl.when(i == 0)
    def _():
        zeros[...] = jnp.zeros_like(zeros)
        def fill(e):
            start = pl.multiple_of(zstart_ref[e], TME)
            return pltpu.make_async_copy(zeros, xs_hbm.at[pl.ds(start, TME)], zsem)

        for e in range(N_EXP):
            @pl.when(zflag_ref[e] > 0)
            def _():
                fill(e).start()
        for e in range(N_EXP):
            @pl.when(zflag_ref[e] > 0)
            def _():
                fill(e).wait()

    base = i * (2 * TM)

    def row(r, carry):
        for k in range(2):
            d = dest_ref[base + k * TM + r]
            pltpu.make_async_copy(hp_ref.at[pl.ds(r, 1)], xs_hbm.at[pl.ds(d, 1)], sem).start(priority=k)
        return carry

    lax.fori_loop(0, TM, row, 0, unroll=ISSUE_UNROLL)

    def drain(r, carry):
        pltpu.make_async_copy(hp_ref.at[pl.ds(0, 1)], xs_hbm.at[pl.ds(0, 1)], sem).wait()
        return carry

    lax.fori_loop(0, 2 * TM, drain, 0, unroll=ISSUE_UNROLL)


def _scatter(dest, zstart, zflag, hp3):
    return pl.pallas_call(
        _scatter_kernel,
        grid_spec=pltpu.PrefetchScalarGridSpec(
            num_scalar_prefetch=3,
            grid=(NT,),
            in_specs=[pl.BlockSpec((TM, PK, LANES), lambda i, *_: (i, 0, 0))],
            out_specs=pl.BlockSpec(memory_space=pl.ANY),
            scratch_shapes=[pltpu.VMEM((TME, PK, LANES), u32), pltpu.SemaphoreType.DMA(()),
                            pltpu.SemaphoreType.DMA(())],
        ),
        out_shape=jax.ShapeDtypeStruct((NS, PK, LANES), u32),
        compiler_params=pltpu.CompilerParams(dimension_semantics=("arbitrary",), vmem_limit_bytes=VMEM_LIMIT,
                                             has_side_effects=True),
        name="moe_scatter",
    )(dest, zstart, zflag, hp3)


def _expert_kernel(layer, te_ref, nt_ref, first_ref, slot_ref, nxt_ref, nsub_ref, xs_ref, w1_hbm, w3_hbm, w2_hbm,
                   ys_ref, wb1, wb3, wb2, w1c, w3c, w2c, sem):
    i = pl.program_id(0)

    def copies(e, s):
        return (pltpu.make_async_copy(w1_hbm.at[layer, e], wb1.at[s], sem.at[s]),
                pltpu.make_async_copy(w3_hbm.at[layer, e], wb3.at[s], sem.at[s]),
                pltpu.make_async_copy(w2_hbm.at[layer, e], wb2.at[s], sem.at[s]))

    @pl.when(i == 0)
    def _():
        for cp in copies(te_ref[0], 0):
            cp.start()

    @pl.when(jnp.logical_and(i < nt_ref[0], first_ref[i] > 0))
    def _():
        s = slot_ref[i]
        for cp in copies(te_ref[i], s):
            cp.wait()

        @pl.when(nxt_ref[i] >= 0)
        def _():
            for cp in copies(nxt_ref[i], 1 - s):
                cp.start()

        w1c[...] = wb1[s].astype(bf16)
        w3c[...] = wb3[s].astype(bf16)
        w2c[...] = wb2[s].astype(bf16)

    for sub in range(ESUB):
        @pl.when(jnp.logical_and(i < nt_ref[0], sub < nsub_ref[i]))
        def _():
            rows = pl.ds(sub * TME * PK, TME * PK)
            xlo, xhi = _unpack_rows(xs_ref.at[rows], TME, bf16)
            a = (jnp.dot(xlo, w1c[0:HALF, :], preferred_element_type=f32)
                 + jnp.dot(xhi, w1c[HALF:D, :], preferred_element_type=f32))
            b = (jnp.dot(xlo, w3c[0:HALF, :], preferred_element_type=f32)
                 + jnp.dot(xhi, w3c[HALF:D, :], preferred_element_type=f32))
            hid = (a * jax.nn.sigmoid(a) * b).astype(bf16)
            _pack_rows(jnp.dot(hid, w2c[...], preferred_element_type=f32), ys_ref.at[rows])


def _experts(layer, tile_expert, ntiles, first, slot, nxt, nsub, xs2d, w1, w3, w2):
    def row_map(i, te, nt, *_):
        return (jnp.minimum(i, nt[0] - 1), 0)

    hbm = pl.BlockSpec(memory_space=pl.ANY)
    return pl.pallas_call(
        functools.partial(_expert_kernel, layer),
        grid_spec=pltpu.PrefetchScalarGridSpec(
            num_scalar_prefetch=6,
            grid=(NTE,),
            in_specs=[pl.BlockSpec((EBLK * PK, LANES), row_map), hbm, hbm, hbm],
            out_specs=pl.BlockSpec((EBLK * PK, LANES), row_map),
            scratch_shapes=[
                pltpu.VMEM((2, D, FF), f32), pltpu.VMEM((2, D, FF), f32), pltpu.VMEM((2, FF, D), f32),
                pltpu.VMEM((D, FF), bf16), pltpu.VMEM((D, FF), bf16), pltpu.VMEM((FF, D), bf16),
                pltpu.SemaphoreType.DMA((2,)),
            ],
        ),
        out_shape=jax.ShapeDtypeStruct((NS * PK, LANES), u32),
        compiler_params=_cparams(),
        name="moe_experts",
    )(tile_expert, ntiles, first, slot, nxt, nsub, xs2d, w1, w3, w2)


def _gather_combine(n, dest_ref, ys_hbm, h_ref, gcol_ref, g_ref, b_ref, buf, sem):
    i = pl.program_id(0)
    slot = lax.rem(i, 2)

    def issue(tile, s):
        base = tile * (2 * TM)

        def row(r, carry):
            for k in range(2):
                d = pl.multiple_of(dest_ref[base + k * TM + r] * PK, PK)
                pltpu.make_async_copy(ys_hbm.at[pl.ds(d, PK)],
                                      buf.at[s, k, pl.ds(pl.multiple_of(r * PK, PK), PK)],
                                      sem.at[s]).start(priority=k)
            return carry

        lax.fori_loop(0, TM, row, 0, unroll=ISSUE_UNROLL)

    @pl.when(i == 0)
    def _():
        issue(0, 0)

    @pl.when(i + 1 < n)
    def _():
        issue(i + 1, 1 - slot)

    def drain(r, carry):
        pltpu.make_async_copy(ys_hbm.at[pl.ds(0, PK)], buf.at[slot, 0, pl.ds(0, PK)], sem.at[slot]).wait()
        return carry

    lax.fori_loop(0, 2 * TM, drain, 0, unroll=ISSUE_UNROLL)

    lo0, hi0 = _unpack_rows(buf.at[slot, 0], TM, f32)
    lo1, hi1 = _unpack_rows(buf.at[slot, 1], TM, f32)
    g0, g1 = gcol_ref[:, 0:1], gcol_ref[:, 1:2]
    ffn = jnp.concatenate([lo0 * g0 + lo1 * g1, hi0 * g0 + hi1 * g1], axis=1)
    return _layer_norm(ALPHA * h_ref[...] + ffn, g_ref[...], b_ref[...])


def _combine_kernel(n, dest_ref, ys_hbm, h_ref, gcol_ref, g_ref, b_ref, o_ref, buf, sem):
    o_ref[...] = _gather_combine(n, dest_ref, ys_hbm, h_ref, gcol_ref, g_ref, b_ref, buf, sem)


def _combine(dest, ys2d, h, gcol, ln_g, ln_b, ntiles_out):
    return pl.pallas_call(
        functools.partial(_combine_kernel, ntiles_out),
        grid_spec=pltpu.PrefetchScalarGridSpec(
            num_scalar_prefetch=1,
            grid=(ntiles_out,),
            in_specs=[
                pl.BlockSpec(memory_space=pl.ANY),
                pl.BlockSpec((TM, D), lambda i, *_: (i, 0)),
                pl.BlockSpec((TM, LANES), lambda i, *_: (i, 0)),
                pl.BlockSpec((1, D), lambda i, *_: (0, 0)),
                pl.BlockSpec((1, D), lambda i, *_: (0, 0)),
            ],
            out_specs=pl.BlockSpec((TM, D), lambda i, *_: (i, 0)),
            scratch_shapes=[pltpu.VMEM((2, 2, TM * PK, LANES), u32), pltpu.SemaphoreType.DMA((2,))],
        ),
        out_shape=jax.ShapeDtypeStruct((ntiles_out * TM, D), f32),
        compiler_params=_cparams(),
        name="moe_combine_ln",
    )(dest, ys2d, h, gcol, ln_g, ln_b)


def _moe_experts(layer, hp2d, eidx, rank, cnt, w1, w3, w2):
    dest, tile_expert, ntiles, zstart, zflag, first, slot, nxt, nsub = _plan(eidx, rank, cnt)
    xs = _scatter(dest, zstart, zflag, hp2d.reshape(TP, PK, LANES))
    ys2d = _experts(layer, tile_expert, ntiles, first, slot, nxt, nsub, xs.reshape(NS * PK, LANES), w1, w3, w2)
    return dest, ys2d


NT_DIMS = (((1,), (1,)), ((), ()))


def _rope_rows(t, cos, sa, sb):
    w = t.shape[1]
    reps = w // LANES
    c = jnp.tile(cos, (1, reps))
    a = jnp.tile(sa, (1, reps))
    b = jnp.tile(sb, (1, reps))
    return t * c + pltpu.roll(t, w - ROT // 2, 1) * a + pltpu.roll(t, ROT // 2, 1) * b


def _combine_qkv_kernel(n, dest_ref, ys_hbm, h_ref, gcol_ref, g_ref, b_ref,
                        wq_hbm, bq_ref, wk_ref, bk_ref, wvT_ref, bv_ref, cosT_ref, sinT_ref, cos_ref, sa_ref, sb_ref,
                        h2_ref, qT_ref, k_ref, vT_ref, buf, gsem, wq_ref, stage, wsem):
    i = pl.program_id(0)
    slot = lax.rem(i, 2)

    def start_row(tile, s, r, k):
        d = pl.multiple_of(dest_ref[tile * (2 * TM) + k * TM + r] * PK, PK)
        pltpu.make_async_copy(ys_hbm.at[pl.ds(d, PK)], buf.at[s, k, pl.ds(pl.multiple_of(r * PK, PK), PK)],
                              gsem.at[s]).start(priority=k)

    def drain(s):
        def one(r, carry):
            pltpu.make_async_copy(ys_hbm.at[pl.ds(0, PK)], buf.at[s, 0, pl.ds(0, PK)], gsem.at[s]).wait()
            return carry
        lax.fori_loop(0, 2 * TM, one, 0, unroll=ISSUE_UNROLL)

    @pl.when(i == 0)
    def _():
        _load_weights_bf16(wq_hbm, wq_ref, stage, wsem)

        def row(r, carry):
            for k in range(2):
                start_row(0, 0, r, k)
            return carry
        lax.fori_loop(0, TM, row, 0, unroll=ISSUE_UNROLL)

    drain(slot)
    lo0, hi0 = _unpack_rows(buf.at[slot, 0], TM, f32)
    lo1, hi1 = _unpack_rows(buf.at[slot, 1], TM, f32)
    g0, g1 = gcol_ref[:, 0:1], gcol_ref[:, 1:2]
    ffn = jnp.concatenate([lo0 * g0 + lo1 * g1, hi0 * g0 + hi1 * g1], axis=1)
    h2 = _layer_norm(ALPHA * h_ref[...] + ffn, g_ref[...], b_ref[...])
    h2_ref[...] = h2
    hb = h2.astype(bf16)

    nxt_tile = jnp.minimum(i + 1, n - 1)
    for r in range(TM):
        for k in range(2):
            start_row(nxt_tile, 1 - slot, r, k)
    scale = LOG2E / math.sqrt(HEAD_DIM)
    half = ROT // 2
    cosT = cosT_ref[...][None]
    sinT = sinT_ref[...][None]
    rows = GQA * HEAD_DIM
    for c in range(D // rows):
        lo, hi = c * rows, (c + 1) * rows
        t = lax.dot_general(wq_ref[:, lo:hi], hb, (((0,), (1,)), ((), ())), preferred_element_type=f32) \
            + bq_ref[lo:hi, :]
        t3 = t.reshape(GQA, HEAD_DIM, TM)
        x1, x2 = t3[:, 0:half, :], t3[:, half:ROT, :]
        r = jnp.concatenate([x1 * cosT - x2 * sinT, x2 * cosT + x1 * sinT, t3[:, ROT:, :]], axis=1)
        qT_ref[lo:hi, :] = (r * scale).reshape(rows, TM).astype(bf16)
    t = jnp.dot(hb, wk_ref[...], preferred_element_type=f32) + bk_ref[...]
    k_ref[...] = _rope_rows(t, cos_ref[...], sa_ref[...], sb_ref[...]).astype(bf16)
    t = lax.dot_general(wvT_ref[...], hb, NT_DIMS, preferred_element_type=f32) + bv_ref[...]
    vT_ref[...] = t.astype(bf16)

    @pl.when(i == n - 1)
    def _():
        drain(1 - slot)


def _combine_qkv(dest, ys2d, h, gcol, ln_g, ln_b, wq, bq_col, wk, bk, wvT, bv_col, tables):
    cosT, sinT, cos_t, sa_t, sb_t = tables
    const = lambda shape: pl.BlockSpec(shape, lambda i, *_: (0,) * len(shape))
    tabT = pl.BlockSpec((ROT // 2, TM), lambda i, *_: (0, i))
    tab = pl.BlockSpec((TM, LANES), lambda i, *_: (i, 0))
    return pl.pallas_call(
        functools.partial(_combine_qkv_kernel, NT),
        grid_spec=pltpu.PrefetchScalarGridSpec(
            num_scalar_prefetch=1,
            grid=(NT,),
            in_specs=[
                HBM,
                pl.BlockSpec((TM, D), lambda i, *_: (i, 0)),
                pl.BlockSpec((TM, LANES), lambda i, *_: (i, 0)),
                const((1, D)), const((1, D)),
                HBM, const((D, 1)),
                const((D, KVW)), const((1, KVW)),
                const((KVW, D)), const((KVW, 1)),
                tabT, tabT, tab, tab, tab,
            ],
            out_specs=[
                pl.BlockSpec((TM, D), lambda i, *_: (i, 0)),
                pl.BlockSpec((D, TM), lambda i, *_: (0, i)),
                pl.BlockSpec((TM, KVW), lambda i, *_: (i, 0)),
                pl.BlockSpec((KVW, TM), lambda i, *_: (0, i)),
            ],
            scratch_shapes=[pltpu.VMEM((2, 2, TM * PK, LANES), u32), pltpu.SemaphoreType.DMA((2,))]
            + _weight_scratch(D, D, 512),
        ),
        out_shape=[
            jax.ShapeDtypeStruct((TP, D), f32),
            jax.ShapeDtypeStruct((D, TP), bf16),
            jax.ShapeDtypeStruct((TP, KVW), bf16),
            jax.ShapeDtypeStruct((KVW, TP), bf16),
        ],
        compiler_params=_cparams(),
        name="moe_combine_ln_qkv_rope",
    )(dest, ys2d, h, gcol, ln_g, ln_b, wq, bq_col, wk, bk, wvT, bv_col, cosT, sinT, cos_t, sa_t, sb_t)


NKEY = 2 * QB + N_META
HC = 8
LW = HC * QB
SUB = 8


def _col_max(s):
    parts = [s[r * SUB:(r + 1) * SUB] for r in range(NKEY // SUB)]
    while len(parts) > 1:
        nxt = [jnp.maximum(parts[j], parts[j + 1]) for j in range(0, len(parts) - 1, 2)]
        if len(parts) % 2:
            nxt.append(parts[-1])
        parts = nxt
    return jnp.max(parts[0], axis=0, keepdims=True)


def _attn_kernel(qT_ref, kc_ref, kp_ref, km_ref, vTc_ref, vTp_ref, vTm_ref, sink_ref, oT_ref):
    i = pl.program_id(0)
    is_meta = i == NT - 1
    ck = lax.broadcasted_iota(i32, (NKEY, QB), 0)
    rq = lax.broadcasted_iota(i32, (NKEY, QB), 1)
    in_band = jnp.logical_and(ck > rq, ck <= rq + QB)
    meta_ok = jnp.logical_and(ck >= 2 * QB, jnp.logical_or(jnp.logical_not(is_meta), ck - 2 * QB <= rq))
    ones = jnp.ones((SUB, NKEY), bf16)

    for blk in range(TM // QB):
        lo = jnp.where(is_meta, 2 * QB, jnp.where(jnp.logical_and(i == 0, blk == 0), QB, 0))
        valid = jnp.logical_or(meta_ok, jnp.logical_and(in_band, ck >= lo))
        bias = jnp.where(valid, 0.0, -jnp.inf)
        bias = jnp.concatenate([bias] * HC, axis=1)
        c0 = blk * QB
        for g in range(N_KV):
            gs = slice(g * HEAD_DIM, (g + 1) * HEAD_DIM)
            if blk == 0:
                kprev, vprevT = kp_ref[:, gs], vTp_ref[gs, :]
            else:
                kprev, vprevT = kc_ref[c0 - QB:c0, gs], vTc_ref[gs, c0 - QB:c0]
            kcat = jnp.concatenate([kprev, kc_ref[c0:c0 + QB, gs], km_ref[:, gs]], axis=0)
            vcatT = jnp.concatenate([vprevT, vTc_ref[gs, c0:c0 + QB], vTm_ref[gs, 0:N_META]], axis=1)
            vext = jnp.concatenate([vcatT, ones], axis=0)
            for c in range(GQA // HC):
                h0 = g * GQA + c * HC
                heads = [qT_ref[(h0 + j) * HEAD_DIM:(h0 + j + 1) * HEAD_DIM, c0:c0 + QB] for j in range(HC)]
                s = jnp.dot(kcat, jnp.concatenate(heads, axis=1), preferred_element_type=f32) + bias
                sink = sink_ref[h0 // HC:h0 // HC + 1, :]
                m = jnp.maximum(_col_max(s), sink)
                p = jnp.exp2(s - m).astype(bf16)
                oe = jnp.dot(vext, p, preferred_element_type=f32)
                den = oe[HEAD_DIM:HEAD_DIM + 1, :] + jnp.exp2(sink - m)
                o = (oe[0:HEAD_DIM, :] * (1.0 / den)).astype(bf16)
                for j in range(HC):
                    oT_ref[(h0 + j) * HEAD_DIM:(h0 + j + 1) * HEAD_DIM, c0:c0 + QB] = o[:, j * QB:(j + 1) * QB]


def _attention(qT, k, vT, sink_lanes):
    prev_blk = lambda i: jnp.maximum(i * (TM // QB) - 1, 0)
    return pl.pallas_call(
        _attn_kernel,
        grid=(NT,),
        in_specs=[
            pl.BlockSpec((D, TM), lambda i: (0, i)),
            pl.BlockSpec((TM, KVW), lambda i: (i, 0)),
            pl.BlockSpec((QB, KVW), lambda i: (prev_blk(i), 0)),
            pl.BlockSpec((N_META, KVW), lambda i: (META_ROW // N_META, 0)),
            pl.BlockSpec((KVW, TM), lambda i: (0, i)),
            pl.BlockSpec((KVW, QB), lambda i: (0, prev_blk(i))),
            pl.BlockSpec((KVW, LANES), lambda i: (0, META_ROW // LANES)),
            _resident((N_HEADS // HC, LW)),
        ],
        out_specs=pl.BlockSpec((D, TM), lambda i: (0, i)),
        out_shape=jax.ShapeDtypeStruct((D, TP), bf16),
        compiler_params=_cparams(),
        name="swa_attention",
    )(qT, k, k, k, vT, vT, vT, sink_lanes)


def _router_weights(wg, bg, we, be):
    gap = R_EXP0 - N_GROUPS
    tail = LANES - R_EXP0 - N_EXP
    wr = jnp.concatenate([wg, jnp.zeros((D, gap), f32), we, jnp.zeros((D, tail), f32)], axis=1)
    br = jnp.concatenate([bg, jnp.zeros((gap,), f32), be, jnp.zeros((tail,), f32)]).reshape(1, LANES)
    return wr.astype(bf16), br


def _rope_tables():
    pos = np.concatenate([np.arange(SEQ) + N_META, np.arange(TM)]).astype(np.float32)
    half = ROT // 2
    inv_freq = (np.float32(ROPE_THETA) ** (-np.arange(0, ROT, 2, dtype=np.float32) / np.float32(ROT)))
    ang = pos[:, None] * inv_freq.astype(np.float32)[None, :]
    cos, sin = np.cos(ang).astype(np.float32), np.sin(ang).astype(np.float32)
    ones = np.ones((TP, HEAD_DIM - ROT), np.float32)
    zeros = np.zeros((TP, HEAD_DIM - ROT), np.float32)
    z8 = np.zeros((TP, half), np.float32)
    cos_h = np.concatenate([cos, cos, ones], axis=1)
    sa_h = np.concatenate([-sin, z8, zeros], axis=1)
    sb_h = np.concatenate([z8, sin, zeros], axis=1)
    rep = LANES // HEAD_DIM
    tabs = (cos.T, sin.T, np.tile(cos_h, (1, rep)), np.tile(sa_h, (1, rep)), np.tile(sb_h, (1, rep)))
    return tuple(jnp.asarray(np.ascontiguousarray(t)) for t in tabs)


def kernel(x, meta_tokens, conv_w_in, conv_b_in, conv_w_dw, conv_b_dw, conv_ln_g, conv_ln_b, conv_w_out,
           conv_b_out, w_k, b_k, w_v, b_v, w_q, b_q, w_o, b_o, sinks, ln_mix_g, ln_mix_b, ln_ffn_g, ln_ffn_b,
           router_group_w, router_group_b, router_expert_w, router_expert_b, expert_w1, expert_w3, expert_w2):
    assert x.shape == (1, SEQ, D)
    row = lambda v: v.reshape(1, -1)
    col = lambda v: v.reshape(-1, 1)
    x2d = x.reshape(SEQ, D)
    meta_pad = jnp.pad(meta_tokens.astype(f32), ((0, TM - N_META), (0, 0)))

    y = _glu(x2d, meta_pad, conv_w_in[0], row(conv_b_in[0]))
    w_dw = conv_w_dw[0].reshape(CONV_W, 2, PK, LANES).transpose(0, 2, 1, 3).reshape(CONV_W, CHUNKS, LANES)
    z = _conv(y, w_dw.astype(bf16), row(conv_b_dw[0]), row(conv_ln_g[0]),
              row(conv_ln_b[0]))
    wr, br = _router_weights(router_group_w[0], router_group_b[0], router_expert_w[0], router_expert_b[0])
    h, hp, eidx, rank, gcol, cnt = _proj_ln_route(
        z, (x2d, meta_pad), conv_w_out[0], row(conv_b_out[0]), row(ln_mix_g[0]), row(ln_mix_b[0]),
        wr, br)
    dest, ys2d = _moe_experts(0, hp, eidx, rank, cnt, expert_w1, expert_w3, expert_w2)

    h, qT, k, vT = _combine_qkv(dest, ys2d, h, gcol, row(ln_ffn_g[0]), row(ln_ffn_b[0]),
                                w_q[0], col(b_q[0]), w_k.astype(bf16), row(b_k),
                                w_v.T.astype(bf16), col(b_v), _rope_tables())
    sink_lanes = jnp.repeat((sinks[0].astype(f32) * LOG2E).reshape(N_HEADS // HC, HC), QB, axis=1)
    attT = _attention(qT, k, vT, sink_lanes)
    wr, br = _router_weights(router_group_w[1], router_group_b[1], router_expert_w[1], router_expert_b[1])
    h, hp, eidx, rank, gcol, cnt = _proj_ln_route(
        attT, h, w_o[0], row(b_o[0]), row(ln_mix_g[1]), row(ln_mix_b[1]), wr, br)
    dest, ys2d = _moe_experts(1, hp, eidx, rank, cnt, expert_w1, expert_w3, expert_w2)
    out = _combine(dest, ys2d, h, gcol, row(ln_ffn_g[1]), row(ln_ffn_b[1]), NXT)
    return out.reshape(1, SEQ, D)
```

```python
import functools
import math

import jax
import jax.numpy as jnp
import numpy as np
from jax import lax
from jax.experimental import pallas as pl
from jax.experimental.pallas import tpu as pltpu

f32 = jnp.float32
bf16 = jnp.bfloat16
i32 = jnp.int32
u32 = jnp.uint32

D = 2048
SEQ = 8192
DEPTH = 2
N_META = 16
CONV_W = 31
HEAD_DIM = 64
N_HEADS = 32
N_KV = 4
GQA = 8
KVW = N_KV * HEAD_DIM
WINDOW = 128
ROT = 16
ROPE_THETA = 500000.0
N_GROUPS = 4
EPG = 8
N_EXP = 32
FF = 256
ALPHA = (2.0 * DEPTH) ** 0.25
LN_EPS = 1e-5
LOG2E = math.log2(math.e)

LANES = 128
TM = 256
NXT = SEQ // TM
NT = NXT + 1
TP = NT * TM
META_ROW = SEQ
CHUNKS = D // LANES
HALO = 32
TME = 256
ESUB = 4
EBLK = ESUB * TME
NTE = (2 * TP) // EBLK + N_EXP
NS = NTE * EBLK
QB = 128
VMEM_LIMIT = 52 * 1024 * 1024


def _cparams():
    return pltpu.CompilerParams(dimension_semantics=("arbitrary",), vmem_limit_bytes=VMEM_LIMIT)


def _resident(shape):
    nd = len(shape)
    return pl.BlockSpec(shape, lambda *a: (0,) * nd, pipeline_mode=pl.Buffered(1))


def _layer_norm(x, g, b):
    mu = jnp.mean(x, axis=-1, keepdims=True)
    xc = x - mu
    var = jnp.mean(xc * xc, axis=-1, keepdims=True)
    return xc * lax.rsqrt(var + LN_EPS) * g + b


def _x_or_meta(i, x_ref, meta_ref):
    return jnp.where(i == NXT, meta_ref[...], x_ref[...])


HALF = D // 2
PK = HALF // LANES


def _pack_rows(v, out2d):
    rows = v.shape[0]
    bits = pltpu.bitcast(v.astype(bf16).astype(f32), u32)
    word = bits[:, HALF:] | lax.shift_right_logical(bits[:, :HALF], jnp.uint32(16))
    for s in range(PK):
        out2d[pl.ds(s, rows, stride=PK), :] = word[:, s * LANES:(s + 1) * LANES]


def _unpack_rows(in2d, rows, dtype):
    lo, hi = [], []
    for s in range(PK):
        w = in2d[pl.ds(s, rows, stride=PK), :]
        lo.append(pltpu.bitcast(lax.shift_left(w, jnp.uint32(16)), f32).astype(dtype))
        hi.append(pltpu.bitcast(w & jnp.uint32(0xFFFF0000), f32).astype(dtype))
    return jnp.concatenate(lo, axis=1), jnp.concatenate(hi, axis=1)


def _load_weights_bf16(w_hbm, w_vmem, stage, sem):
    rc = stage.shape[1]
    n = w_hbm.shape[0] // rc

    def cp(c):
        return pltpu.make_async_copy(w_hbm.at[pl.ds(c * rc, rc)], stage.at[c % 2], sem.at[c % 2])

    cp(0).start()
    for c in range(n):
        cp(c).wait()
        if c + 1 < n:
            cp(c + 1).start()
        w_vmem[c * rc:(c + 1) * rc, :] = stage[c % 2].astype(bf16)


def _weight_scratch(rows, cols, chunk_rows):
    return [pltpu.VMEM((rows, cols), bf16), pltpu.VMEM((2, chunk_rows, cols), f32), pltpu.SemaphoreType.DMA((2,))]


HBM = pl.BlockSpec(memory_space=pl.ANY)


def _glu_kernel(x_ref, meta_ref, w_hbm, b_ref, yp_ref, ybuf, w_ref, stage, sem):
    i = pl.program_id(0)

    @pl.when(i == 0)
    def _():
        _load_weights_bf16(w_hbm, w_ref, stage, sem)

    xb = _x_or_meta(i, x_ref, meta_ref).astype(bf16)
    cw = 512
    for c in range(D // cw):
        lo, hi = c * cw, (c + 1) * cw
        a = jnp.dot(xb, w_ref[:, lo:hi], preferred_element_type=f32) + b_ref[:, lo:hi]
        g = jnp.dot(xb, w_ref[:, D + lo:D + hi], preferred_element_type=f32) + b_ref[:, D + lo:D + hi]
        ybuf[:, lo:hi] = a * jax.nn.sigmoid(g)
    _pack_rows(ybuf[...], yp_ref)


def _glu(x2d, meta_pad, w_in, b_in):
    return pl.pallas_call(
        _glu_kernel,
        grid=(NT,),
        in_specs=[
            pl.BlockSpec((TM, D), lambda i: (jnp.minimum(i, NXT - 1), 0)),
            _resident((TM, D)),
            HBM,
            _resident((1, 2 * D)),
        ],
        out_specs=pl.BlockSpec((TM * PK, LANES), lambda i: (i, 0)),
        out_shape=jax.ShapeDtypeStruct((TP * PK, LANES), u32),
        scratch_shapes=[pltpu.VMEM((TM, D), f32)] + _weight_scratch(D, 2 * D, 256),
        compiler_params=_cparams(),
        name="glu",
    )(x2d, meta_pad, w_in, b_in)


def _chunk_row(c):
    return 2 * (c % PK) + c // PK


def _conv_kernel(yp_ref, ymeta_ref, w_ref, bdw_ref, g_ref, b_ref, z_ref, scr, accs):
    i = pl.program_id(0)
    hrows = HALO * PK
    trows = TM * PK

    @pl.when(i == 0)
    def _():
        scr[0:(HALO - N_META) * PK, :] = jnp.zeros(((HALO - N_META) * PK, LANES), u32)
        scr[(HALO - N_META) * PK:hrows, :] = ymeta_ref[...]

    @pl.when(i == NT - 1)
    def _():
        scr[0:hrows, :] = jnp.zeros((hrows, LANES), u32)

    @pl.when(jnp.logical_and(i > 0, i < NT - 1))
    def _():
        scr[0:hrows, :] = scr[trows:trows + hrows, :]

    scr[hrows:hrows + trows, :] = yp_ref[...]

    tb = 16
    first = (HALO - (CONV_W - 1)) * PK

    def block(t, carry):
        base = pl.multiple_of(t * (tb * PK), tb * PK)
        acc = jnp.zeros((tb, CHUNKS, LANES), f32)
        for j in range(CONV_W):
            words = scr[pl.ds(base + first + j * PK, tb * PK), :]
            sl = pltpu.bitcast(words, bf16).reshape(tb, CHUNKS, LANES)
            acc = acc + sl.astype(f32) * w_ref[j].astype(f32)[None]
        accs[pl.ds(pl.multiple_of(t * (tb * CHUNKS), tb * CHUNKS), tb * CHUNKS), :] = acc.reshape(tb * CHUNKS, LANES)
        return carry

    lax.fori_loop(0, TM // tb, block, 0)

    rb = 16

    def finish(t, carry):
        r0 = pl.multiple_of(t * rb, rb)
        cols = [accs[pl.ds(r0 * CHUNKS + _chunk_row(c), rb, stride=CHUNKS), :] for c in range(CHUNKS)]
        v = _layer_norm(jnp.concatenate(cols, axis=1) + bdw_ref[...], g_ref[...], b_ref[...])
        z_ref[pl.ds(r0, rb), :] = (v * jax.nn.sigmoid(v)).astype(bf16)
        return carry

    lax.fori_loop(0, TM // rb, finish, 0, unroll=8)


def _conv(yp, w_dw3, b_dw, ln_g, ln_b):
    return pl.pallas_call(
        _conv_kernel,
        grid=(NT,),
        in_specs=[
            pl.BlockSpec((TM * PK, LANES), lambda i: (i, 0)),
            pl.BlockSpec((N_META * PK, LANES), lambda i: (META_ROW // N_META, 0)),
            _resident((CONV_W, CHUNKS, LANES)),
            _resident((1, D)),
            _resident((1, D)),
            _resident((1, D)),
        ],
        out_specs=pl.BlockSpec((TM, D), lambda i: (i, 0)),
        out_shape=jax.ShapeDtypeStruct((TP, D), bf16),
        scratch_shapes=[
            pltpu.VMEM(((TM + HALO) * PK, LANES), u32),
            pltpu.VMEM((TM * CHUNKS, LANES), f32),
        ],
        compiler_params=_cparams(),
        name="conv_ln_swish",
    )(yp, yp, w_dw3, b_dw, ln_g, ln_b)


R_EXP0 = 8


def _route(logits, running, valid):
    lt = logits.T
    gl = [lt[k:k + 1, :] for k in range(N_GROUPS)]
    gm = functools.reduce(jnp.maximum, gl)
    gex = [jnp.exp(v - gm) for v in gl]
    gden = functools.reduce(lambda a, b: a + b, gex)
    gp = [v / gden for v in gex]
    best = gp[0]
    gi = jnp.zeros((1, TM), i32)
    for k in range(1, N_GROUPS):
        better = gp[k] > best
        gi = jnp.where(better, k, gi)
        best = jnp.where(better, gp[k], best)
    esel = lt[R_EXP0:R_EXP0 + EPG, :]
    for k in range(1, N_GROUPS):
        esel = jnp.where(gi == k, lt[R_EXP0 + EPG * k:R_EXP0 + EPG * (k + 1), :], esel)
    em = jnp.max(esel, axis=0, keepdims=True)
    eex = jnp.exp(esel - em)
    ep = eex / jnp.sum(eex, axis=0, keepdims=True)
    io8 = lax.broadcasted_iota(i32, (EPG, TM), 0)
    v1 = jnp.max(ep, axis=0, keepdims=True)
    i1 = jnp.min(jnp.where(ep == v1, io8, EPG), axis=0, keepdims=True)
    ep2 = jnp.where(io8 == i1, -1.0, ep)
    v2 = jnp.max(ep2, axis=0, keepdims=True)
    i2 = jnp.min(jnp.where(ep2 == v2, io8, EPG), axis=0, keepdims=True)
    s = v1 + v2
    gate0 = best * (v1 / s)
    gate1 = best * (v2 / s)
    f0 = gi * EPG + i1
    f1 = gi * EPG + i2

    io32 = lax.broadcasted_iota(i32, (N_EXP, TM), 0)
    oh0 = (io32 == f0).astype(f32)
    oh1 = (io32 == f1).astype(f32)
    cnt = oh0 + oh1
    upper = (lax.broadcasted_iota(i32, (TM, TM), 0) < lax.broadcasted_iota(i32, (TM, TM), 1))
    before = jnp.dot(cnt.astype(bf16), upper.astype(f32).astype(bf16), preferred_element_type=f32)
    base = running[...] + before
    r0 = jnp.sum(oh0 * base, axis=0, keepdims=True).astype(i32)
    r1 = jnp.sum(oh1 * base, axis=0, keepdims=True).astype(i32)
    running[...] = running[...] + valid * jnp.sum(cnt, axis=1, keepdims=True)

    io128 = lax.broadcasted_iota(i32, (LANES, TM), 0)
    gcol = jnp.where(io128 == 0, gate0, jnp.where(io128 == 1, gate1, 0.0)).T
    return f0, f1, r0, r1, gcol


def _proj_ln_route_kernel(first, a_ref, *refs):
    if first:
        x_ref, meta_ref, w_hbm, *rest = refs
    else:
        res_ref, w_hbm, *rest = refs
    (bias_ref, g_ref, b_ref, wr_ref, br_ref, h_ref, hp_ref, eidx_ref, rank_ref, gcol_ref, cnt_ref,
     running, accbuf, lgbuf, w_ref, stage, sem) = rest
    i = pl.program_id(0)

    @pl.when(i == 0)
    def _():
        running[...] = jnp.zeros_like(running)
        accbuf[1] = jnp.zeros((TM, D), f32)
        lgbuf[...] = jnp.zeros_like(lgbuf)
        _load_weights_bf16(w_hbm, w_ref, stage, sem)

    valid = (i >= 2).astype(f32)
    for parity in range(2):
        @pl.when(lax.rem(i, 2) == parity)
        def _():
            if first:
                accbuf[parity] = jnp.dot(a_ref[...], w_ref[...], preferred_element_type=f32)
                res = jnp.where(jnp.minimum(i - 1, NT - 1) == NXT, meta_ref[...], x_ref[...])
            else:
                accbuf[parity] = lax.dot_general(a_ref[...], w_ref[...], (((0,), (0,)), ((), ())),
                                                 preferred_element_type=f32)
                res = res_ref[...]
            h = _layer_norm(ALPHA * res + accbuf[1 - parity] + bias_ref[...], g_ref[...], b_ref[...])
            h_ref[...] = h
            _pack_rows(h, hp_ref)
            lgbuf[1 - parity] = jnp.dot(h.astype(bf16), wr_ref[...], preferred_element_type=f32) + br_ref[...]
            f0, f1, r0, r1, gcol = _route(lgbuf[parity], running, valid)
            eidx_ref[0, 0:1, :] = f0
            eidx_ref[0, 1:2, :] = f1
            rank_ref[0, 0:1, :] = r0
            rank_ref[0, 1:2, :] = r1
            gcol_ref[...] = gcol
            cnt_ref[...] = running[...]


def _proj_ln_route(a, res, w, bias, ln_g, ln_b, wr, br):
    first = isinstance(res, tuple)
    cur = lambda i: jnp.minimum(i, NT - 1)
    prv = lambda i: jnp.clip(i - 1, 0, NT - 1)
    rtd = lambda i: jnp.maximum(i - 2, 0)
    if first:
        a_spec = pl.BlockSpec((TM, D), lambda i: (cur(i), 0))
        res_specs = [pl.BlockSpec((TM, D), lambda i: (jnp.minimum(prv(i), NXT - 1), 0)), _resident((TM, D))]
        res_args = list(res)
    else:
        a_spec = pl.BlockSpec((D, TM), lambda i: (0, cur(i)))
        res_specs = [pl.BlockSpec((TM, D), lambda i: (prv(i), 0))]
        res_args = [res]
    tile3 = pl.BlockSpec((1, 2, TM), lambda i: (rtd(i), 0, 0))
    return pl.pallas_call(
        functools.partial(_proj_ln_route_kernel, first),
        grid=(NT + 2,),
        in_specs=[a_spec] + res_specs + [
            HBM, _resident((1, D)), _resident((1, D)), _resident((1, D)),
            _resident((D, LANES)), _resident((1, LANES)),
        ],
        out_specs=[
            pl.BlockSpec((TM, D), lambda i: (prv(i), 0)),
            pl.BlockSpec((TM * PK, LANES), lambda i: (prv(i), 0)),
            tile3, tile3,
            pl.BlockSpec((TM, LANES), lambda i: (rtd(i), 0)),
            pl.BlockSpec((N_EXP, TM), lambda i: (0, 0)),
        ],
        out_shape=[
            jax.ShapeDtypeStruct((TP, D), f32),
            jax.ShapeDtypeStruct((TP * PK, LANES), u32),
            jax.ShapeDtypeStruct((NT, 2, TM), i32),
            jax.ShapeDtypeStruct((NT, 2, TM), i32),
            jax.ShapeDtypeStruct((TP, LANES), f32),
            jax.ShapeDtypeStruct((N_EXP, TM), f32),
        ],
        scratch_shapes=[pltpu.VMEM((N_EXP, TM), f32), pltpu.VMEM((2, TM, D), f32), pltpu.VMEM((2, TM, LANES), f32)]
        + _weight_scratch(D, D, 512),
        compiler_params=_cparams(),
        name="proj_ln_route",
    )(a, *res_args, w, bias, ln_g, ln_b, wr, br)


def _plan(eidx, rank, cnt):
    counts = cnt[:, 0].astype(i32)
    used = ((counts + TME - 1) // TME) * TME
    padded = ((counts + EBLK - 1) // EBLK) * EBLK
    ends = jnp.cumsum(padded)
    offs = ends - padded
    ntiles = ends[-1] // EBLK
    off_of = jnp.sum(jnp.where(eidx[..., None] == jnp.arange(N_EXP, dtype=i32), offs, 0), axis=-1)
    dest = (off_of + rank).reshape(-1)
    tile_start = jnp.minimum(jnp.arange(NTE, dtype=i32), ntiles - 1) * EBLK
    tile_expert = jnp.minimum(jnp.sum(tile_start[:, None] >= ends[None, :], axis=1), N_EXP - 1).astype(i32)
    nsub = jnp.clip((offs[tile_expert] + used[tile_expert] - tile_start) // TME, 0, ESUB).astype(i32)
    zstart = jnp.where(used > 0, offs + used - TME, 0).astype(i32)
    zflag = (used > 0).astype(i32)
    tid = jnp.arange(NTE, dtype=i32)
    live = tid < ntiles
    first = jnp.logical_and(live, jnp.logical_or(tid == 0, tile_expert != jnp.roll(tile_expert, 1)))
    slot = (jnp.cumsum(first.astype(i32)) - 1) % 2
    nxt_first = lax.cummin(jnp.where(first, tid, NTE), reverse=True)
    after = jnp.concatenate([nxt_first[1:], jnp.full((1,), NTE, i32)])
    nxt = jnp.where(after < NTE, tile_expert[jnp.minimum(after, NTE - 1)], -1)
    return (dest.astype(i32), tile_expert, ntiles.reshape(1).astype(i32), zstart, zflag,
            first.astype(i32), slot.astype(i32), nxt.astype(i32), nsub)


ISSUE_UNROLL = 8


def _scatter_kernel(dest_ref, zstart_ref, zflag_ref, hp_ref, xs_hbm, zeros, sem, zsem):
    i = pl.program_id(0)

    @pl.when(i == 0)
    def _():
        zeros[...] = jnp.zeros_like(zeros)
        def fill(e):
            start = pl.multiple_of(zstart_ref[e], TME)
            return pltpu.make_async_copy(zeros, xs_hbm.at[pl.ds(start, TME)], zsem)

        for e in range(N_EXP):
            @pl.when(zflag_ref[e] > 0)
            def _():
                fill(e).start()
        for e in range(N_EXP):
            @pl.when(zflag_ref[e] > 0)
            def _():
                fill(e).wait()

    base = i * (2 * TM)

    def row(r, carry):
        for k in range(2):
            d = dest_ref[base + k * TM + r]
            pltpu.make_async_copy(hp_ref.at[pl.ds(r, 1)], xs_hbm.at[pl.ds(d, 1)], sem).start(priority=k)
        return carry

    lax.fori_loop(0, TM, row, 0, unroll=ISSUE_UNROLL)

    def drain(r, carry):
        pltpu.make_async_copy(hp_ref.at[pl.ds(0, 1)], xs_hbm.at[pl.ds(0, 1)], sem).wait()
        return carry

    lax.fori_loop(0, 2 * TM, drain, 0, unroll=ISSUE_UNROLL)


def _scatter(dest, zstart, zflag, hp3):
    return pl.pallas_call(
        _scatter_kernel,
        grid_spec=pltpu.PrefetchScalarGridSpec(
            num_scalar_prefetch=3,
            grid=(NT,),
            in_specs=[pl.BlockSpec((TM, PK, LANES), lambda i, *_: (i, 0, 0))],
            out_specs=pl.BlockSpec(memory_space=pl.ANY),
            scratch_shapes=[pltpu.VMEM((TME, PK, LANES), u32), pltpu.SemaphoreType.DMA(()),
                            pltpu.SemaphoreType.DMA(())],
        ),
        out_shape=jax.ShapeDtypeStruct((NS, PK, LANES), u32),
        compiler_params=pltpu.CompilerParams(dimension_semantics=("arbitrary",), vmem_limit_bytes=VMEM_LIMIT,
                                             has_side_effects=True),
        name="moe_scatter",
    )(dest, zstart, zflag, hp3)


def _expert_kernel(layer, te_ref, nt_ref, first_ref, slot_ref, nxt_ref, nsub_ref, xs_ref, w1_hbm, w3_hbm, w2_hbm,
                   ys_ref, wb1, wb3, wb2, w1c, w3c, w2c, sem):
    i = pl.program_id(0)

    def copies(e, s):
        return (pltpu.make_async_copy(w1_hbm.at[layer, e], wb1.at[s], sem.at[s]),
                pltpu.make_async_copy(w3_hbm.at[layer, e], wb3.at[s], sem.at[s]),
                pltpu.make_async_copy(w2_hbm.at[layer, e], wb2.at[s], sem.at[s]))

    @pl.when(i == 0)
    def _():
        for cp in copies(te_ref[0], 0):
            cp.start()

    @pl.when(jnp.logical_and(i < nt_ref[0], first_ref[i] > 0))
    def _():
        s = slot_ref[i]
        for cp in copies(te_ref[i], s):
            cp.wait()

        @pl.when(nxt_ref[i] >= 0)
        def _():
            for cp in copies(nxt_ref[i], 1 - s):
                cp.start()

        w1c[...] = wb1[s].astype(bf16)
        w3c[...] = wb3[s].astype(bf16)
        w2c[...] = wb2[s].astype(bf16)

    for sub in range(ESUB):
        @pl.when(jnp.logical_and(i < nt_ref[0], sub < nsub_ref[i]))
        def _():
            rows = pl.ds(sub * TME * PK, TME * PK)
            xlo, xhi = _unpack_rows(xs_ref.at[rows], TME, bf16)
            a = (jnp.dot(xlo, w1c[0:HALF, :], preferred_element_type=f32)
                 + jnp.dot(xhi, w1c[HALF:D, :], preferred_element_type=f32))
            b = (jnp.dot(xlo, w3c[0:HALF, :], preferred_element_type=f32)
                 + jnp.dot(xhi, w3c[HALF:D, :], preferred_element_type=f32))
            hid = (a * jax.nn.sigmoid(a) * b).astype(bf16)
            _pack_rows(jnp.dot(hid, w2c[...], preferred_element_type=f32), ys_ref.at[rows])


def _experts(layer, tile_expert, ntiles, first, slot, nxt, nsub, xs2d, w1, w3, w2):
    def row_map(i, te, nt, *_):
        return (jnp.minimum(i, nt[0] - 1), 0)

    hbm = pl.BlockSpec(memory_space=pl.ANY)
    return pl.pallas_call(
        functools.partial(_expert_kernel, layer),
        grid_spec=pltpu.PrefetchScalarGridSpec(
            num_scalar_prefetch=6,
            grid=(NTE,),
            in_specs=[pl.BlockSpec((EBLK * PK, LANES), row_map), hbm, hbm, hbm],
            out_specs=pl.BlockSpec((EBLK * PK, LANES), row_map),
            scratch_shapes=[
                pltpu.VMEM((2, D, FF), f32), pltpu.VMEM((2, D, FF), f32), pltpu.VMEM((2, FF, D), f32),
                pltpu.VMEM((D, FF), bf16), pltpu.VMEM((D, FF), bf16), pltpu.VMEM((FF, D), bf16),
                pltpu.SemaphoreType.DMA((2,)),
            ],
        ),
        out_shape=jax.ShapeDtypeStruct((NS * PK, LANES), u32),
        compiler_params=_cparams(),
        name="moe_experts",
    )(tile_expert, ntiles, first, slot, nxt, nsub, xs2d, w1, w3, w2)


def _gather_combine(n, dest_ref, ys_hbm, h_ref, gcol_ref, g_ref, b_ref, buf, sem):
    i = pl.program_id(0)
    slot = lax.rem(i, 2)

    def issue(tile, s):
        base = tile * (2 * TM)

        def row(r, carry):
            for k in range(2):
                d = pl.multiple_of(dest_ref[base + k * TM + r] * PK, PK)
                pltpu.make_async_copy(ys_hbm.at[pl.ds(d, PK)],
                                      buf.at[s, k, pl.ds(pl.multiple_of(r * PK, PK), PK)],
                                      sem.at[s]).start(priority=k)
            return carry

        lax.fori_loop(0, TM, row, 0, unroll=ISSUE_UNROLL)

    @pl.when(i == 0)
    def _():
        issue(0, 0)

    @pl.when(i + 1 < n)
    def _():
        issue(i + 1, 1 - slot)

    def drain(r, carry):
        pltpu.make_async_copy(ys_hbm.at[pl.ds(0, PK)], buf.at[slot, 0, pl.ds(0, PK)], sem.at[slot]).wait()
        return carry

    lax.fori_loop(0, 2 * TM, drain, 0, unroll=ISSUE_UNROLL)

    lo0, hi0 = _unpack_rows(buf.at[slot, 0], TM, f32)
    lo1, hi1 = _unpack_rows(buf.at[slot, 1], TM, f32)
    g0, g1 = gcol_ref[:, 0:1], gcol_ref[:, 1:2]
    ffn = jnp.concatenate([lo0 * g0 + lo1 * g1, hi0 * g0 + hi1 * g1], axis=1)
    return _layer_norm(ALPHA * h_ref[...] + ffn, g_ref[...], b_ref[...])


def _combine_kernel(n, dest_ref, ys_hbm, h_ref, gcol_ref, g_ref, b_ref, o_ref, buf, sem):
    o_ref[...] = _gather_combine(n, dest_ref, ys_hbm, h_ref, gcol_ref, g_ref, b_ref, buf, sem)


def _combine(dest, ys2d, h, gcol, ln_g, ln_b, ntiles_out):
    return pl.pallas_call(
        functools.partial(_combine_kernel, ntiles_out),
        grid_spec=pltpu.PrefetchScalarGridSpec(
            num_scalar_prefetch=1,
            grid=(ntiles_out,),
            in_specs=[
                pl.BlockSpec(memory_space=pl.ANY),
                pl.BlockSpec((TM, D), lambda i, *_: (i, 0)),
                pl.BlockSpec((TM, LANES), lambda i, *_: (i, 0)),
                pl.BlockSpec((1, D), lambda i, *_: (0, 0)),
                pl.BlockSpec((1, D), lambda i, *_: (0, 0)),
            ],
            out_specs=pl.BlockSpec((TM, D), lambda i, *_: (i, 0)),
            scratch_shapes=[pltpu.VMEM((2, 2, TM * PK, LANES), u32), pltpu.SemaphoreType.DMA((2,))],
        ),
        out_shape=jax.ShapeDtypeStruct((ntiles_out * TM, D), f32),
        compiler_params=_cparams(),
        name="moe_combine_ln",
    )(dest, ys2d, h, gcol, ln_g, ln_b)


def _moe_experts(layer, hp2d, eidx, rank, cnt, w1, w3, w2):
    dest, tile_expert, ntiles, zstart, zflag, first, slot, nxt, nsub = _plan(eidx, rank, cnt)
    xs = _scatter(dest, zstart, zflag, hp2d.reshape(TP, PK, LANES))
    ys2d = _experts(layer, tile_expert, ntiles, first, slot, nxt, nsub, xs.reshape(NS * PK, LANES), w1, w3, w2)
    return dest, ys2d


NT_DIMS = (((1,), (1,)), ((), ()))


def _rope_rows(t, cos, sa, sb):
    w = t.shape[1]
    reps = w // LANES
    c = jnp.tile(cos, (1, reps))
    a = jnp.tile(sa, (1, reps))
    b = jnp.tile(sb, (1, reps))
    return t * c + pltpu.roll(t, w - ROT // 2, 1) * a + pltpu.roll(t, ROT // 2, 1) * b


def _combine_qkv_kernel(n, dest_ref, ys_hbm, h_ref, gcol_ref, g_ref, b_ref,
                        wq_hbm, bq_ref, wk_ref, bk_ref, wvT_ref, bv_ref, cosT_ref, sinT_ref, cos_ref, sa_ref, sb_ref,
                        h2_ref, qT_ref, k_ref, vT_ref, buf, gsem, wq_ref, stage, wsem):
    @pl.when(pl.program_id(0) == 0)
    def _():
        _load_weights_bf16(wq_hbm, wq_ref, stage, wsem)

    h2 = _gather_combine(n, dest_ref, ys_hbm, h_ref, gcol_ref, g_ref, b_ref, buf, gsem)
    h2_ref[...] = h2
    hb = h2.astype(bf16)
    scale = LOG2E / math.sqrt(HEAD_DIM)
    half = ROT // 2
    cosT = cosT_ref[...][None]
    sinT = sinT_ref[...][None]
    rows = GQA * HEAD_DIM
    for c in range(D // rows):
        lo, hi = c * rows, (c + 1) * rows
        t = lax.dot_general(wq_ref[:, lo:hi], hb, (((0,), (1,)), ((), ())), preferred_element_type=f32) \
            + bq_ref[lo:hi, :]
        t3 = t.reshape(GQA, HEAD_DIM, TM)
        x1, x2 = t3[:, 0:half, :], t3[:, half:ROT, :]
        r = jnp.concatenate([x1 * cosT - x2 * sinT, x2 * cosT + x1 * sinT, t3[:, ROT:, :]], axis=1)
        qT_ref[lo:hi, :] = (r * scale).reshape(rows, TM).astype(bf16)
    t = jnp.dot(hb, wk_ref[...], preferred_element_type=f32) + bk_ref[...]
    k_ref[...] = _rope_rows(t, cos_ref[...], sa_ref[...], sb_ref[...]).astype(bf16)
    t = lax.dot_general(wvT_ref[...], hb, NT_DIMS, preferred_element_type=f32) + bv_ref[...]
    vT_ref[...] = t.astype(bf16)


def _combine_qkv(dest, ys2d, h, gcol, ln_g, ln_b, wq, bq_col, wk, bk, wvT, bv_col, tables):
    cosT, sinT, cos_t, sa_t, sb_t = tables
    const = lambda shape: pl.BlockSpec(shape, lambda i, *_: (0,) * len(shape))
    tabT = pl.BlockSpec((ROT // 2, TM), lambda i, *_: (0, i))
    tab = pl.BlockSpec((TM, LANES), lambda i, *_: (i, 0))
    return pl.pallas_call(
        functools.partial(_combine_qkv_kernel, NT),
        grid_spec=pltpu.PrefetchScalarGridSpec(
            num_scalar_prefetch=1,
            grid=(NT,),
            in_specs=[
                HBM,
                pl.BlockSpec((TM, D), lambda i, *_: (i, 0)),
                pl.BlockSpec((TM, LANES), lambda i, *_: (i, 0)),
                const((1, D)), const((1, D)),
                HBM, const((D, 1)),
                const((D, KVW)), const((1, KVW)),
                const((KVW, D)), const((KVW, 1)),
                tabT, tabT, tab, tab, tab,
            ],
            out_specs=[
                pl.BlockSpec((TM, D), lambda i, *_: (i, 0)),
                pl.BlockSpec((D, TM), lambda i, *_: (0, i)),
                pl.BlockSpec((TM, KVW), lambda i, *_: (i, 0)),
                pl.BlockSpec((KVW, TM), lambda i, *_: (0, i)),
            ],
            scratch_shapes=[pltpu.VMEM((2, 2, TM * PK, LANES), u32), pltpu.SemaphoreType.DMA((2,))]
            + _weight_scratch(D, D, 512),
        ),
        out_shape=[
            jax.ShapeDtypeStruct((TP, D), f32),
            jax.ShapeDtypeStruct((D, TP), bf16),
            jax.ShapeDtypeStruct((TP, KVW), bf16),
            jax.ShapeDtypeStruct((KVW, TP), bf16),
        ],
        compiler_params=_cparams(),
        name="moe_combine_ln_qkv_rope",
    )(dest, ys2d, h, gcol, ln_g, ln_b, wq, bq_col, wk, bk, wvT, bv_col, cosT, sinT, cos_t, sa_t, sb_t)


NKEY = 2 * QB + N_META
HC = 8
LW = HC * QB
SUB = 8


def _col_max(s):
    parts = [s[r * SUB:(r + 1) * SUB] for r in range(NKEY // SUB)]
    while len(parts) > 1:
        nxt = [jnp.maximum(parts[j], parts[j + 1]) for j in range(0, len(parts) - 1, 2)]
        if len(parts) % 2:
            nxt.append(parts[-1])
        parts = nxt
    return jnp.max(parts[0], axis=0, keepdims=True)


def _attn_kernel(qT_ref, kc_ref, kp_ref, km_ref, vTc_ref, vTp_ref, vTm_ref, sink_ref, oT_ref):
    i = pl.program_id(0)
    is_meta = i == NT - 1
    ck = lax.broadcasted_iota(i32, (NKEY, QB), 0)
    rq = lax.broadcasted_iota(i32, (NKEY, QB), 1)
    in_band = jnp.logical_and(ck > rq, ck <= rq + QB)
    meta_ok = jnp.logical_and(ck >= 2 * QB, jnp.logical_or(jnp.logical_not(is_meta), ck - 2 * QB <= rq))
    ones = jnp.ones((SUB, NKEY), bf16)

    for blk in range(TM // QB):
        lo = jnp.where(is_meta, 2 * QB, jnp.where(jnp.logical_and(i == 0, blk == 0), QB, 0))
        valid = jnp.logical_or(meta_ok, jnp.logical_and(in_band, ck >= lo))
        bias = jnp.where(valid, 0.0, -jnp.inf)
        bias = jnp.concatenate([bias] * HC, axis=1)
        c0 = blk * QB
        for g in range(N_KV):
            gs = slice(g * HEAD_DIM, (g + 1) * HEAD_DIM)
            if blk == 0:
                kprev, vprevT = kp_ref[:, gs], vTp_ref[gs, :]
            else:
                kprev, vprevT = kc_ref[c0 - QB:c0, gs], vTc_ref[gs, c0 - QB:c0]
            kcat = jnp.concatenate([kprev, kc_ref[c0:c0 + QB, gs], km_ref[:, gs]], axis=0)
            vcatT = jnp.concatenate([vprevT, vTc_ref[gs, c0:c0 + QB], vTm_ref[gs, 0:N_META]], axis=1)
            vext = jnp.concatenate([vcatT, ones], axis=0)
            for c in range(GQA // HC):
                h0 = g * GQA + c * HC
                heads = [qT_ref[(h0 + j) * HEAD_DIM:(h0 + j + 1) * HEAD_DIM, c0:c0 + QB] for j in range(HC)]
                s = jnp.dot(kcat, jnp.concatenate(heads, axis=1), preferred_element_type=f32) + bias
                sink = sink_ref[h0 // HC:h0 // HC + 1, :]
                m = jnp.maximum(_col_max(s), sink)
                p = jnp.exp2(s - m).astype(bf16)
                oe = jnp.dot(vext, p, preferred_element_type=f32)
                den = oe[HEAD_DIM:HEAD_DIM + 1, :] + jnp.exp2(sink - m)
                o = (oe[0:HEAD_DIM, :] * (1.0 / den)).astype(bf16)
                for j in range(HC):
                    oT_ref[(h0 + j) * HEAD_DIM:(h0 + j + 1) * HEAD_DIM, c0:c0 + QB] = o[:, j * QB:(j + 1) * QB]


def _attention(qT, k, vT, sink_lanes):
    prev_blk = lambda i: jnp.maximum(i * (TM // QB) - 1, 0)
    return pl.pallas_call(
        _attn_kernel,
        grid=(NT,),
        in_specs=[
            pl.BlockSpec((D, TM), lambda i: (0, i)),
            pl.BlockSpec((TM, KVW), lambda i: (i, 0)),
            pl.BlockSpec((QB, KVW), lambda i: (prev_blk(i), 0)),
            pl.BlockSpec((N_META, KVW), lambda i: (META_ROW // N_META, 0)),
            pl.BlockSpec((KVW, TM), lambda i: (0, i)),
            pl.BlockSpec((KVW, QB), lambda i: (0, prev_blk(i))),
            pl.BlockSpec((KVW, LANES), lambda i: (0, META_ROW // LANES)),
            _resident((N_HEADS // HC, LW)),
        ],
        out_specs=pl.BlockSpec((D, TM), lambda i: (0, i)),
        out_shape=jax.ShapeDtypeStruct((D, TP), bf16),
        compiler_params=_cparams(),
        name="swa_attention",
    )(qT, k, k, k, vT, vT, vT, sink_lanes)


def _router_weights(wg, bg, we, be):
    gap = R_EXP0 - N_GROUPS
    tail = LANES - R_EXP0 - N_EXP
    wr = jnp.concatenate([wg, jnp.zeros((D, gap), f32), we, jnp.zeros((D, tail), f32)], axis=1)
    br = jnp.concatenate([bg, jnp.zeros((gap,), f32), be, jnp.zeros((tail,), f32)]).reshape(1, LANES)
    return wr.astype(bf16), br


def _rope_tables():
    pos = np.concatenate([np.arange(SEQ) + N_META, np.arange(TM)]).astype(np.float32)
    half = ROT // 2
    inv_freq = (np.float32(ROPE_THETA) ** (-np.arange(0, ROT, 2, dtype=np.float32) / np.float32(ROT)))
    ang = pos[:, None] * inv_freq.astype(np.float32)[None, :]
    cos, sin = np.cos(ang).astype(np.float32), np.sin(ang).astype(np.float32)
    ones = np.ones((TP, HEAD_DIM - ROT), np.float32)
    zeros = np.zeros((TP, HEAD_DIM - ROT), np.float32)
    z8 = np.zeros((TP, half), np.float32)
    cos_h = np.concatenate([cos, cos, ones], axis=1)
    sa_h = np.concatenate([-sin, z8, zeros], axis=1)
    sb_h = np.concatenate([z8, sin, zeros], axis=1)
    rep = LANES // HEAD_DIM
    tabs = (cos.T, sin.T, np.tile(cos_h, (1, rep)), np.tile(sa_h, (1, rep)), np.tile(sb_h, (1, rep)))
    return tuple(jnp.asarray(np.ascontiguousarray(t)) for t in tabs)


def kernel(x, meta_tokens, conv_w_in, conv_b_in, conv_w_dw, conv_b_dw, conv_ln_g, conv_ln_b, conv_w_out,
           conv_b_out, w_k, b_k, w_v, b_v, w_q, b_q, w_o, b_o, sinks, ln_mix_g, ln_mix_b, ln_ffn_g, ln_ffn_b,
           router_group_w, router_group_b, router_expert_w, router_expert_b, expert_w1, expert_w3, expert_w2):
    assert x.shape == (1, SEQ, D)
    row = lambda v: v.reshape(1, -1)
    col = lambda v: v.reshape(-1, 1)
    x2d = x.reshape(SEQ, D)
    meta_pad = jnp.pad(meta_tokens.astype(f32), ((0, TM - N_META), (0, 0)))

    y = _glu(x2d, meta_pad, conv_w_in[0], row(conv_b_in[0]))
    w_dw = conv_w_dw[0].reshape(CONV_W, 2, PK, LANES).transpose(0, 2, 1, 3).reshape(CONV_W, CHUNKS, LANES)
    z = _conv(y, w_dw.astype(bf16), row(conv_b_dw[0]), row(conv_ln_g[0]),
              row(conv_ln_b[0]))
    wr, br = _router_weights(router_group_w[0], router_group_b[0], router_expert_w[0], router_expert_b[0])
    h, hp, eidx, rank, gcol, cnt = _proj_ln_route(
        z, (x2d, meta_pad), conv_w_out[0], row(conv_b_out[0]), row(ln_mix_g[0]), row(ln_mix_b[0]),
        wr, br)
    dest, ys2d = _moe_experts(0, hp, eidx, rank, cnt, expert_w1, expert_w3, expert_w2)

    h, qT, k, vT = _combine_qkv(dest, ys2d, h, gcol, row(ln_ffn_g[0]), row(ln_ffn_b[0]),
                                w_q[0], col(b_q[0]), w_k.astype(bf16), row(b_k),
                                w_v.T.astype(bf16), col(b_v), _rope_tables())
    sink_lanes = jnp.repeat((sinks[0].astype(f32) * LOG2E).reshape(N_HEADS // HC, HC), QB, axis=1)
    attT = _attention(qT, k, vT, sink_lanes)
    wr, br = _router_weights(router_group_w[1], router_group_b[1], router_expert_w[1], router_expert_b[1])
    h, hp, eidx, rank, gcol, cnt = _proj_ln_route(
        attT, h, w_o[0], row(b_o[0]), row(ln_mix_g[1]), row(ln_mix_b[1]), wr, br)
    dest, ys2d = _moe_experts(1, hp, eidx, rank, cnt, expert_w1, expert_w3, expert_w2)
    out = _combine(dest, ys2d, h, gcol, row(ln_ffn_g[1]), row(ln_ffn_b[1]), NXT)
    return out.reshape(1, SEQ, D)
```

```python
import functools
import math

import jax
import jax.numpy as jnp
import numpy as np
from jax import lax
from jax.experimental import pallas as pl
from jax.experimental.pallas import tpu as pltpu

f32 = jnp.float32
bf16 = jnp.bfloat16
i32 = jnp.int32
u32 = jnp.uint32

D = 2048
SEQ = 8192
DEPTH = 2
N_META = 16
CONV_W = 31
HEAD_DIM = 64
N_HEADS = 32
N_KV = 4
GQA = 8
KVW = N_KV * HEAD_DIM
WINDOW = 128
ROT = 16
ROPE_THETA = 500000.0
N_GROUPS = 4
EPG = 8
N_EXP = 32
FF = 256
ALPHA = (2.0 * DEPTH) ** 0.25
LN_EPS = 1e-5
LOG2E = math.log2(math.e)

LANES = 128
TM = 256
NXT = SEQ // TM
NT = NXT + 1
TP = NT * TM
META_ROW = SEQ
CHUNKS = D // LANES
HALO = 32
TME = 256
ESUB = 2
EBLK = ESUB * TME
NTE = (2 * TP) // EBLK + N_EXP
NS = NTE * EBLK
QB = 128
VMEM_LIMIT = 52 * 1024 * 1024


def _cparams():
    return pltpu.CompilerParams(dimension_semantics=("arbitrary",), vmem_limit_bytes=VMEM_LIMIT)


def _resident(shape):
    nd = len(shape)
    return pl.BlockSpec(shape, lambda *a: (0,) * nd, pipeline_mode=pl.Buffered(1))


def _layer_norm(x, g, b):
    mu = jnp.mean(x, axis=-1, keepdims=True)
    xc = x - mu
    var = jnp.mean(xc * xc, axis=-1, keepdims=True)
    return xc * lax.rsqrt(var + LN_EPS) * g + b


def _x_or_meta(i, x_ref, meta_ref):
    return jnp.where(i == NXT, meta_ref[...], x_ref[...])


HALF = D // 2
PK = HALF // LANES


def _pack_rows(v, out2d):
    rows = v.shape[0]
    bits = pltpu.bitcast(v.astype(bf16).astype(f32), u32)
    word = bits[:, HALF:] | lax.shift_right_logical(bits[:, :HALF], jnp.uint32(16))
    for s in range(PK):
        out2d[pl.ds(s, rows, stride=PK), :] = word[:, s * LANES:(s + 1) * LANES]


def _unpack_rows(in2d, rows, dtype):
    lo, hi = [], []
    for s in range(PK):
        w = in2d[pl.ds(s, rows, stride=PK), :]
        lo.append(pltpu.bitcast(lax.shift_left(w, jnp.uint32(16)), f32).astype(dtype))
        hi.append(pltpu.bitcast(w & jnp.uint32(0xFFFF0000), f32).astype(dtype))
    return jnp.concatenate(lo, axis=1), jnp.concatenate(hi, axis=1)


def _load_weights_bf16(w_hbm, w_vmem, stage, sem):
    rc = stage.shape[1]
    n = w_hbm.shape[0] // rc

    def cp(c):
        return pltpu.make_async_copy(w_hbm.at[pl.ds(c * rc, rc)], stage.at[c % 2], sem.at[c % 2])

    cp(0).start()
    for c in range(n):
        cp(c).wait()
        if c + 1 < n:
            cp(c + 1).start()
        w_vmem[c * rc:(c + 1) * rc, :] = stage[c % 2].astype(bf16)


def _weight_scratch(rows, cols, chunk_rows):
    return [pltpu.VMEM((rows, cols), bf16), pltpu.VMEM((2, chunk_rows, cols), f32), pltpu.SemaphoreType.DMA((2,))]


HBM = pl.BlockSpec(memory_space=pl.ANY)


def _glu_kernel(x_ref, meta_ref, w_hbm, b_ref, yp_ref, ybuf, w_ref, stage, sem):
    i = pl.program_id(0)

    @pl.when(i == 0)
    def _():
        _load_weights_bf16(w_hbm, w_ref, stage, sem)

    xb = _x_or_meta(i, x_ref, meta_ref).astype(bf16)
    cw = 512
    for c in range(D // cw):
        lo, hi = c * cw, (c + 1) * cw
        a = jnp.dot(xb, w_ref[:, lo:hi], preferred_element_type=f32) + b_ref[:, lo:hi]
        g = jnp.dot(xb, w_ref[:, D + lo:D + hi], preferred_element_type=f32) + b_ref[:, D + lo:D + hi]
        ybuf[:, lo:hi] = a * jax.nn.sigmoid(g)
    _pack_rows(ybuf[...], yp_ref)


def _glu(x2d, meta_pad, w_in, b_in):
    return pl.pallas_call(
        _glu_kernel,
        grid=(NT,),
        in_specs=[
            pl.BlockSpec((TM, D), lambda i: (jnp.minimum(i, NXT - 1), 0)),
            _resident((TM, D)),
            HBM,
            _resident((1, 2 * D)),
        ],
        out_specs=pl.BlockSpec((TM * PK, LANES), lambda i: (i, 0)),
        out_shape=jax.ShapeDtypeStruct((TP * PK, LANES), u32),
        scratch_shapes=[pltpu.VMEM((TM, D), f32)] + _weight_scratch(D, 2 * D, 256),
        compiler_params=_cparams(),
        name="glu",
    )(x2d, meta_pad, w_in, b_in)


APITCH = TM + 8


def _chunk_row(c):
    return 2 * (c % PK) + c // PK


def _conv_kernel(yp_ref, ymeta_ref, w_ref, bdw_ref, g_ref, b_ref, z_ref, scr, accs):
    i = pl.program_id(0)
    hrows = HALO * PK
    trows = TM * PK

    @pl.when(i == 0)
    def _():
        scr[0:(HALO - N_META) * PK, :] = jnp.zeros(((HALO - N_META) * PK, LANES), u32)
        scr[(HALO - N_META) * PK:hrows, :] = ymeta_ref[...]

    @pl.when(i == NT - 1)
    def _():
        scr[0:hrows, :] = jnp.zeros((hrows, LANES), u32)

    @pl.when(jnp.logical_and(i > 0, i < NT - 1))
    def _():
        scr[0:hrows, :] = scr[trows:trows + hrows, :]

    scr[hrows:hrows + trows, :] = yp_ref[...]

    tb = 16
    first = (HALO - (CONV_W - 1)) * PK

    def block(t, carry):
        base = pl.multiple_of(t * (tb * PK), tb * PK)
        acc = jnp.zeros((tb, CHUNKS, LANES), f32)
        for j in range(CONV_W):
            words = scr[pl.ds(base + first + j * PK, tb * PK), :]
            sl = pltpu.bitcast(words, bf16).reshape(tb, CHUNKS, LANES)
            acc = acc + sl.astype(f32) * w_ref[j].astype(f32)[None]
        flat = acc.reshape(tb * CHUNKS, LANES)
        for tok in range(tb):
            for half in range(CHUNKS // 8):
                src = flat[tok * CHUNKS + half * 8:tok * CHUNKS + half * 8 + 8, :]
                accs[pl.ds(half * 8 * APITCH + t * tb + tok, 8, stride=APITCH), :] = src
        return carry

    lax.fori_loop(0, TM // tb, block, 0)

    rb = 16

    def finish(t, carry):
        r0 = pl.multiple_of(t * rb, rb)
        cols = [accs[pl.ds(_chunk_row(c) * APITCH + r0, rb), :] for c in range(CHUNKS)]
        v = _layer_norm(jnp.concatenate(cols, axis=1) + bdw_ref[...], g_ref[...], b_ref[...])
        z_ref[pl.ds(r0, rb), :] = (v * jax.nn.sigmoid(v)).astype(bf16)
        return carry

    lax.fori_loop(0, TM // rb, finish, 0, unroll=8)


def _conv(yp, w_dw3, b_dw, ln_g, ln_b):
    return pl.pallas_call(
        _conv_kernel,
        grid=(NT,),
        in_specs=[
            pl.BlockSpec((TM * PK, LANES), lambda i: (i, 0)),
            pl.BlockSpec((N_META * PK, LANES), lambda i: (META_ROW // N_META, 0)),
            _resident((CONV_W, CHUNKS, LANES)),
            _resident((1, D)),
            _resident((1, D)),
            _resident((1, D)),
        ],
        out_specs=pl.BlockSpec((TM, D), lambda i: (i, 0)),
        out_shape=jax.ShapeDtypeStruct((TP, D), bf16),
        scratch_shapes=[
            pltpu.VMEM(((TM + HALO) * PK, LANES), u32),
            pltpu.VMEM((CHUNKS * APITCH, LANES), f32),
        ],
        compiler_params=_cparams(),
        name="conv_ln_swish",
    )(yp, yp, w_dw3, b_dw, ln_g, ln_b)


R_EXP0 = 8


def _route(h, wr_ref, br_ref, running, valid):
    logits = jnp.dot(h.astype(bf16), wr_ref[...], preferred_element_type=f32) + br_ref[...]
    lt = logits.T
    gl = [lt[k:k + 1, :] for k in range(N_GROUPS)]
    gm = functools.reduce(jnp.maximum, gl)
    gex = [jnp.exp(v - gm) for v in gl]
    gden = functools.reduce(lambda a, b: a + b, gex)
    gp = [v / gden for v in gex]
    best = gp[0]
    gi = jnp.zeros((1, TM), i32)
    for k in range(1, N_GROUPS):
        better = gp[k] > best
        gi = jnp.where(better, k, gi)
        best = jnp.where(better, gp[k], best)
    esel = lt[R_EXP0:R_EXP0 + EPG, :]
    for k in range(1, N_GROUPS):
        esel = jnp.where(gi == k, lt[R_EXP0 + EPG * k:R_EXP0 + EPG * (k + 1), :], esel)
    em = jnp.max(esel, axis=0, keepdims=True)
    eex = jnp.exp(esel - em)
    ep = eex / jnp.sum(eex, axis=0, keepdims=True)
    io8 = lax.broadcasted_iota(i32, (EPG, TM), 0)
    v1 = jnp.max(ep, axis=0, keepdims=True)
    i1 = jnp.min(jnp.where(ep == v1, io8, EPG), axis=0, keepdims=True)
    ep2 = jnp.where(io8 == i1, -1.0, ep)
    v2 = jnp.max(ep2, axis=0, keepdims=True)
    i2 = jnp.min(jnp.where(ep2 == v2, io8, EPG), axis=0, keepdims=True)
    s = v1 + v2
    gate0 = best * (v1 / s)
    gate1 = best * (v2 / s)
    f0 = gi * EPG + i1
    f1 = gi * EPG + i2

    io32 = lax.broadcasted_iota(i32, (N_EXP, TM), 0)
    oh0 = (io32 == f0).astype(f32)
    oh1 = (io32 == f1).astype(f32)
    cnt = oh0 + oh1
    upper = (lax.broadcasted_iota(i32, (TM, TM), 0) < lax.broadcasted_iota(i32, (TM, TM), 1))
    before = jnp.dot(cnt.astype(bf16), upper.astype(f32).astype(bf16), preferred_element_type=f32)
    base = running[...] + before
    r0 = jnp.sum(oh0 * base, axis=0, keepdims=True).astype(i32)
    r1 = jnp.sum(oh1 * base, axis=0, keepdims=True).astype(i32)
    running[...] = running[...] + valid * jnp.sum(cnt, axis=1, keepdims=True)

    io128 = lax.broadcasted_iota(i32, (LANES, TM), 0)
    gcol = jnp.where(io128 == 0, gate0, jnp.where(io128 == 1, gate1, 0.0)).T
    return f0, f1, r0, r1, gcol


def _proj_ln_route_epilogue(valid, a, res, bias_ref, g_ref, b_ref, wr_ref, br_ref,
                            h_ref, hp_ref, eidx_ref, rank_ref, gcol_ref, cnt_ref, running):
    mix = a + bias_ref[...]
    h = _layer_norm(ALPHA * res + mix, g_ref[...], b_ref[...])
    h_ref[...] = h
    _pack_rows(h, hp_ref)
    f0, f1, r0, r1, gcol = _route(h, wr_ref, br_ref, running, valid)
    eidx_ref[0, 0:1, :] = f0
    eidx_ref[0, 1:2, :] = f1
    rank_ref[0, 0:1, :] = r0
    rank_ref[0, 1:2, :] = r1
    gcol_ref[...] = gcol
    cnt_ref[...] = running[...]


def _proj_ln_route_kernel(first, a_ref, *refs):
    if first:
        x_ref, meta_ref, w_hbm, *rest = refs
    else:
        res_ref, w_hbm, *rest = refs
    *rest, running, accbuf, w_ref, stage, sem = rest
    i = pl.program_id(0)

    @pl.when(i == 0)
    def _():
        running[...] = jnp.zeros_like(running)
        accbuf[1] = jnp.zeros((TM, D), f32)
        _load_weights_bf16(w_hbm, w_ref, stage, sem)

    valid = (i > 0).astype(f32)
    for parity in range(2):
        @pl.when(lax.rem(i, 2) == parity)
        def _():
            prev = accbuf[1 - parity]
            if first:
                accbuf[parity] = jnp.dot(a_ref[...], w_ref[...], preferred_element_type=f32)
                res = jnp.where(i - 1 == NXT, meta_ref[...], x_ref[...])
            else:
                accbuf[parity] = lax.dot_general(a_ref[...], w_ref[...], (((0,), (0,)), ((), ())),
                                                 preferred_element_type=f32)
                res = res_ref[...]
            _proj_ln_route_epilogue(valid, prev, res, *rest, running)


def _proj_ln_route(a, res, w, bias, ln_g, ln_b, wr, br):
    first = isinstance(res, tuple)
    cur = lambda i: jnp.minimum(i, NT - 1)
    prv = lambda i: jnp.maximum(i - 1, 0)
    if first:
        a_spec = pl.BlockSpec((TM, D), lambda i: (cur(i), 0))
        res_specs = [pl.BlockSpec((TM, D), lambda i: (jnp.minimum(prv(i), NXT - 1), 0)), _resident((TM, D))]
        res_args = list(res)
    else:
        a_spec = pl.BlockSpec((D, TM), lambda i: (0, cur(i)))
        res_specs = [pl.BlockSpec((TM, D), lambda i: (prv(i), 0))]
        res_args = [res]
    tile3 = pl.BlockSpec((1, 2, TM), lambda i: (prv(i), 0, 0))
    return pl.pallas_call(
        functools.partial(_proj_ln_route_kernel, first),
        grid=(NT + 1,),
        in_specs=[a_spec] + res_specs + [
            HBM, _resident((1, D)), _resident((1, D)), _resident((1, D)),
            _resident((D, LANES)), _resident((1, LANES)),
        ],
        out_specs=[
            pl.BlockSpec((TM, D), lambda i: (prv(i), 0)),
            pl.BlockSpec((TM * PK, LANES), lambda i: (prv(i), 0)),
            tile3, tile3,
            pl.BlockSpec((TM, LANES), lambda i: (prv(i), 0)),
            pl.BlockSpec((N_EXP, TM), lambda i: (0, 0)),
        ],
        out_shape=[
            jax.ShapeDtypeStruct((TP, D), f32),
            jax.ShapeDtypeStruct((TP * PK, LANES), u32),
            jax.ShapeDtypeStruct((NT, 2, TM), i32),
            jax.ShapeDtypeStruct((NT, 2, TM), i32),
            jax.ShapeDtypeStruct((TP, LANES), f32),
            jax.ShapeDtypeStruct((N_EXP, TM), f32),
        ],
        scratch_shapes=[pltpu.VMEM((N_EXP, TM), f32), pltpu.VMEM((2, TM, D), f32)] + _weight_scratch(D, D, 512),
        compiler_params=_cparams(),
        name="proj_ln_route",
    )(a, *res_args, w, bias, ln_g, ln_b, wr, br)


def _plan(eidx, rank, cnt):
    counts = cnt[:, 0].astype(i32)
    used = ((counts + TME - 1) // TME) * TME
    padded = ((counts + EBLK - 1) // EBLK) * EBLK
    ends = jnp.cumsum(padded)
    offs = ends - padded
    ntiles = ends[-1] // EBLK
    off_of = jnp.sum(jnp.where(eidx[..., None] == jnp.arange(N_EXP, dtype=i32), offs, 0), axis=-1)
    dest = (off_of + rank).reshape(-1)
    tile_start = jnp.minimum(jnp.arange(NTE, dtype=i32), ntiles - 1) * EBLK
    tile_expert = jnp.minimum(jnp.sum(tile_start[:, None] >= ends[None, :], axis=1), N_EXP - 1).astype(i32)
    nsub = jnp.clip((offs[tile_expert] + used[tile_expert] - tile_start) // TME, 0, ESUB).astype(i32)
    zstart = jnp.where(used > 0, offs + used - TME, 0).astype(i32)
    zflag = (used > 0).astype(i32)
    tid = jnp.arange(NTE, dtype=i32)
    live = tid < ntiles
    first = jnp.logical_and(live, jnp.logical_or(tid == 0, tile_expert != jnp.roll(tile_expert, 1)))
    slot = (jnp.cumsum(first.astype(i32)) - 1) % 2
    nxt_first = lax.cummin(jnp.where(first, tid, NTE), reverse=True)
    after = jnp.concatenate([nxt_first[1:], jnp.full((1,), NTE, i32)])
    nxt = jnp.where(after < NTE, tile_expert[jnp.minimum(after, NTE - 1)], -1)
    return (dest.astype(i32), tile_expert, ntiles.reshape(1).astype(i32), zstart, zflag,
            first.astype(i32), slot.astype(i32), nxt.astype(i32), nsub)


ISSUE_UNROLL = 8


def _scatter_kernel(dest_ref, zstart_ref, zflag_ref, hp_ref, xs_hbm, zeros, sem, zsem):
    i = pl.program_id(0)

    @pl.when(i == 0)
    def _():
        zeros[...] = jnp.zeros_like(zeros)
        def fill(e):
            start = pl.multiple_of(zstart_ref[e], TME)
            return pltpu.make_async_copy(zeros, xs_hbm.at[pl.ds(start, TME)], zsem)

        for e in range(N_EXP):
            @pl.when(zflag_ref[e] > 0)
            def _():
                fill(e).start()
        for e in range(N_EXP):
            @pl.when(zflag_ref[e] > 0)
            def _():
                fill(e).wait()

    base = i * (2 * TM)

    def row(r, carry):
        for k in range(2):
            d = dest_ref[base + k * TM + r]
            pltpu.make_async_copy(hp_ref.at[pl.ds(r, 1)], xs_hbm.at[pl.ds(d, 1)], sem).start(priority=k)
        return carry

    lax.fori_loop(0, TM, row, 0, unroll=ISSUE_UNROLL)

    def drain(r, carry):
        pltpu.make_async_copy(hp_ref.at[pl.ds(0, 1)], xs_hbm.at[pl.ds(0, 1)], sem).wait()
        return carry

    lax.fori_loop(0, 2 * TM, drain, 0, unroll=ISSUE_UNROLL)


def _scatter(dest, zstart, zflag, hp3):
    return pl.pallas_call(
        _scatter_kernel,
        grid_spec=pltpu.PrefetchScalarGridSpec(
            num_scalar_prefetch=3,
            grid=(NT,),
            in_specs=[pl.BlockSpec((TM, PK, LANES), lambda i, *_: (i, 0, 0))],
            out_specs=pl.BlockSpec(memory_space=pl.ANY),
            scratch_shapes=[pltpu.VMEM((TME, PK, LANES), u32), pltpu.SemaphoreType.DMA(()),
                            pltpu.SemaphoreType.DMA(())],
        ),
        out_shape=jax.ShapeDtypeStruct((NS, PK, LANES), u32),
        compiler_params=pltpu.CompilerParams(dimension_semantics=("arbitrary",), vmem_limit_bytes=VMEM_LIMIT,
                                             has_side_effects=True),
        name="moe_scatter",
    )(dest, zstart, zflag, hp3)


def _expert_kernel(layer, te_ref, nt_ref, first_ref, slot_ref, nxt_ref, nsub_ref, xs_ref, w1_hbm, w3_hbm, w2_hbm,
                   ys_ref, wb1, wb3, wb2, w1c, w3c, w2c, sem):
    i = pl.program_id(0)

    def copies(e, s):
        return (pltpu.make_async_copy(w1_hbm.at[layer, e], wb1.at[s], sem.at[s]),
                pltpu.make_async_copy(w3_hbm.at[layer, e], wb3.at[s], sem.at[s]),
                pltpu.make_async_copy(w2_hbm.at[layer, e], wb2.at[s], sem.at[s]))

    @pl.when(i == 0)
    def _():
        for cp in copies(te_ref[0], 0):
            cp.start()

    @pl.when(jnp.logical_and(i < nt_ref[0], first_ref[i] > 0))
    def _():
        s = slot_ref[i]
        for cp in copies(te_ref[i], s):
            cp.wait()

        @pl.when(nxt_ref[i] >= 0)
        def _():
            for cp in copies(nxt_ref[i], 1 - s):
                cp.start()

        w1c[...] = wb1[s].astype(bf16)
        w3c[...] = wb3[s].astype(bf16)
        w2c[...] = wb2[s].astype(bf16)

    for sub in range(ESUB):
        @pl.when(jnp.logical_and(i < nt_ref[0], sub < nsub_ref[i]))
        def _():
            rows = pl.ds(sub * TME * PK, TME * PK)
            xlo, xhi = _unpack_rows(xs_ref.at[rows], TME, bf16)
            a = (jnp.dot(xlo, w1c[0:HALF, :], preferred_element_type=f32)
                 + jnp.dot(xhi, w1c[HALF:D, :], preferred_element_type=f32))
            b = (jnp.dot(xlo, w3c[0:HALF, :], preferred_element_type=f32)
                 + jnp.dot(xhi, w3c[HALF:D, :], preferred_element_type=f32))
            hid = (a * jax.nn.sigmoid(a) * b).astype(bf16)
            _pack_rows(jnp.dot(hid, w2c[...], preferred_element_type=f32), ys_ref.at[rows])


def _experts(layer, tile_expert, ntiles, first, slot, nxt, nsub, xs2d, w1, w3, w2):
    def row_map(i, te, nt, *_):
        return (jnp.minimum(i, nt[0] - 1), 0)

    hbm = pl.BlockSpec(memory_space=pl.ANY)
    return pl.pallas_call(
        functools.partial(_expert_kernel, layer),
        grid_spec=pltpu.PrefetchScalarGridSpec(
            num_scalar_prefetch=6,
            grid=(NTE,),
            in_specs=[pl.BlockSpec((EBLK * PK, LANES), row_map), hbm, hbm, hbm],
            out_specs=pl.BlockSpec((EBLK * PK, LANES), row_map),
            scratch_shapes=[
                pltpu.VMEM((2, D, FF), f32), pltpu.VMEM((2, D, FF), f32), pltpu.VMEM((2, FF, D), f32),
                pltpu.VMEM((D, FF), bf16), pltpu.VMEM((D, FF), bf16), pltpu.VMEM((FF, D), bf16),
                pltpu.SemaphoreType.DMA((2,)),
            ],
        ),
        out_shape=jax.ShapeDtypeStruct((NS * PK, LANES), u32),
        compiler_params=_cparams(),
        name="moe_experts",
    )(tile_expert, ntiles, first, slot, nxt, nsub, xs2d, w1, w3, w2)


def _gather_combine(n, dest_ref, ys_hbm, h_ref, gcol_ref, g_ref, b_ref, buf, sem):
    i = pl.program_id(0)
    slot = lax.rem(i, 2)

    def issue(tile, s):
        base = tile * (2 * TM)

        def row(r, carry):
            for k in range(2):
                d = pl.multiple_of(dest_ref[base + k * TM + r] * PK, PK)
                pltpu.make_async_copy(ys_hbm.at[pl.ds(d, PK)],
                                      buf.at[s, k, pl.ds(pl.multiple_of(r * PK, PK), PK)],
                                      sem.at[s]).start(priority=k)
            return carry

        lax.fori_loop(0, TM, row, 0, unroll=ISSUE_UNROLL)

    @pl.when(i == 0)
    def _():
        issue(0, 0)

    @pl.when(i + 1 < n)
    def _():
        issue(i + 1, 1 - slot)

    def drain(r, carry):
        pltpu.make_async_copy(ys_hbm.at[pl.ds(0, PK)], buf.at[slot, 0, pl.ds(0, PK)], sem.at[slot]).wait()
        return carry

    lax.fori_loop(0, 2 * TM, drain, 0, unroll=ISSUE_UNROLL)

    lo0, hi0 = _unpack_rows(buf.at[slot, 0], TM, f32)
    lo1, hi1 = _unpack_rows(buf.at[slot, 1], TM, f32)
    g0, g1 = gcol_ref[:, 0:1], gcol_ref[:, 1:2]
    ffn = jnp.concatenate([lo0 * g0 + lo1 * g1, hi0 * g0 + hi1 * g1], axis=1)
    return _layer_norm(ALPHA * h_ref[...] + ffn, g_ref[...], b_ref[...])


def _combine_kernel(n, dest_ref, ys_hbm, h_ref, gcol_ref, g_ref, b_ref, o_ref, buf, sem):
    o_ref[...] = _gather_combine(n, dest_ref, ys_hbm, h_ref, gcol_ref, g_ref, b_ref, buf, sem)


def _combine(dest, ys2d, h, gcol, ln_g, ln_b, ntiles_out):
    return pl.pallas_call(
        functools.partial(_combine_kernel, ntiles_out),
        grid_spec=pltpu.PrefetchScalarGridSpec(
            num_scalar_prefetch=1,
            grid=(ntiles_out,),
            in_specs=[
                pl.BlockSpec(memory_space=pl.ANY),
                pl.BlockSpec((TM, D), lambda i, *_: (i, 0)),
                pl.BlockSpec((TM, LANES), lambda i, *_: (i, 0)),
                pl.BlockSpec((1, D), lambda i, *_: (0, 0)),
                pl.BlockSpec((1, D), lambda i, *_: (0, 0)),
            ],
            out_specs=pl.BlockSpec((TM, D), lambda i, *_: (i, 0)),
            scratch_shapes=[pltpu.VMEM((2, 2, TM * PK, LANES), u32), pltpu.SemaphoreType.DMA((2,))],
        ),
        out_shape=jax.ShapeDtypeStruct((ntiles_out * TM, D), f32),
        compiler_params=_cparams(),
        name="moe_combine_ln",
    )(dest, ys2d, h, gcol, ln_g, ln_b)


def _moe_experts(layer, hp2d, eidx, rank, cnt, w1, w3, w2):
    dest, tile_expert, ntiles, zstart, zflag, first, slot, nxt, nsub = _plan(eidx, rank, cnt)
    xs = _scatter(dest, zstart, zflag, hp2d.reshape(TP, PK, LANES))
    ys2d = _experts(layer, tile_expert, ntiles, first, slot, nxt, nsub, xs.reshape(NS * PK, LANES), w1, w3, w2)
    return dest, ys2d


NT_DIMS = (((1,), (1,)), ((), ()))


def _rope_rows(t, cos, sa, sb):
    w = t.shape[1]
    reps = w // LANES
    c = jnp.tile(cos, (1, reps))
    a = jnp.tile(sa, (1, reps))
    b = jnp.tile(sb, (1, reps))
    return t * c + pltpu.roll(t, w - ROT // 2, 1) * a + pltpu.roll(t, ROT // 2, 1) * b


def _combine_qkv_kernel(n, dest_ref, ys_hbm, h_ref, gcol_ref, g_ref, b_ref,
                        wq_hbm, bq_ref, wk_ref, bk_ref, wvT_ref, bv_ref, cosT_ref, sinT_ref, cos_ref, sa_ref, sb_ref,
                        h2_ref, qT_ref, k_ref, vT_ref, buf, gsem, wq_ref, stage, wsem):
    @pl.when(pl.program_id(0) == 0)
    def _():
        _load_weights_bf16(wq_hbm, wq_ref, stage, wsem)

    h2 = _gather_combine(n, dest_ref, ys_hbm, h_ref, gcol_ref, g_ref, b_ref, buf, gsem)
    h2_ref[...] = h2
    hb = h2.astype(bf16)
    scale = LOG2E / math.sqrt(HEAD_DIM)
    half = ROT // 2
    cosT = cosT_ref[...][None]
    sinT = sinT_ref[...][None]
    rows = GQA * HEAD_DIM
    for c in range(D // rows):
        lo, hi = c * rows, (c + 1) * rows
        t = lax.dot_general(wq_ref[:, lo:hi], hb, (((0,), (1,)), ((), ())), preferred_element_type=f32) \
            + bq_ref[lo:hi, :]
        t3 = t.reshape(GQA, HEAD_DIM, TM)
        x1, x2 = t3[:, 0:half, :], t3[:, half:ROT, :]
        r = jnp.concatenate([x1 * cosT - x2 * sinT, x2 * cosT + x1 * sinT, t3[:, ROT:, :]], axis=1)
        qT_ref[lo:hi, :] = (r * scale).reshape(rows, TM).astype(bf16)
    t = jnp.dot(hb, wk_ref[...], preferred_element_type=f32) + bk_ref[...]
    k_ref[...] = _rope_rows(t, cos_ref[...], sa_ref[...], sb_ref[...]).astype(bf16)
    t = lax.dot_general(wvT_ref[...], hb, NT_DIMS, preferred_element_type=f32) + bv_ref[...]
    vT_ref[...] = t.astype(bf16)


def _combine_qkv(dest, ys2d, h, gcol, ln_g, ln_b, wq, bq_col, wk, bk, wvT, bv_col, tables):
    cosT, sinT, cos_t, sa_t, sb_t = tables
    const = lambda shape: pl.BlockSpec(shape, lambda i, *_: (0,) * len(shape))
    tabT = pl.BlockSpec((ROT // 2, TM), lambda i, *_: (0, i))
    tab = pl.BlockSpec((TM, LANES), lambda i, *_: (i, 0))
    return pl.pallas_call(
        functools.partial(_combine_qkv_kernel, NT),
        grid_spec=pltpu.PrefetchScalarGridSpec(
            num_scalar_prefetch=1,
            grid=(NT,),
            in_specs=[
                HBM,
                pl.BlockSpec((TM, D), lambda i, *_: (i, 0)),
                pl.BlockSpec((TM, LANES), lambda i, *_: (i, 0)),
                const((1, D)), const((1, D)),
                HBM, const((D, 1)),
                const((D, KVW)), const((1, KVW)),
                const((KVW, D)), const((KVW, 1)),
                tabT, tabT, tab, tab, tab,
            ],
            out_specs=[
                pl.BlockSpec((TM, D), lambda i, *_: (i, 0)),
                pl.BlockSpec((D, TM), lambda i, *_: (0, i)),
                pl.BlockSpec((TM, KVW), lambda i, *_: (i, 0)),
                pl.BlockSpec((KVW, TM), lambda i, *_: (0, i)),
            ],
            scratch_shapes=[pltpu.VMEM((2, 2, TM * PK, LANES), u32), pltpu.SemaphoreType.DMA((2,))]
            + _weight_scratch(D, D, 512),
        ),
        out_shape=[
            jax.ShapeDtypeStruct((TP, D), f32),
            jax.ShapeDtypeStruct((D, TP), bf16),
            jax.ShapeDtypeStruct((TP, KVW), bf16),
            jax.ShapeDtypeStruct((KVW, TP), bf16),
        ],
        compiler_params=_cparams(),
        name="moe_combine_ln_qkv_rope",
    )(dest, ys2d, h, gcol, ln_g, ln_b, wq, bq_col, wk, bk, wvT, bv_col, cosT, sinT, cos_t, sa_t, sb_t)


NKEY = 2 * QB + N_META
HC = 8
LW = HC * QB
SUB = 8


def _col_max(s):
    parts = [s[r * SUB:(r + 1) * SUB] for r in range(NKEY // SUB)]
    while len(parts) > 1:
        nxt = [jnp.maximum(parts[j], parts[j + 1]) for j in range(0, len(parts) - 1, 2)]
        if len(parts) % 2:
            nxt.append(parts[-1])
        parts = nxt
    return jnp.max(parts[0], axis=0, keepdims=True)


def _attn_kernel(qT_ref, kc_ref, kp_ref, km_ref, vTc_ref, vTp_ref, vTm_ref, sink_ref, oT_ref):
    i = pl.program_id(0)
    is_meta = i == NT - 1
    ck = lax.broadcasted_iota(i32, (NKEY, QB), 0)
    rq = lax.broadcasted_iota(i32, (NKEY, QB), 1)
    in_band = jnp.logical_and(ck > rq, ck <= rq + QB)
    meta_ok = jnp.logical_and(ck >= 2 * QB, jnp.logical_or(jnp.logical_not(is_meta), ck - 2 * QB <= rq))
    ones = jnp.ones((SUB, NKEY), bf16)

    for blk in range(TM // QB):
        lo = jnp.where(is_meta, 2 * QB, jnp.where(jnp.logical_and(i == 0, blk == 0), QB, 0))
        valid = jnp.logical_or(meta_ok, jnp.logical_and(in_band, ck >= lo))
        bias = jnp.where(valid, 0.0, -jnp.inf)
        bias = jnp.concatenate([bias] * HC, axis=1)
        c0 = blk * QB
        for g in range(N_KV):
            gs = slice(g * HEAD_DIM, (g + 1) * HEAD_DIM)
            if blk == 0:
                kprev, vprevT = kp_ref[:, gs], vTp_ref[gs, :]
            else:
                kprev, vprevT = kc_ref[c0 - QB:c0, gs], vTc_ref[gs, c0 - QB:c0]
            kcat = jnp.concatenate([kprev, kc_ref[c0:c0 + QB, gs], km_ref[:, gs]], axis=0)
            vcatT = jnp.concatenate([vprevT, vTc_ref[gs, c0:c0 + QB], vTm_ref[gs, 0:N_META]], axis=1)
            vext = jnp.concatenate([vcatT, ones], axis=0)
            for c in range(GQA // HC):
                h0 = g * GQA + c * HC
                heads = [qT_ref[(h0 + j) * HEAD_DIM:(h0 + j + 1) * HEAD_DIM, c0:c0 + QB] for j in range(HC)]
                s = jnp.dot(kcat, jnp.concatenate(heads, axis=1), preferred_element_type=f32) + bias
                sink = sink_ref[h0 // HC:h0 // HC + 1, :]
                m = jnp.maximum(_col_max(s), sink)
                p = jnp.exp2(s - m).astype(bf16)
                oe = jnp.dot(vext, p, preferred_element_type=f32)
                den = oe[HEAD_DIM:HEAD_DIM + 1, :] + jnp.exp2(sink - m)
                o = (oe[0:HEAD_DIM, :] * (1.0 / den)).astype(bf16)
                for j in range(HC):
                    oT_ref[(h0 + j) * HEAD_DIM:(h0 + j + 1) * HEAD_DIM, c0:c0 + QB] = o[:, j * QB:(j + 1) * QB]


def _attention(qT, k, vT, sink_lanes):
    prev_blk = lambda i: jnp.maximum(i * (TM // QB) - 1, 0)
    return pl.pallas_call(
        _attn_kernel,
        grid=(NT,),
        in_specs=[
            pl.BlockSpec((D, TM), lambda i: (0, i)),
            pl.BlockSpec((TM, KVW), lambda i: (i, 0)),
            pl.BlockSpec((QB, KVW), lambda i: (prev_blk(i), 0)),
            pl.BlockSpec((N_META, KVW), lambda i: (META_ROW // N_META, 0)),
            pl.BlockSpec((KVW, TM), lambda i: (0, i)),
            pl.BlockSpec((KVW, QB), lambda i: (0, prev_blk(i))),
            pl.BlockSpec((KVW, LANES), lambda i: (0, META_ROW // LANES)),
            _resident((N_HEADS // HC, LW)),
        ],
        out_specs=pl.BlockSpec((D, TM), lambda i: (0, i)),
        out_shape=jax.ShapeDtypeStruct((D, TP), bf16),
        compiler_params=_cparams(),
        name="swa_attention",
    )(qT, k, k, k, vT, vT, vT, sink_lanes)


def _router_weights(wg, bg, we, be):
    gap = R_EXP0 - N_GROUPS
    tail = LANES - R_EXP0 - N_EXP
    wr = jnp.concatenate([wg, jnp.zeros((D, gap), f32), we, jnp.zeros((D, tail), f32)], axis=1)
    br = jnp.concatenate([bg, jnp.zeros((gap,), f32), be, jnp.zeros((tail,), f32)]).reshape(1, LANES)
    return wr.astype(bf16), br


def _rope_tables():
    pos = np.concatenate([np.arange(SEQ) + N_META, np.arange(TM)]).astype(np.float32)
    half = ROT // 2
    inv_freq = (np.float32(ROPE_THETA) ** (-np.arange(0, ROT, 2, dtype=np.float32) / np.float32(ROT)))
    ang = pos[:, None] * inv_freq.astype(np.float32)[None, :]
    cos, sin = np.cos(ang).astype(np.float32), np.sin(ang).astype(np.float32)
    ones = np.ones((TP, HEAD_DIM - ROT), np.float32)
    zeros = np.zeros((TP, HEAD_DIM - ROT), np.float32)
    z8 = np.zeros((TP, half), np.float32)
    cos_h = np.concatenate([cos, cos, ones], axis=1)
    sa_h = np.concatenate([-sin, z8, zeros], axis=1)
    sb_h = np.concatenate([z8, sin, zeros], axis=1)
    rep = LANES // HEAD_DIM
    tabs = (cos.T, sin.T, np.tile(cos_h, (1, rep)), np.tile(sa_h, (1, rep)), np.tile(sb_h, (1, rep)))
    return tuple(jnp.asarray(np.ascontiguousarray(t)) for t in tabs)


def kernel(x, meta_tokens, conv_w_in, conv_b_in, conv_w_dw, conv_b_dw, conv_ln_g, conv_ln_b, conv_w_out,
           conv_b_out, w_k, b_k, w_v, b_v, w_q, b_q, w_o, b_o, sinks, ln_mix_g, ln_mix_b, ln_ffn_g, ln_ffn_b,
           router_group_w, router_group_b, router_expert_w, router_expert_b, expert_w1, expert_w3, expert_w2):
    assert x.shape == (1, SEQ, D)
    row = lambda v: v.reshape(1, -1)
    col = lambda v: v.reshape(-1, 1)
    x2d = x.reshape(SEQ, D)
    meta_pad = jnp.pad(meta_tokens.astype(f32), ((0, TM - N_META), (0, 0)))

    y = _glu(x2d, meta_pad, conv_w_in[0], row(conv_b_in[0]))
    w_dw = conv_w_dw[0].reshape(CONV_W, 2, PK, LANES).transpose(0, 2, 1, 3).reshape(CONV_W, CHUNKS, LANES)
    z = _conv(y, w_dw.astype(bf16), row(conv_b_dw[0]), row(conv_ln_g[0]),
              row(conv_ln_b[0]))
    wr, br = _router_weights(router_group_w[0], router_group_b[0], router_expert_w[0], router_expert_b[0])
    h, hp, eidx, rank, gcol, cnt = _proj_ln_route(
        z, (x2d, meta_pad), conv_w_out[0], row(conv_b_out[0]), row(ln_mix_g[0]), row(ln_mix_b[0]),
        wr, br)
    dest, ys2d = _moe_experts(0, hp, eidx, rank, cnt, expert_w1, expert_w3, expert_w2)

    h, qT, k, vT = _combine_qkv(dest, ys2d, h, gcol, row(ln_ffn_g[0]), row(ln_ffn_b[0]),
                                w_q[0], col(b_q[0]), w_k.astype(bf16), row(b_k),
                                w_v.T.astype(bf16), col(b_v), _rope_tables())
    sink_lanes = jnp.repeat((sinks[0].astype(f32) * LOG2E).reshape(N_HEADS // HC, HC), QB, axis=1)
    attT = _attention(qT, k, vT, sink_lanes)
    wr, br = _router_weights(router_group_w[1], router_group_b[1], router_expert_w[1], router_expert_b[1])
    h, hp, eidx, rank, gcol, cnt = _proj_ln_route(
        attT, h, w_o[0], row(b_o[0]), row(ln_mix_g[1]), row(ln_mix_b[1]), wr, br)
    dest, ys2d = _moe_experts(1, hp, eidx, rank, cnt, expert_w1, expert_w3, expert_w2)
    out = _combine(dest, ys2d, h, gcol, row(ln_ffn_g[1]), row(ln_ffn_b[1]), NXT)
    return out.reshape(1, SEQ, D)
```

```python
import functools
import math

import jax
import jax.numpy as jnp
import numpy as np
from jax import lax
from jax.experimental import pallas as pl
from jax.experimental.pallas import tpu as pltpu

f32 = jnp.float32
bf16 = jnp.bfloat16
i32 = jnp.int32
u32 = jnp.uint32

D = 2048
SEQ = 8192
DEPTH = 2
N_META = 16
CONV_W = 31
HEAD_DIM = 64
N_HEADS = 32
N_KV = 4
GQA = 8
KVW = N_KV * HEAD_DIM
WINDOW = 128
ROT = 16
ROPE_THETA = 500000.0
N_GROUPS = 4
EPG = 8
N_EXP = 32
FF = 256
ALPHA = (2.0 * DEPTH) ** 0.25
LN_EPS = 1e-5
LOG2E = math.log2(math.e)

LANES = 128
TM = 256
NXT = SEQ // TM
NT = NXT + 1
TP = NT * TM
META_ROW = SEQ
CHUNKS = D // LANES
HALO = 32
TME = 256
ESUB = 2
EBLK = ESUB * TME
NTE = (2 * TP) // EBLK + N_EXP
NS = NTE * EBLK
QB = 128
VMEM_LIMIT = 52 * 1024 * 1024


def _cparams():
    return pltpu.CompilerParams(dimension_semantics=("arbitrary",), vmem_limit_bytes=VMEM_LIMIT)


def _resident(shape):
    nd = len(shape)
    return pl.BlockSpec(shape, lambda *a: (0,) * nd, pipeline_mode=pl.Buffered(1))


def _layer_norm(x, g, b):
    mu = jnp.mean(x, axis=-1, keepdims=True)
    xc = x - mu
    var = jnp.mean(xc * xc, axis=-1, keepdims=True)
    return xc * lax.rsqrt(var + LN_EPS) * g + b


def _x_or_meta(i, x_ref, meta_ref):
    return jnp.where(i == NXT, meta_ref[...], x_ref[...])


HALF = D // 2
PK = HALF // LANES


def _pack_rows(v, out2d):
    rows = v.shape[0]
    bits = pltpu.bitcast(v.astype(bf16).astype(f32), u32)
    word = bits[:, HALF:] | lax.shift_right_logical(bits[:, :HALF], jnp.uint32(16))
    for s in range(PK):
        out2d[pl.ds(s, rows, stride=PK), :] = word[:, s * LANES:(s + 1) * LANES]


def _unpack_rows(in2d, rows, dtype):
    lo, hi = [], []
    for s in range(PK):
        w = in2d[pl.ds(s, rows, stride=PK), :]
        lo.append(pltpu.bitcast(lax.shift_left(w, jnp.uint32(16)), f32).astype(dtype))
        hi.append(pltpu.bitcast(w & jnp.uint32(0xFFFF0000), f32).astype(dtype))
    return jnp.concatenate(lo, axis=1), jnp.concatenate(hi, axis=1)


def _load_weights_bf16(w_hbm, w_vmem, stage, sem):
    rc = stage.shape[1]
    n = w_hbm.shape[0] // rc

    def cp(c):
        return pltpu.make_async_copy(w_hbm.at[pl.ds(c * rc, rc)], stage.at[c % 2], sem.at[c % 2])

    cp(0).start()
    for c in range(n):
        cp(c).wait()
        if c + 1 < n:
            cp(c + 1).start()
        w_vmem[c * rc:(c + 1) * rc, :] = stage[c % 2].astype(bf16)


def _weight_scratch(rows, cols, chunk_rows):
    return [pltpu.VMEM((rows, cols), bf16), pltpu.VMEM((2, chunk_rows, cols), f32), pltpu.SemaphoreType.DMA((2,))]


HBM = pl.BlockSpec(memory_space=pl.ANY)


def _glu_kernel(x_ref, meta_ref, w_hbm, b_ref, yp_ref, ybuf, w_ref, stage, sem):
    i = pl.program_id(0)

    @pl.when(i == 0)
    def _():
        _load_weights_bf16(w_hbm, w_ref, stage, sem)

    xb = _x_or_meta(i, x_ref, meta_ref).astype(bf16)
    cw = 256
    for c in range(D // cw):
        lo, hi = c * cw, (c + 1) * cw
        a = jnp.dot(xb, w_ref[:, lo:hi], preferred_element_type=f32) + b_ref[:, lo:hi]
        g = jnp.dot(xb, w_ref[:, D + lo:D + hi], preferred_element_type=f32) + b_ref[:, D + lo:D + hi]
        ybuf[:, lo:hi] = a * jax.nn.sigmoid(g)
    _pack_rows(ybuf[...], yp_ref)


def _glu(x2d, meta_pad, w_in, b_in):
    return pl.pallas_call(
        _glu_kernel,
        grid=(NT,),
        in_specs=[
            pl.BlockSpec((TM, D), lambda i: (jnp.minimum(i, NXT - 1), 0)),
            _resident((TM, D)),
            HBM,
            _resident((1, 2 * D)),
        ],
        out_specs=pl.BlockSpec((TM * PK, LANES), lambda i: (i, 0)),
        out_shape=jax.ShapeDtypeStruct((TP * PK, LANES), u32),
        scratch_shapes=[pltpu.VMEM((TM, D), f32)] + _weight_scratch(D, 2 * D, 256),
        compiler_params=_cparams(),
        name="glu",
    )(x2d, meta_pad, w_in, b_in)


APITCH = TM + 8


def _chunk_row(c):
    return 2 * (c % PK) + c // PK


def _conv_kernel(yp_ref, ymeta_ref, w_ref, bdw_ref, g_ref, b_ref, z_ref, scr, accs):
    i = pl.program_id(0)
    hrows = HALO * PK
    trows = TM * PK

    @pl.when(i == 0)
    def _():
        scr[0:(HALO - N_META) * PK, :] = jnp.zeros(((HALO - N_META) * PK, LANES), u32)
        scr[(HALO - N_META) * PK:hrows, :] = ymeta_ref[...]

    @pl.when(i == NT - 1)
    def _():
        scr[0:hrows, :] = jnp.zeros((hrows, LANES), u32)

    @pl.when(jnp.logical_and(i > 0, i < NT - 1))
    def _():
        scr[0:hrows, :] = scr[trows:trows + hrows, :]

    scr[hrows:hrows + trows, :] = yp_ref[...]

    tb = 16
    first = (HALO - (CONV_W - 1)) * PK

    def block(t, carry):
        base = pl.multiple_of(t * (tb * PK), tb * PK)
        acc = jnp.zeros((tb, CHUNKS, LANES), f32)
        for j in range(CONV_W):
            words = scr[pl.ds(base + first + j * PK, tb * PK), :]
            sl = pltpu.bitcast(words, bf16).reshape(tb, CHUNKS, LANES)
            acc = acc + sl.astype(f32) * w_ref[j].astype(f32)[None]
        flat = acc.reshape(tb * CHUNKS, LANES)
        for tok in range(tb):
            for half in range(CHUNKS // 8):
                src = flat[tok * CHUNKS + half * 8:tok * CHUNKS + half * 8 + 8, :]
                accs[pl.ds(half * 8 * APITCH + t * tb + tok, 8, stride=APITCH), :] = src
        return carry

    lax.fori_loop(0, TM // tb, block, 0)

    rb = 16

    def finish(t, carry):
        r0 = pl.multiple_of(t * rb, rb)
        cols = [accs[pl.ds(_chunk_row(c) * APITCH + r0, rb), :] for c in range(CHUNKS)]
        v = _layer_norm(jnp.concatenate(cols, axis=1) + bdw_ref[...], g_ref[...], b_ref[...])
        z_ref[pl.ds(r0, rb), :] = (v * jax.nn.sigmoid(v)).astype(bf16)
        return carry

    lax.fori_loop(0, TM // rb, finish, 0, unroll=8)


def _conv(yp, w_dw3, b_dw, ln_g, ln_b):
    return pl.pallas_call(
        _conv_kernel,
        grid=(NT,),
        in_specs=[
            pl.BlockSpec((TM * PK, LANES), lambda i: (i, 0)),
            pl.BlockSpec((N_META * PK, LANES), lambda i: (META_ROW // N_META, 0)),
            _resident((CONV_W, CHUNKS, LANES)),
            _resident((1, D)),
            _resident((1, D)),
            _resident((1, D)),
        ],
        out_specs=pl.BlockSpec((TM, D), lambda i: (i, 0)),
        out_shape=jax.ShapeDtypeStruct((TP, D), bf16),
        scratch_shapes=[
            pltpu.VMEM(((TM + HALO) * PK, LANES), u32),
            pltpu.VMEM((CHUNKS * APITCH, LANES), f32),
        ],
        compiler_params=_cparams(),
        name="conv_ln_swish",
    )(yp, yp, w_dw3, b_dw, ln_g, ln_b)


R_EXP0 = 8


def _route(h, wr_ref, br_ref, running, valid):
    logits = jnp.dot(h.astype(bf16), wr_ref[...], preferred_element_type=f32) + br_ref[...]
    lt = logits.T
    gl = [lt[k:k + 1, :] for k in range(N_GROUPS)]
    gm = functools.reduce(jnp.maximum, gl)
    gex = [jnp.exp(v - gm) for v in gl]
    gden = functools.reduce(lambda a, b: a + b, gex)
    gp = [v / gden for v in gex]
    best = gp[0]
    gi = jnp.zeros((1, TM), i32)
    for k in range(1, N_GROUPS):
        better = gp[k] > best
        gi = jnp.where(better, k, gi)
        best = jnp.where(better, gp[k], best)
    esel = lt[R_EXP0:R_EXP0 + EPG, :]
    for k in range(1, N_GROUPS):
        esel = jnp.where(gi == k, lt[R_EXP0 + EPG * k:R_EXP0 + EPG * (k + 1), :], esel)
    em = jnp.max(esel, axis=0, keepdims=True)
    eex = jnp.exp(esel - em)
    ep = eex / jnp.sum(eex, axis=0, keepdims=True)
    io8 = lax.broadcasted_iota(i32, (EPG, TM), 0)
    v1 = jnp.max(ep, axis=0, keepdims=True)
    i1 = jnp.min(jnp.where(ep == v1, io8, EPG), axis=0, keepdims=True)
    ep2 = jnp.where(io8 == i1, -1.0, ep)
    v2 = jnp.max(ep2, axis=0, keepdims=True)
    i2 = jnp.min(jnp.where(ep2 == v2, io8, EPG), axis=0, keepdims=True)
    s = v1 + v2
    gate0 = best * (v1 / s)
    gate1 = best * (v2 / s)
    f0 = gi * EPG + i1
    f1 = gi * EPG + i2

    io32 = lax.broadcasted_iota(i32, (N_EXP, TM), 0)
    oh0 = (io32 == f0).astype(f32)
    oh1 = (io32 == f1).astype(f32)
    cnt = oh0 + oh1
    upper = (lax.broadcasted_iota(i32, (TM, TM), 0) < lax.broadcasted_iota(i32, (TM, TM), 1))
    before = jnp.dot(cnt.astype(bf16), upper.astype(f32).astype(bf16), preferred_element_type=f32)
    base = running[...] + before
    r0 = jnp.sum(oh0 * base, axis=0, keepdims=True).astype(i32)
    r1 = jnp.sum(oh1 * base, axis=0, keepdims=True).astype(i32)
    running[...] = running[...] + valid * jnp.sum(cnt, axis=1, keepdims=True)

    io128 = lax.broadcasted_iota(i32, (LANES, TM), 0)
    gcol = jnp.where(io128 == 0, gate0, jnp.where(io128 == 1, gate1, 0.0)).T
    return f0, f1, r0, r1, gcol


def _proj_ln_route_epilogue(valid, a, res, bias_ref, g_ref, b_ref, wr_ref, br_ref,
                            h_ref, hp_ref, eidx_ref, rank_ref, gcol_ref, cnt_ref, running):
    mix = a + bias_ref[...]
    h = _layer_norm(ALPHA * res + mix, g_ref[...], b_ref[...])
    h_ref[...] = h
    _pack_rows(h, hp_ref)
    f0, f1, r0, r1, gcol = _route(h, wr_ref, br_ref, running, valid)
    eidx_ref[0, 0:1, :] = f0
    eidx_ref[0, 1:2, :] = f1
    rank_ref[0, 0:1, :] = r0
    rank_ref[0, 1:2, :] = r1
    gcol_ref[...] = gcol
    cnt_ref[...] = running[...]


def _proj_ln_route_kernel(first, a_ref, *refs):
    if first:
        x_ref, meta_ref, w_hbm, *rest = refs
    else:
        res_ref, w_hbm, *rest = refs
    *rest, running, accbuf, w_ref, stage, sem = rest
    i = pl.program_id(0)

    @pl.when(i == 0)
    def _():
        running[...] = jnp.zeros_like(running)
        accbuf[1] = jnp.zeros((TM, D), f32)
        _load_weights_bf16(w_hbm, w_ref, stage, sem)

    valid = (i > 0).astype(f32)
    for parity in range(2):
        @pl.when(lax.rem(i, 2) == parity)
        def _():
            prev = accbuf[1 - parity]
            if first:
                accbuf[parity] = jnp.dot(a_ref[...], w_ref[...], preferred_element_type=f32)
                res = jnp.where(i - 1 == NXT, meta_ref[...], x_ref[...])
            else:
                accbuf[parity] = lax.dot_general(a_ref[...], w_ref[...], (((0,), (0,)), ((), ())),
                                                 preferred_element_type=f32)
                res = res_ref[...]
            _proj_ln_route_epilogue(valid, prev, res, *rest, running)


def _proj_ln_route(a, res, w, bias, ln_g, ln_b, wr, br):
    first = isinstance(res, tuple)
    cur = lambda i: jnp.minimum(i, NT - 1)
    prv = lambda i: jnp.maximum(i - 1, 0)
    if first:
        a_spec = pl.BlockSpec((TM, D), lambda i: (cur(i), 0))
        res_specs = [pl.BlockSpec((TM, D), lambda i: (jnp.minimum(prv(i), NXT - 1), 0)), _resident((TM, D))]
        res_args = list(res)
    else:
        a_spec = pl.BlockSpec((D, TM), lambda i: (0, cur(i)))
        res_specs = [pl.BlockSpec((TM, D), lambda i: (prv(i), 0))]
        res_args = [res]
    tile3 = pl.BlockSpec((1, 2, TM), lambda i: (prv(i), 0, 0))
    return pl.pallas_call(
        functools.partial(_proj_ln_route_kernel, first),
        grid=(NT + 1,),
        in_specs=[a_spec] + res_specs + [
            HBM, _resident((1, D)), _resident((1, D)), _resident((1, D)),
            _resident((D, LANES)), _resident((1, LANES)),
        ],
        out_specs=[
            pl.BlockSpec((TM, D), lambda i: (prv(i), 0)),
            pl.BlockSpec((TM * PK, LANES), lambda i: (prv(i), 0)),
            tile3, tile3,
            pl.BlockSpec((TM, LANES), lambda i: (prv(i), 0)),
            pl.BlockSpec((N_EXP, TM), lambda i: (0, 0)),
        ],
        out_shape=[
            jax.ShapeDtypeStruct((TP, D), f32),
            jax.ShapeDtypeStruct((TP * PK, LANES), u32),
            jax.ShapeDtypeStruct((NT, 2, TM), i32),
            jax.ShapeDtypeStruct((NT, 2, TM), i32),
            jax.ShapeDtypeStruct((TP, LANES), f32),
            jax.ShapeDtypeStruct((N_EXP, TM), f32),
        ],
        scratch_shapes=[pltpu.VMEM((N_EXP, TM), f32), pltpu.VMEM((2, TM, D), f32)] + _weight_scratch(D, D, 512),
        compiler_params=_cparams(),
        name="proj_ln_route",
    )(a, *res_args, w, bias, ln_g, ln_b, wr, br)


def _plan(eidx, rank, cnt):
    counts = cnt[:, 0].astype(i32)
    used = ((counts + TME - 1) // TME) * TME
    padded = ((counts + EBLK - 1) // EBLK) * EBLK
    ends = jnp.cumsum(padded)
    offs = ends - padded
    ntiles = ends[-1] // EBLK
    off_of = jnp.sum(jnp.where(eidx[..., None] == jnp.arange(N_EXP, dtype=i32), offs, 0), axis=-1)
    dest = (off_of + rank).reshape(-1)
    tile_start = jnp.minimum(jnp.arange(NTE, dtype=i32), ntiles - 1) * EBLK
    tile_expert = jnp.minimum(jnp.sum(tile_start[:, None] >= ends[None, :], axis=1), N_EXP - 1).astype(i32)
    nsub = jnp.clip((offs[tile_expert] + used[tile_expert] - tile_start) // TME, 0, ESUB).astype(i32)
    zstart = jnp.where(used > 0, offs + used - TME, 0).astype(i32)
    zflag = (used > 0).astype(i32)
    tid = jnp.arange(NTE, dtype=i32)
    live = tid < ntiles
    first = jnp.logical_and(live, jnp.logical_or(tid == 0, tile_expert != jnp.roll(tile_expert, 1)))
    slot = (jnp.cumsum(first.astype(i32)) - 1) % 2
    nxt_first = lax.cummin(jnp.where(first, tid, NTE), reverse=True)
    after = jnp.concatenate([nxt_first[1:], jnp.full((1,), NTE, i32)])
    nxt = jnp.where(after < NTE, tile_expert[jnp.minimum(after, NTE - 1)], -1)
    return (dest.astype(i32), tile_expert, ntiles.reshape(1).astype(i32), zstart, zflag,
            first.astype(i32), slot.astype(i32), nxt.astype(i32), nsub)


ISSUE_UNROLL = 8


def _scatter_kernel(dest_ref, zstart_ref, zflag_ref, hp_ref, xs_hbm, zeros, sem, zsem):
    i = pl.program_id(0)

    @pl.when(i == 0)
    def _():
        zeros[...] = jnp.zeros_like(zeros)
        def fill(e):
            start = pl.multiple_of(zstart_ref[e], TME)
            return pltpu.make_async_copy(zeros, xs_hbm.at[pl.ds(start, TME)], zsem)

        for e in range(N_EXP):
            @pl.when(zflag_ref[e] > 0)
            def _():
                fill(e).start()
        for e in range(N_EXP):
            @pl.when(zflag_ref[e] > 0)
            def _():
                fill(e).wait()

    base = i * (2 * TM)

    def row(r, carry):
        for k in range(2):
            d = dest_ref[base + k * TM + r]
            pltpu.make_async_copy(hp_ref.at[pl.ds(r, 1)], xs_hbm.at[pl.ds(d, 1)], sem).start(priority=k)
        return carry

    lax.fori_loop(0, TM, row, 0, unroll=ISSUE_UNROLL)

    def drain(r, carry):
        pltpu.make_async_copy(hp_ref.at[pl.ds(0, 1)], xs_hbm.at[pl.ds(0, 1)], sem).wait()
        return carry

    lax.fori_loop(0, 2 * TM, drain, 0, unroll=ISSUE_UNROLL)


def _scatter(dest, zstart, zflag, hp3):
    return pl.pallas_call(
        _scatter_kernel,
        grid_spec=pltpu.PrefetchScalarGridSpec(
            num_scalar_prefetch=3,
            grid=(NT,),
            in_specs=[pl.BlockSpec((TM, PK, LANES), lambda i, *_: (i, 0, 0))],
            out_specs=pl.BlockSpec(memory_space=pl.ANY),
            scratch_shapes=[pltpu.VMEM((TME, PK, LANES), u32), pltpu.SemaphoreType.DMA(()),
                            pltpu.SemaphoreType.DMA(())],
        ),
        out_shape=jax.ShapeDtypeStruct((NS, PK, LANES), u32),
        compiler_params=pltpu.CompilerParams(dimension_semantics=("arbitrary",), vmem_limit_bytes=VMEM_LIMIT,
                                             has_side_effects=True),
        name="moe_scatter",
    )(dest, zstart, zflag, hp3)


def _expert_kernel(layer, te_ref, nt_ref, first_ref, slot_ref, nxt_ref, nsub_ref, xs_ref, w1_hbm, w3_hbm, w2_hbm,
                   ys_ref, wb1, wb3, wb2, w1c, w3c, w2c, sem):
    i = pl.program_id(0)

    def copies(e, s):
        return (pltpu.make_async_copy(w1_hbm.at[layer, e], wb1.at[s], sem.at[s]),
                pltpu.make_async_copy(w3_hbm.at[layer, e], wb3.at[s], sem.at[s]),
                pltpu.make_async_copy(w2_hbm.at[layer, e], wb2.at[s], sem.at[s]))

    @pl.when(i == 0)
    def _():
        for cp in copies(te_ref[0], 0):
            cp.start()

    @pl.when(jnp.logical_and(i < nt_ref[0], first_ref[i] > 0))
    def _():
        s = slot_ref[i]
        for cp in copies(te_ref[i], s):
            cp.wait()

        @pl.when(nxt_ref[i] >= 0)
        def _():
            for cp in copies(nxt_ref[i], 1 - s):
                cp.start()

        w1c[...] = wb1[s].astype(bf16)
        w3c[...] = wb3[s].astype(bf16)
        w2c[...] = wb2[s].astype(bf16)

    for sub in range(ESUB):
        @pl.when(jnp.logical_and(i < nt_ref[0], sub < nsub_ref[i]))
        def _():
            rows = pl.ds(sub * TME * PK, TME * PK)
            xlo, xhi = _unpack_rows(xs_ref.at[rows], TME, bf16)
            a = (jnp.dot(xlo, w1c[0:HALF, :], preferred_element_type=f32)
                 + jnp.dot(xhi, w1c[HALF:D, :], preferred_element_type=f32))
            b = (jnp.dot(xlo, w3c[0:HALF, :], preferred_element_type=f32)
                 + jnp.dot(xhi, w3c[HALF:D, :], preferred_element_type=f32))
            hid = (a * jax.nn.sigmoid(a) * b).astype(bf16)
            _pack_rows(jnp.dot(hid, w2c[...], preferred_element_type=f32), ys_ref.at[rows])


def _experts(layer, tile_expert, ntiles, first, slot, nxt, nsub, xs2d, w1, w3, w2):
    def row_map(i, te, nt, *_):
        return (jnp.minimum(i, nt[0] - 1), 0)

    hbm = pl.BlockSpec(memory_space=pl.ANY)
    return pl.pallas_call(
        functools.partial(_expert_kernel, layer),
        grid_spec=pltpu.PrefetchScalarGridSpec(
            num_scalar_prefetch=6,
            grid=(NTE,),
            in_specs=[pl.BlockSpec((EBLK * PK, LANES), row_map), hbm, hbm, hbm],
            out_specs=pl.BlockSpec((EBLK * PK, LANES), row_map),
            scratch_shapes=[
                pltpu.VMEM((2, D, FF), f32), pltpu.VMEM((2, D, FF), f32), pltpu.VMEM((2, FF, D), f32),
                pltpu.VMEM((D, FF), bf16), pltpu.VMEM((D, FF), bf16), pltpu.VMEM((FF, D), bf16),
                pltpu.SemaphoreType.DMA((2,)),
            ],
        ),
        out_shape=jax.ShapeDtypeStruct((NS * PK, LANES), u32),
        compiler_params=_cparams(),
        name="moe_experts",
    )(tile_expert, ntiles, first, slot, nxt, nsub, xs2d, w1, w3, w2)


def _gather_combine(n, dest_ref, ys_hbm, h_ref, gcol_ref, g_ref, b_ref, buf, sem):
    i = pl.program_id(0)
    slot = lax.rem(i, 2)

    def issue(tile, s):
        base = tile * (2 * TM)

        def row(r, carry):
            for k in range(2):
                d = pl.multiple_of(dest_ref[base + k * TM + r] * PK, PK)
                pltpu.make_async_copy(ys_hbm.at[pl.ds(d, PK)],
                                      buf.at[s, k, pl.ds(pl.multiple_of(r * PK, PK), PK)],
                                      sem.at[s]).start(priority=k)
            return carry

        lax.fori_loop(0, TM, row, 0, unroll=ISSUE_UNROLL)

    @pl.when(i == 0)
    def _():
        issue(0, 0)

    @pl.when(i + 1 < n)
    def _():
        issue(i + 1, 1 - slot)

    def drain(r, carry):
        pltpu.make_async_copy(ys_hbm.at[pl.ds(0, PK)], buf.at[slot, 0, pl.ds(0, PK)], sem.at[slot]).wait()
        return carry

    lax.fori_loop(0, 2 * TM, drain, 0, unroll=ISSUE_UNROLL)

    lo0, hi0 = _unpack_rows(buf.at[slot, 0], TM, f32)
    lo1, hi1 = _unpack_rows(buf.at[slot, 1], TM, f32)
    g0, g1 = gcol_ref[:, 0:1], gcol_ref[:, 1:2]
    ffn = jnp.concatenate([lo0 * g0 + lo1 * g1, hi0 * g0 + hi1 * g1], axis=1)
    return _layer_norm(ALPHA * h_ref[...] + ffn, g_ref[...], b_ref[...])


def _combine_kernel(n, dest_ref, ys_hbm, h_ref, gcol_ref, g_ref, b_ref, o_ref, buf, sem):
    o_ref[...] = _gather_combine(n, dest_ref, ys_hbm, h_ref, gcol_ref, g_ref, b_ref, buf, sem)


def _combine(dest, ys2d, h, gcol, ln_g, ln_b, ntiles_out):
    return pl.pallas_call(
        functools.partial(_combine_kernel, ntiles_out),
        grid_spec=pltpu.PrefetchScalarGridSpec(
            num_scalar_prefetch=1,
            grid=(ntiles_out,),
            in_specs=[
                pl.BlockSpec(memory_space=pl.ANY),
                pl.BlockSpec((TM, D), lambda i, *_: (i, 0)),
                pl.BlockSpec((TM, LANES), lambda i, *_: (i, 0)),
                pl.BlockSpec((1, D), lambda i, *_: (0, 0)),
                pl.BlockSpec((1, D), lambda i, *_: (0, 0)),
            ],
            out_specs=pl.BlockSpec((TM, D), lambda i, *_: (i, 0)),
            scratch_shapes=[pltpu.VMEM((2, 2, TM * PK, LANES), u32), pltpu.SemaphoreType.DMA((2,))],
        ),
        out_shape=jax.ShapeDtypeStruct((ntiles_out * TM, D), f32),
        compiler_params=_cparams(),
        name="moe_combine_ln",
    )(dest, ys2d, h, gcol, ln_g, ln_b)


def _moe_experts(layer, hp2d, eidx, rank, cnt, w1, w3, w2):
    dest, tile_expert, ntiles, zstart, zflag, first, slot, nxt, nsub = _plan(eidx, rank, cnt)
    xs = _scatter(dest, zstart, zflag, hp2d.reshape(TP, PK, LANES))
    ys2d = _experts(layer, tile_expert, ntiles, first, slot, nxt, nsub, xs.reshape(NS * PK, LANES), w1, w3, w2)
    return dest, ys2d


NT_DIMS = (((1,), (1,)), ((), ()))


def _rope_rows(t, cos, sa, sb):
    w = t.shape[1]
    reps = w // LANES
    c = jnp.tile(cos, (1, reps))
    a = jnp.tile(sa, (1, reps))
    b = jnp.tile(sb, (1, reps))
    return t * c + pltpu.roll(t, w - ROT // 2, 1) * a + pltpu.roll(t, ROT // 2, 1) * b


def _combine_qkv_kernel(n, dest_ref, ys_hbm, h_ref, gcol_ref, g_ref, b_ref,
                        wq_hbm, bq_ref, wk_ref, bk_ref, wvT_ref, bv_ref, cosT_ref, sinT_ref, cos_ref, sa_ref, sb_ref,
                        h2_ref, qT_ref, k_ref, vT_ref, buf, gsem, wq_ref, stage, wsem):
    @pl.when(pl.program_id(0) == 0)
    def _():
        _load_weights_bf16(wq_hbm, wq_ref, stage, wsem)

    h2 = _gather_combine(n, dest_ref, ys_hbm, h_ref, gcol_ref, g_ref, b_ref, buf, gsem)
    h2_ref[...] = h2
    hb = h2.astype(bf16)
    scale = LOG2E / math.sqrt(HEAD_DIM)
    half = ROT // 2
    cosT = cosT_ref[...][None]
    sinT = sinT_ref[...][None]
    rows = GQA * HEAD_DIM
    for c in range(D // rows):
        lo, hi = c * rows, (c + 1) * rows
        t = lax.dot_general(wq_ref[:, lo:hi], hb, (((0,), (1,)), ((), ())), preferred_element_type=f32) \
            + bq_ref[lo:hi, :]
        t3 = t.reshape(GQA, HEAD_DIM, TM)
        x1, x2 = t3[:, 0:half, :], t3[:, half:ROT, :]
        r = jnp.concatenate([x1 * cosT - x2 * sinT, x2 * cosT + x1 * sinT, t3[:, ROT:, :]], axis=1)
        qT_ref[lo:hi, :] = (r * scale).reshape(rows, TM).astype(bf16)
    t = jnp.dot(hb, wk_ref[...], preferred_element_type=f32) + bk_ref[...]
    k_ref[...] = _rope_rows(t, cos_ref[...], sa_ref[...], sb_ref[...]).astype(bf16)
    t = lax.dot_general(wvT_ref[...], hb, NT_DIMS, preferred_element_type=f32) + bv_ref[...]
    vT_ref[...] = t.astype(bf16)


def _combine_qkv(dest, ys2d, h, gcol, ln_g, ln_b, wq, bq_col, wk, bk, wvT, bv_col, tables):
    cosT, sinT, cos_t, sa_t, sb_t = tables
    const = lambda shape: pl.BlockSpec(shape, lambda i, *_: (0,) * len(shape))
    tabT = pl.BlockSpec((ROT // 2, TM), lambda i, *_: (0, i))
    tab = pl.BlockSpec((TM, LANES), lambda i, *_: (i, 0))
    return pl.pallas_call(
        functools.partial(_combine_qkv_kernel, NT),
        grid_spec=pltpu.PrefetchScalarGridSpec(
            num_scalar_prefetch=1,
            grid=(NT,),
            in_specs=[
                HBM,
                pl.BlockSpec((TM, D), lambda i, *_: (i, 0)),
                pl.BlockSpec((TM, LANES), lambda i, *_: (i, 0)),
                const((1, D)), const((1, D)),
                HBM, const((D, 1)),
                const((D, KVW)), const((1, KVW)),
                const((KVW, D)), const((KVW, 1)),
                tabT, tabT, tab, tab, tab,
            ],
            out_specs=[
                pl.BlockSpec((TM, D), lambda i, *_: (i, 0)),
                pl.BlockSpec((D, TM), lambda i, *_: (0, i)),
                pl.BlockSpec((TM, KVW), lambda i, *_: (i, 0)),
                pl.BlockSpec((KVW, TM), lambda i, *_: (0, i)),
            ],
            scratch_shapes=[pltpu.VMEM((2, 2, TM * PK, LANES), u32), pltpu.SemaphoreType.DMA((2,))]
            + _weight_scratch(D, D, 512),
        ),
        out_shape=[
            jax.ShapeDtypeStruct((TP, D), f32),
            jax.ShapeDtypeStruct((D, TP), bf16),
            jax.ShapeDtypeStruct((TP, KVW), bf16),
            jax.ShapeDtypeStruct((KVW, TP), bf16),
        ],
        compiler_params=_cparams(),
        name="moe_combine_ln_qkv_rope",
    )(dest, ys2d, h, gcol, ln_g, ln_b, wq, bq_col, wk, bk, wvT, bv_col, cosT, sinT, cos_t, sa_t, sb_t)


NKEY = 2 * QB + N_META
HC = 8
LW = HC * QB
SUB = 8


def _col_max(s):
    parts = [s[r * SUB:(r + 1) * SUB] for r in range(NKEY // SUB)]
    while len(parts) > 1:
        nxt = [jnp.maximum(parts[j], parts[j + 1]) for j in range(0, len(parts) - 1, 2)]
        if len(parts) % 2:
            nxt.append(parts[-1])
        parts = nxt
    return jnp.max(parts[0], axis=0, keepdims=True)


def _attn_kernel(qT_ref, kc_ref, kp_ref, km_ref, vTc_ref, vTp_ref, vTm_ref, sink_ref, oT_ref):
    i = pl.program_id(0)
    is_meta = i == NT - 1
    ck = lax.broadcasted_iota(i32, (NKEY, QB), 0)
    rq = lax.broadcasted_iota(i32, (NKEY, QB), 1)
    in_band = jnp.logical_and(ck > rq, ck <= rq + QB)
    meta_ok = jnp.logical_and(ck >= 2 * QB, jnp.logical_or(jnp.logical_not(is_meta), ck - 2 * QB <= rq))
    ones = jnp.ones((SUB, NKEY), bf16)

    for blk in range(TM // QB):
        lo = jnp.where(is_meta, 2 * QB, jnp.where(jnp.logical_and(i == 0, blk == 0), QB, 0))
        valid = jnp.logical_or(meta_ok, jnp.logical_and(in_band, ck >= lo))
        bias = jnp.where(valid, 0.0, -jnp.inf)
        bias = jnp.concatenate([bias] * HC, axis=1)
        c0 = blk * QB
        for g in range(N_KV):
            gs = slice(g * HEAD_DIM, (g + 1) * HEAD_DIM)
            if blk == 0:
                kprev, vprevT = kp_ref[:, gs], vTp_ref[gs, :]
            else:
                kprev, vprevT = kc_ref[c0 - QB:c0, gs], vTc_ref[gs, c0 - QB:c0]
            kcat = jnp.concatenate([kprev, kc_ref[c0:c0 + QB, gs], km_ref[:, gs]], axis=0)
            vcatT = jnp.concatenate([vprevT, vTc_ref[gs, c0:c0 + QB], vTm_ref[gs, 0:N_META]], axis=1)
            vext = jnp.concatenate([vcatT, ones], axis=0)
            for c in range(GQA // HC):
                h0 = g * GQA + c * HC
                heads = [qT_ref[(h0 + j) * HEAD_DIM:(h0 + j + 1) * HEAD_DIM, c0:c0 + QB] for j in range(HC)]
                s = jnp.dot(kcat, jnp.concatenate(heads, axis=1), preferred_element_type=f32) + bias
                sink = sink_ref[h0 // HC:h0 // HC + 1, :]
                m = jnp.maximum(_col_max(s), sink)
                p = jnp.exp2(s - m).astype(bf16)
                oe = jnp.dot(vext, p, preferred_element_type=f32)
                den = oe[HEAD_DIM:HEAD_DIM + 1, :] + jnp.exp2(sink - m)
                o = (oe[0:HEAD_DIM, :] * (1.0 / den)).astype(bf16)
                for j in range(HC):
                    oT_ref[(h0 + j) * HEAD_DIM:(h0 + j + 1) * HEAD_DIM, c0:c0 + QB] = o[:, j * QB:(j + 1) * QB]


def _attention(qT, k, vT, sink_lanes):
    prev_blk = lambda i: jnp.maximum(i * (TM // QB) - 1, 0)
    return pl.pallas_call(
        _attn_kernel,
        grid=(NT,),
        in_specs=[
            pl.BlockSpec((D, TM), lambda i: (0, i)),
            pl.BlockSpec((TM, KVW), lambda i: (i, 0)),
            pl.BlockSpec((QB, KVW), lambda i: (prev_blk(i), 0)),
            pl.BlockSpec((N_META, KVW), lambda i: (META_ROW // N_META, 0)),
            pl.BlockSpec((KVW, TM), lambda i: (0, i)),
            pl.BlockSpec((KVW, QB), lambda i: (0, prev_blk(i))),
            pl.BlockSpec((KVW, LANES), lambda i: (0, META_ROW // LANES)),
            _resident((N_HEADS // HC, LW)),
        ],
        out_specs=pl.BlockSpec((D, TM), lambda i: (0, i)),
        out_shape=jax.ShapeDtypeStruct((D, TP), bf16),
        compiler_params=_cparams(),
        name="swa_attention",
    )(qT, k, k, k, vT, vT, vT, sink_lanes)


def _router_weights(wg, bg, we, be):
    gap = R_EXP0 - N_GROUPS
    tail = LANES - R_EXP0 - N_EXP
    wr = jnp.concatenate([wg, jnp.zeros((D, gap), f32), we, jnp.zeros((D, tail), f32)], axis=1)
    br = jnp.concatenate([bg, jnp.zeros((gap,), f32), be, jnp.zeros((tail,), f32)]).reshape(1, LANES)
    return wr.astype(bf16), br


def _rope_tables():
    pos = np.concatenate([np.arange(SEQ) + N_META, np.arange(TM)]).astype(np.float32)
    half = ROT // 2
    inv_freq = (np.float32(ROPE_THETA) ** (-np.arange(0, ROT, 2, dtype=np.float32) / np.float32(ROT)))
    ang = pos[:, None] * inv_freq.astype(np.float32)[None, :]
    cos, sin = np.cos(ang).astype(np.float32), np.sin(ang).astype(np.float32)
    ones = np.ones((TP, HEAD_DIM - ROT), np.float32)
    zeros = np.zeros((TP, HEAD_DIM - ROT), np.float32)
    z8 = np.zeros((TP, half), np.float32)
    cos_h = np.concatenate([cos, cos, ones], axis=1)
    sa_h = np.concatenate([-sin, z8, zeros], axis=1)
    sb_h = np.concatenate([z8, sin, zeros], axis=1)
    rep = LANES // HEAD_DIM
    tabs = (cos.T, sin.T, np.tile(cos_h, (1, rep)), np.tile(sa_h, (1, rep)), np.tile(sb_h, (1, rep)))
    return tuple(jnp.asarray(np.ascontiguousarray(t)) for t in tabs)


def kernel(x, meta_tokens, conv_w_in, conv_b_in, conv_w_dw, conv_b_dw, conv_ln_g, conv_ln_b, conv_w_out,
           conv_b_out, w_k, b_k, w_v, b_v, w_q, b_q, w_o, b_o, sinks, ln_mix_g, ln_mix_b, ln_ffn_g, ln_ffn_b,
           router_group_w, router_group_b, router_expert_w, router_expert_b, expert_w1, expert_w3, expert_w2):
    assert x.shape == (1, SEQ, D)
    row = lambda v: v.reshape(1, -1)
    col = lambda v: v.reshape(-1, 1)
    x2d = x.reshape(SEQ, D)
    meta_pad = jnp.pad(meta_tokens.astype(f32), ((0, TM - N_META), (0, 0)))

    y = _glu(x2d, meta_pad, conv_w_in[0], row(conv_b_in[0]))
    w_dw = conv_w_dw[0].reshape(CONV_W, 2, PK, LANES).transpose(0, 2, 1, 3).reshape(CONV_W, CHUNKS, LANES)
    z = _conv(y, w_dw.astype(bf16), row(conv_b_dw[0]), row(conv_ln_g[0]),
              row(conv_ln_b[0]))
    wr, br = _router_weights(router_group_w[0], router_group_b[0], router_expert_w[0], router_expert_b[0])
    h, hp, eidx, rank, gcol, cnt = _proj_ln_route(
        z, (x2d, meta_pad), conv_w_out[0], row(conv_b_out[0]), row(ln_mix_g[0]), row(ln_mix_b[0]),
        wr, br)
    dest, ys2d = _moe_experts(0, hp, eidx, rank, cnt, expert_w1, expert_w3, expert_w2)

    h, qT, k, vT = _combine_qkv(dest, ys2d, h, gcol, row(ln_ffn_g[0]), row(ln_ffn_b[0]),
                                w_q[0], col(b_q[0]), w_k.astype(bf16), row(b_k),
                                w_v.T.astype(bf16), col(b_v), _rope_tables())
    sink_lanes = jnp.repeat((sinks[0].astype(f32) * LOG2E).reshape(N_HEADS // HC, HC), QB, axis=1)
    attT = _attention(qT, k, vT, sink_lanes)
    wr, br = _router_weights(router_group_w[1], router_group_b[1], router_expert_w[1], router_expert_b[1])
    h, hp, eidx, rank, gcol, cnt = _proj_ln_route(
        attT, h, w_o[0], row(b_o[0]), row(ln_mix_g[1]), row(ln_mix_b[1]), wr, br)
    dest, ys2d = _moe_experts(1, hp, eidx, rank, cnt, expert_w1, expert_w3, expert_w2)
    out = _combine(dest, ys2d, h, gcol, row(ln_ffn_g[1]), row(ln_ffn_b[1]), NXT)
    return out.reshape(1, SEQ, D)
```
